```python
import math
import jax, jax.numpy as jnp
from jax import lax
import numpy as np

D_MODEL = 1024
BATCH = 2
SEQ = 8192
DEPTH = 1
DEC_BATCH = 8
DEC_SEQ = 4096
PAST_LEN = 128

HEAD_DIM = 64
DA_HEADS = 4
DA_QK = DA_HEADS * 2 * HEAD_DIM
DA_V = DA_HEADS * 2 * HEAD_DIM
WA_Q_HEADS = 8
WA_KV_HEADS = 2
WA_REP = WA_Q_HEADS // WA_KV_HEADS
WA_Q = WA_Q_HEADS * HEAD_DIM
WA_KV = WA_KV_HEADS * HEAD_DIM
WINDOW = 128
BLOCK = 128
C_DA_Q = 0
C_DA_K = C_DA_Q + DA_QK
C_DA_V = C_DA_K + DA_QK
C_WA_Q = C_DA_V + DA_V
C_WA_K = C_WA_Q + WA_Q
C_WA_V = C_WA_K + WA_KV
IN_COLS = C_WA_V + WA_KV
MIX_WIDTH = DA_V + WA_Q
N_BUCKETS = 32
MAX_DISTANCE = 128
N_BIAS_HEADS = DA_HEADS + WA_Q_HEADS
N_EXPERTS = 16
EXPERT_FF = 1024
CAPACITY_FACTOR = 2
RMS_EPS = 1e-6
SUBLN_EPS = 1e-5

kernel_name = "hybrid_diffattn_swa_ec_encoder"


def rmsnorm(x, g, eps=RMS_EPS):
    xf = x.astype(jnp.float32)
    y = xf * lax.rsqrt(jnp.mean(xf * xf, axis=-1, keepdims=True) + eps)
    return (y * g.astype(jnp.float32)).astype(x.dtype)


def t5_bucket(rel):
    nb = N_BUCKETS // 2
    max_exact = nb // 2
    ret = jnp.where(rel > 0, nb, 0)
    n = jnp.abs(rel)
    nf = jnp.maximum(n, 1).astype(jnp.float32)
    large = max_exact + (jnp.log(nf / max_exact) / math.log(MAX_DISTANCE / max_exact)
                         * (nb - max_exact)).astype(jnp.int32)
    large = jnp.minimum(large, nb - 1)
    return ret + jnp.where(n < max_exact, n, large)


def diff_attention(q, k, v, rel_bias_da, lam, lambda_init, subln_g):
    B, S = q.shape[0], q.shape[1]
    nblk = S // BLOCK
    scale = HEAD_DIM ** -0.5
    qb = q.reshape(B, nblk, BLOCK, DA_HEADS, 2, HEAD_DIM).transpose(1, 0, 2, 3, 4, 5)
    starts = jnp.arange(nblk, dtype=jnp.int32) * BLOCK
    kpos = jnp.arange(S, dtype=jnp.int32)

    def one_block(args):
        qblk, start = args
        qpos = start + jnp.arange(BLOCK, dtype=jnp.int32)
        bias = rel_bias_da[t5_bucket(kpos[None, :] - qpos[:, None])]
        bias = bias.astype(jnp.float32).transpose(2, 0, 1)
        logits = jnp.einsum('bqhmd,bkhmd->bhmqk', qblk, k).astype(jnp.float32) * scale
        logits = logits + bias[None, :, None]
        p = jax.nn.softmax(logits, axis=-1)
        attn = p[:, :, 0] - lam * p[:, :, 1]
        return jnp.einsum('bhqk,bkhe->bqhe', attn.astype(v.dtype), v)

    out = lax.map(one_block, (qb, starts))
    out = out.transpose(1, 0, 2, 3, 4).reshape(B, S, DA_HEADS, 2 * HEAD_DIM)
    out = rmsnorm(out, subln_g, SUBLN_EPS) * (1.0 - lambda_init)
    return out.reshape(B, S, DA_V).astype(q.dtype)


def windowed_gqa(q, k, v, rel_bias_wa, sink):
    B, S = q.shape[0], q.shape[1]
    nblk = S // BLOCK
    scale = HEAD_DIM ** -0.5
    qb = q.reshape(B, nblk, BLOCK, WA_KV_HEADS, WA_REP, HEAD_DIM)

    def band(t):
        tp = jnp.pad(t, ((0, 0), (WINDOW, WINDOW), (0, 0), (0, 0)))
        tp = tp.reshape(B, nblk + 2, BLOCK, WA_KV_HEADS, HEAD_DIM)
        return jnp.concatenate([tp[:, :-2], tp[:, 1:-1], tp[:, 2:]], axis=2)

    kb, vb = band(k), band(v)
    nk = 3 * BLOCK
    rel = jnp.arange(nk, dtype=jnp.int32)[None, :] - BLOCK - jnp.arange(BLOCK, dtype=jnp.int32)[:, None]
    bias = rel_bias_wa[t5_bucket(rel)].astype(jnp.float32)
    bias = bias.transpose(2, 0, 1).reshape(WA_KV_HEADS, WA_REP, BLOCK, nk)
    kpos = (jnp.arange(nblk, dtype=jnp.int32)[:, None] * BLOCK - BLOCK
            + jnp.arange(nk, dtype=jnp.int32)[None, :])
    valid = ((jnp.abs(rel) <= WINDOW)[None]
             & ((kpos >= 0) & (kpos < S))[:, None, :])
    logits = jnp.einsum('bnqgrd,bnkgd->bngrqk', qb, kb).astype(jnp.float32) * scale
    logits = logits + bias[None, None]
    logits = jnp.where(valid[None, :, None, None], logits, -jnp.inf)
    s = sink.astype(jnp.float32).reshape(1, 1, WA_KV_HEADS, WA_REP, 1, 1)
    m = jnp.maximum(jnp.max(logits, axis=-1, keepdims=True), s)
    e = jnp.exp(logits - m)
    p = e / (jnp.sum(e, axis=-1, keepdims=True) + jnp.exp(s - m))
    out = jnp.einsum('bngrqk,bnkgd->bnqgrd', p.astype(vb.dtype), vb)
    return out.reshape(B, S, WA_Q)


def expert_choice_moe(x, w_router, w_gate, w_up, w_down):
    B, S, D = x.shape
    n_tok = B * S
    cap = CAPACITY_FACTOR * n_tok // N_EXPERTS
    xt = x.reshape(n_tok, D)
    aff = jax.nn.softmax(jnp.matmul(xt, w_router).astype(jnp.float32), axis=-1)
    gate, idx = lax.top_k(aff.T, cap)
    xg = xt[idx]
    h = jax.nn.silu(jnp.einsum('ecd,edf->ecf', xg, w_gate)) * jnp.einsum('ecd,edf->ecf', xg, w_up)
    y = jnp.einsum('ecf,efd->ecd', h, w_down) * gate[..., None].astype(x.dtype)
    out = jnp.zeros_like(xt).at[idx.reshape(-1)].add(y.reshape(-1, D))
    return out.reshape(B, S, D)


def encoder_layer(x, layer, rel_bias, norm1, w_in, lam_q1, lam_k1, lam_q2, lam_k2,
                  subln_g, sink, w_o, norm2, w_router, w_gate, w_up, w_down):
    B, S, _ = x.shape
    h = rmsnorm(x, norm1)
    proj = jnp.matmul(h, w_in)
    da_q = proj[..., C_DA_Q:C_DA_K].reshape(B, S, DA_HEADS, 2, HEAD_DIM)
    da_k = proj[..., C_DA_K:C_DA_V].reshape(B, S, DA_HEADS, 2, HEAD_DIM)
    da_v = proj[..., C_DA_V:C_WA_Q].reshape(B, S, DA_HEADS, 2 * HEAD_DIM)
    wa_q = proj[..., C_WA_Q:C_WA_K].reshape(B, S, WA_Q_HEADS, HEAD_DIM)
    wa_k = proj[..., C_WA_K:C_WA_V].reshape(B, S, WA_KV_HEADS, HEAD_DIM)
    wa_v = proj[..., C_WA_V:IN_COLS].reshape(B, S, WA_KV_HEADS, HEAD_DIM)

    lambda_init = 0.8 - 0.6 * math.exp(-0.3 * layer)
    lam = (jnp.exp(jnp.sum(lam_q1.astype(jnp.float32) * lam_k1.astype(jnp.float32)))
           - jnp.exp(jnp.sum(lam_q2.astype(jnp.float32) * lam_k2.astype(jnp.float32)))
           + lambda_init)
    out_a = diff_attention(da_q, da_k, da_v, rel_bias[:, :DA_HEADS], lam, lambda_init, subln_g)
    out_b = windowed_gqa(wa_q, wa_k, wa_v, rel_bias[:, DA_HEADS:], sink)
    mixed = jnp.concatenate([out_a, out_b], axis=-1)
    x = x + jnp.matmul(mixed, w_o)
    x = x + expert_choice_moe(rmsnorm(x, norm2), w_router, w_gate, w_up, w_down)
    return x


def trunk(x, rel_bias, norm1, w_in, lam_q1, lam_k1, lam_q2, lam_k2, subln_g, sink,
          w_o, norm2, w_router, w_gate, w_up, w_down, final_norm):
    for l in range(DEPTH):
        x = encoder_layer(x, l, rel_bias, norm1[l], w_in[l], lam_q1[l], lam_k1[l],
                          lam_q2[l], lam_k2[l], subln_g[l], sink[l], w_o[l], norm2[l],
                          w_router[l], w_gate[l], w_up[l], w_down[l])
    return rmsnorm(x, final_norm)


def setup_inputs(seed: int = 0) -> dict:
    key = jax.random.key(seed)
    ks = jax.random.split(key, 20)
    f32 = jnp.float32
    nrm = lambda k, shape, s: jax.random.normal(k, shape, f32) * s
    return {
        "x_prompt": nrm(ks[0], (BATCH, SEQ, D_MODEL), 1.0),
        "x_sample": nrm(ks[1], (DEC_BATCH, DEC_SEQ, D_MODEL), 1.0),
        "rel_bias": nrm(ks[2], (N_BUCKETS, N_BIAS_HEADS), 0.1),
        "norm1": 1.0 + nrm(ks[3], (DEPTH, D_MODEL), 0.01),
        "w_in": nrm(ks[4], (DEPTH, D_MODEL, IN_COLS), D_MODEL ** -0.5),
        "lam_q1": nrm(ks[5], (DEPTH, HEAD_DIM), 0.1),
        "lam_k1": nrm(ks[6], (DEPTH, HEAD_DIM), 0.1),
        "lam_q2": nrm(ks[7], (DEPTH, HEAD_DIM), 0.1),
        "lam_k2": nrm(ks[8], (DEPTH, HEAD_DIM), 0.1),
        "subln_g": 1.0 + nrm(ks[9], (DEPTH, 2 * HEAD_DIM), 0.01),
        "sink": nrm(ks[10], (DEPTH, WA_Q_HEADS), 0.5),
        "w_o": nrm(ks[11], (DEPTH, MIX_WIDTH, D_MODEL), MIX_WIDTH ** -0.5),
        "norm2": 1.0 + nrm(ks[12], (DEPTH, D_MODEL), 0.01),
        "w_router": nrm(ks[13], (DEPTH, D_MODEL, N_EXPERTS), D_MODEL ** -0.5),
        "w_gate": nrm(ks[14], (DEPTH, N_EXPERTS, D_MODEL, EXPERT_FF), D_MODEL ** -0.5),
        "w_up": nrm(ks[15], (DEPTH, N_EXPERTS, D_MODEL, EXPERT_FF), D_MODEL ** -0.5),
        "w_down": nrm(ks[16], (DEPTH, N_EXPERTS, EXPERT_FF, D_MODEL), EXPERT_FF ** -0.5),
        "final_norm": 1.0 + nrm(ks[17], (D_MODEL,), 0.01),
    }


def reference(x_prompt, x_sample, rel_bias, norm1, w_in, lam_q1, lam_k1, lam_q2, lam_k2,
              subln_g, sink, w_o, norm2, w_router, w_gate, w_up, w_down, final_norm):
    y_prompt = trunk(x_prompt, rel_bias, norm1, w_in, lam_q1, lam_k1, lam_q2, lam_k2,
                     subln_g, sink, w_o, norm2, w_router, w_gate, w_up, w_down, final_norm)
    y_sample = trunk(x_sample, rel_bias, norm1, w_in, lam_q1, lam_k1, lam_q2, lam_k2,
                     subln_g, sink, w_o, norm2, w_router, w_gate, w_up, w_down, final_norm)
    return (y_prompt, y_sample)
```

```python
import functools
import math

import jax
import jax.numpy as jnp
from jax import lax
from jax.experimental import pallas as pl
from jax.experimental.pallas import tpu as pltpu

F32 = jnp.float32
BF16 = jnp.bfloat16
I32 = jnp.int32

D_MODEL = 1024
HEAD_DIM = 64
DA_HEADS = 4
WA_Q_HEADS = 8
WA_REP = 4
WINDOW = 128
IN_COLS = 2304
DA_V = 512
WA_Q = 512
N_BUCKETS = 32
MAX_DISTANCE = 128
N_EXPERTS = 16
EXPERT_FF = 1024
CAPACITY_FACTOR = 2
RMS_EPS = 1e-6
SUBLN_EPS = 1e-5
LAMBDA_INIT = 0.8 - 0.6 * math.exp(-0.3 * 0)

LANES = 128
BF16_SUBLANES = 16
VMEM_LIMIT = 56 * 1024 * 1024
NT_DIMS = (((1,), (1,)), ((), ()))


def _params(sem, vmem=VMEM_LIMIT):
    return pltpu.CompilerParams(dimension_semantics=sem, vmem_limit_bytes=vmem)


def _rms(x, g, eps):
    return x * lax.rsqrt(jnp.mean(x * x, axis=-1, keepdims=True) + eps) * g


def _inproj_kernel(x_ref, g_ref, w_ref, o_ref):
    h = _rms(x_ref[...], g_ref[...], RMS_EPS).astype(BF16)
    o_ref[...] = jnp.dot(h, w_ref[...], preferred_element_type=F32).astype(o_ref.dtype)


def _inproj(x2, g, w, tm):
    m = x2.shape[0]
    return pl.pallas_call(
        _inproj_kernel,
        out_shape=jax.ShapeDtypeStruct((m, IN_COLS), BF16),
        grid=(m // tm,),
        in_specs=[pl.BlockSpec((tm, D_MODEL), lambda i: (i, 0)),
                  pl.BlockSpec((1, D_MODEL), lambda i: (0, 0)),
                  pl.BlockSpec((D_MODEL, IN_COLS), lambda i: (0, 0))],
        out_specs=pl.BlockSpec((tm, IN_COLS), lambda i: (i, 0)),
        compiler_params=_params(("parallel",)),
        name="inproj",
    )(x2, g, w)


def _t5_bucket(rel):
    nb = N_BUCKETS // 2
    max_exact = nb // 2
    ret = jnp.where(rel > 0, nb, 0)
    n = jnp.abs(rel)
    nf = jnp.maximum(n, 1).astype(F32)
    large = max_exact + (jnp.log(nf / max_exact) / math.log(MAX_DISTANCE / max_exact)
                         * (nb - max_exact)).astype(I32)
    large = jnp.minimum(large, nb - 1)
    return ret + jnp.where(n < max_exact, n, large)


def _bias_lookup(bucket, tbl_ref, col):
    out = jnp.zeros(bucket.shape, F32)
    for b in range(N_BUCKETS):
        out = jnp.where(bucket == b, tbl_ref[b, col], out)
    return out


def _toeplitz(row, tq, tk):
    width = tq + tk
    rb = jnp.broadcast_to(row, (tq, width))
    rolled = pltpu.roll(rb, width - tq + 1, 1, stride=1, stride_axis=0)
    return rolled[:, :tk]


def _da_kernel(tbl_ref, lq1_ref, lk1_ref, lq2_ref, lk2_ref, q_ref, k_ref, v_ref, g_ref,
               o_ref, m_ref, l_ref, acc_ref, *, tq, tk, seq):
    h = pl.program_id(1)
    q0 = pl.program_id(2) * tq
    nk = seq // tk
    n_left = jnp.maximum(q0 - (MAX_DISTANCE - 1), 0) // tk
    first_right = jnp.minimum((q0 + tq + MAX_DISTANCE - 1 + tk - 1) // tk, nk)
    c_left = tbl_ref[N_BUCKETS // 2 - 1, h]
    c_right = tbl_ref[N_BUCKETS - 1, h]

    q = q_ref[0]
    lane = lax.broadcasted_iota(I32, q.shape, 1)
    outs = []
    for mp in range(2):
        keep = (lane < HEAD_DIM) if mp == 0 else (lane >= HEAD_DIM)
        qm = jnp.where(keep, q, jnp.zeros_like(q))
        m_ref[...] = jnp.full(m_ref.shape, -jnp.inf, F32)
        l_ref[...] = jnp.zeros(l_ref.shape, F32)
        acc_ref[...] = jnp.zeros(acc_ref.shape, F32)

        def step(ki, bias, const, qm=qm):
            start = pl.multiple_of(ki * tk, tk)
            k = k_ref[0, pl.ds(start, tk), :]
            v = v_ref[0, pl.ds(start, tk), :]
            s = lax.dot_general(qm, k, NT_DIMS, preferred_element_type=F32)
            if bias is not None:
                s = s + bias
            m_prev = m_ref[...]
            m_new = jnp.maximum(m_prev, jnp.max(s, axis=1, keepdims=True) + const)
            alpha = jnp.exp(m_prev - m_new)
            p = jnp.exp(s - (m_new - const))
            l_ref[...] = alpha * l_ref[...] + jnp.sum(p, axis=1, keepdims=True)
            acc_ref[...] = alpha * acc_ref[...] + jnp.dot(p.astype(BF16), v,
                                                          preferred_element_type=F32)
            m_ref[...] = m_new

        def far_left(ki, c):
            step(ki, None, c_left)
            return c

        def far_right(ki, c):
            step(ki, None, c_right)
            return c

        def near(ki, c):
            jj = lax.broadcasted_iota(I32, (1, tq + tk), 1)
            rel = jnp.clip(ki * tk - q0 + jj - (tq - 1), -MAX_DISTANCE, MAX_DISTANCE)
            row = _bias_lookup(_t5_bucket(rel), tbl_ref, h)
            step(ki, _toeplitz(row, tq, tk), 0.0)
            return c

        lax.fori_loop(0, n_left, far_left, 0)
        lax.fori_loop(n_left, first_right, near, 0)
        lax.fori_loop(first_right, nk, far_right, 0)
        outs.append(acc_ref[...] / l_ref[...])

    lam = (jnp.exp(jnp.sum(lq1_ref[...] * lk1_ref[...], axis=1, keepdims=True))
           - jnp.exp(jnp.sum(lq2_ref[...] * lk2_ref[...], axis=1, keepdims=True))
           + LAMBDA_INIT)
    o = outs[0] - lam * outs[1]
    o = _rms(o, g_ref[...], SUBLN_EPS) * (1.0 - LAMBDA_INIT)
    o_ref[0] = o.astype(o_ref.dtype)


def _diff_attention(proj3, tbl, lq1, lk1, lq2, lk2, subln_g, tq, tk):
    b, s, _ = proj3.shape
    vec = lambda n: pl.BlockSpec((1, n), lambda bi, hi, qi: (0, 0))
    kern = functools.partial(_da_kernel, tq=tq, tk=tk, seq=s)
    return pl.pallas_call(
        kern,
        out_shape=jax.ShapeDtypeStruct((b, s, DA_V), BF16),
        grid=(b, DA_HEADS, s // tq),
        in_specs=[pl.BlockSpec(memory_space=pltpu.SMEM),
                  vec(HEAD_DIM), vec(HEAD_DIM), vec(HEAD_DIM), vec(HEAD_DIM),
                  pl.BlockSpec((1, tq, LANES), lambda bi, hi, qi: (bi, qi, hi)),
                  pl.BlockSpec((1, s, LANES), lambda bi, hi, qi: (bi, 0, DA_HEADS + hi)),
                  pl.BlockSpec((1, s, LANES), lambda bi, hi, qi: (bi, 0, 2 * DA_HEADS + hi)),
                  vec(2 * HEAD_DIM)],
        out_specs=pl.BlockSpec((1, tq, LANES), lambda bi, hi, qi: (bi, qi, hi)),
        scratch_shapes=[pltpu.VMEM((tq, 1), F32), pltpu.VMEM((tq, 1), F32),
                        pltpu.VMEM((tq, 2 * HEAD_DIM), F32)],
        compiler_params=_params(("parallel", "parallel", "arbitrary")),
        name="diff_attention",
    )(tbl, lq1, lk1, lq2, lk2, proj3, proj3, proj3, subln_g)


def _wa_kernel(tbl_ref, sink_ref, q_ref, k_ref, v_ref, o_ref, *, tq, seq):
    q0 = pl.program_id(1) * tq
    win = tq + 2 * WINDOW
    ws = pl.multiple_of(jnp.clip(q0 - WINDOW, 0, seq - win), LANES)
    kw = k_ref[0, pl.ds(ws, win), :]
    vw = v_ref[0, pl.ds(ws, win), :]
    q = q_ref[0]
    jj = lax.broadcasted_iota(I32, (1, win + tq), 1)
    rel = ws - q0 + jj - (tq - 1)
    valid = jnp.abs(rel) <= WINDOW
    bucket = _t5_bucket(jnp.clip(rel, -MAX_DISTANCE, MAX_DISTANCE))
    outs = []
    for hq in range(WA_Q_HEADS):
        g = hq // WA_REP
        row = jnp.where(valid, _bias_lookup(bucket, tbl_ref, DA_HEADS + hq), -jnp.inf)
        s = lax.dot_general(q[:, hq * HEAD_DIM:(hq + 1) * HEAD_DIM],
                            kw[:, g * HEAD_DIM:(g + 1) * HEAD_DIM], NT_DIMS,
                            preferred_element_type=F32)
        s = s + _toeplitz(row, tq, win)
        sk = sink_ref[0, hq]
        m = jnp.maximum(jnp.max(s, axis=1, keepdims=True), sk)
        e = jnp.exp(s - m)
        denom = jnp.sum(e, axis=1, keepdims=True) + jnp.exp(sk - m)
        pv = jnp.dot(e.astype(BF16), vw[:, g * HEAD_DIM:(g + 1) * HEAD_DIM],
                     preferred_element_type=F32)
        outs.append(pv / denom)
    o_ref[0] = jnp.concatenate(outs, axis=1).astype(o_ref.dtype)


def _windowed_gqa(proj3, tbl, sink, tq):
    b, s, _ = proj3.shape
    kern = functools.partial(_wa_kernel, tq=tq, seq=s)
    kv_blk = (IN_COLS - 2 * LANES) // LANES
    return pl.pallas_call(
        kern,
        out_shape=jax.ShapeDtypeStruct((b, s, WA_Q), BF16),
        grid=(b, s // tq),
        in_specs=[pl.BlockSpec(memory_space=pltpu.SMEM),
                  pl.BlockSpec(memory_space=pltpu.SMEM),
                  pl.BlockSpec((1, tq, WA_Q), lambda bi, qi: (bi, qi, 3)),
                  pl.BlockSpec((1, s, LANES), lambda bi, qi: (bi, 0, kv_blk)),
                  pl.BlockSpec((1, s, LANES), lambda bi, qi: (bi, 0, kv_blk + 1))],
        out_specs=pl.BlockSpec((1, tq, WA_Q), lambda bi, qi: (bi, qi, 0)),
        compiler_params=_params(("parallel", "arbitrary")),
        name="windowed_gqa",
    )(tbl, sink, proj3, proj3, proj3)


def _oproj_kernel(a_ref, b_ref, x_ref, w1_ref, w2_ref, g_ref, wr_ref, x1_ref, h_ref, aff_ref):
    x1 = (x_ref[...] + jnp.dot(a_ref[...], w1_ref[...], preferred_element_type=F32)
          + jnp.dot(b_ref[...], w2_ref[...], preferred_element_type=F32))
    x1_ref[...] = x1
    h = _rms(x1, g_ref[...], RMS_EPS)
    h_ref[...] = h.astype(h_ref.dtype)
    logits = lax.dot_general(wr_ref[...], h, NT_DIMS, precision=lax.Precision.HIGHEST,
                             preferred_element_type=F32)
    e = jnp.exp(logits - jnp.max(logits, axis=0, keepdims=True))
    aff_ref[...] = e / jnp.sum(e, axis=0, keepdims=True)


def _oproj(a, b, x2, w1, w2, g, wr_t, tm):
    m = x2.shape[0]
    row = lambda n: pl.BlockSpec((tm, n), lambda i: (i, 0))
    full = lambda r, c: pl.BlockSpec((r, c), lambda i: (0, 0))
    return pl.pallas_call(
        _oproj_kernel,
        out_shape=(jax.ShapeDtypeStruct((m, D_MODEL), F32),
                   jax.ShapeDtypeStruct((m, D_MODEL), BF16),
                   jax.ShapeDtypeStruct((N_EXPERTS, m), F32)),
        grid=(m // tm,),
        in_specs=[row(DA_V), row(WA_Q), row(D_MODEL), full(DA_V, D_MODEL), full(WA_Q, D_MODEL),
                  full(1, D_MODEL), full(N_EXPERTS, D_MODEL)],
        out_specs=(row(D_MODEL), row(D_MODEL), pl.BlockSpec((N_EXPERTS, tm), lambda i: (0, i))),
        compiler_params=_params(("parallel",)),
        name="oproj_router",
    )(a, b, x2, w1, w2, g, wr_t)


def _cumsum_lanes(x):
    n = x.shape[1]
    lane = lax.broadcasted_iota(I32, x.shape, 1)
    shift = 1
    while shift < n:
        x = x + jnp.where(lane >= shift, pltpu.roll(x, shift, 1), 0)
        shift *= 2
    return x


def _select_kernel(aff_ref, slot_ref, excl_ref, *, cap):
    bits = pltpu.bitcast(aff_ref[...], I32)

    def count(mask):
        return jnp.sum(mask.astype(F32), axis=1, keepdims=True)

    def body(i, prefix):
        cand = prefix | jnp.left_shift(jnp.int32(1), 30 - i)
        return jnp.where(count(bits >= cand) >= cap, cand, prefix)

    thr = lax.fori_loop(0, 31, body, jnp.zeros((bits.shape[0], 1), I32))
    gt = bits > thr
    eq = bits == thr
    need = cap - count(gt)
    eq_i = eq.astype(I32)
    eq_excl = (_cumsum_lanes(eq_i) - eq_i).astype(F32)
    sel = gt | (eq & (eq_excl < need))
    sel_i = sel.astype(I32)
    excl = _cumsum_lanes(sel_i) - sel_i
    slot_ref[...] = jnp.where(sel, excl, -1)
    excl_ref[...] = excl


def _select(aff_t, cap):
    shp = jax.ShapeDtypeStruct(aff_t.shape, I32)
    return pl.pallas_call(
        functools.partial(_select_kernel, cap=cap),
        out_shape=(shp, shp),
        compiler_params=_params(None),
        name="expert_select",
    )(aff_t)


def _gather_kernel(cnt_ref, x_ref, slot_ref, o_ref, *, nb, rw):
    e = pl.program_id(0)
    b = pl.program_id(1)

    @pl.when(b == 0)
    def _():
        o_ref[...] = jnp.zeros(o_ref.shape, o_ref.dtype)

    lo = cnt_ref[e * (nb + 1) + b]
    hi = cnt_ref[e * (nb + 1) + b + 1]
    slot = slot_ref[0]
    x = x_ref[...]
    rows = lax.broadcasted_iota(I32, (rw, slot.shape[1]), 0)

    def window(w, c):
        base = pl.multiple_of(w * rw, rw)
        onehot = jnp.where(slot - base == rows, 1.0, 0.0).astype(BF16)
        o_ref[0, pl.ds(base, rw), :] += jnp.dot(onehot, x, preferred_element_type=F32).astype(o_ref.dtype)
        return c

    lax.fori_loop(lo // rw, (hi + rw - 1) // rw, window, 0)


def _gather(cnt, h2, slot3, cap, tb):
    m = h2.shape[0]
    nb = m // tb
    return pl.pallas_call(
        functools.partial(_gather_kernel, nb=nb, rw=LANES),
        out_shape=jax.ShapeDtypeStruct((N_EXPERTS, cap, D_MODEL), BF16),
        grid_spec=pltpu.PrefetchScalarGridSpec(
            num_scalar_prefetch=1,
            grid=(N_EXPERTS, nb),
            in_specs=[pl.BlockSpec((tb, D_MODEL), lambda e, b, c: (b, 0)),
                      pl.BlockSpec((1, 1, tb), lambda e, b, c: (e, 0, b))],
            out_specs=pl.BlockSpec((1, cap, D_MODEL), lambda e, b, c: (e, 0, 0))),
        compiler_params=_params(("parallel", "arbitrary")),
        name="moe_gather",
    )(cnt, h2, slot3)


def _ffn_kernel(x_ref, wg_ref, wu_ref, wd_ref, y_ref):
    x = x_ref[0]
    g = jnp.dot(x, wg_ref[0], preferred_element_type=F32)
    u = jnp.dot(x, wu_ref[0], preferred_element_type=F32)
    h = (g * jax.nn.sigmoid(g) * u).astype(BF16)
    y_ref[0] = jnp.dot(h, wd_ref[0], preferred_element_type=F32).astype(y_ref.dtype)


def _ffn(xg, wg, wu, wd, tm):
    _, cap, _ = xg.shape
    wspec = lambda r, c: pl.BlockSpec((1, r, c), lambda e, i: (e, 0, 0))
    return pl.pallas_call(
        _ffn_kernel,
        out_shape=jax.ShapeDtypeStruct(xg.shape, BF16),
        grid=(N_EXPERTS, cap // tm),
        in_specs=[pl.BlockSpec((1, tm, D_MODEL), lambda e, i: (e, i, 0)),
                  wspec(D_MODEL, EXPERT_FF), wspec(D_MODEL, EXPERT_FF), wspec(EXPERT_FF, D_MODEL)],
        out_specs=pl.BlockSpec((1, tm, D_MODEL), lambda e, i: (e, i, 0)),
        compiler_params=_params(("parallel", "arbitrary")),
        name="moe_ffn",
    )(xg, wg, wu, wd)


def _combine_kernel(cnt_ref, x1_ref, slot_ref, gate_ref, fn_ref, y_hbm, o_ref,
                    buf, sem, xbuf, xsem, acc_ref, *, nb, cap, rw):
    b = pl.program_id(0)
    tb = x1_ref.shape[0]

    def window_copy(e, start, dst, dsem):
        return pltpu.make_async_copy(y_hbm.at[e, pl.ds(start, rw), :], dst, dsem)

    starts, his = [], []
    for e in range(N_EXPERTS):
        lo = cnt_ref[e * (nb + 1) + b]
        his.append(cnt_ref[e * (nb + 1) + b + 1])
        st = pl.multiple_of(jnp.minimum((lo // BF16_SUBLANES) * BF16_SUBLANES, cap - rw), BF16_SUBLANES)
        starts.append(st)
        window_copy(e, st, buf.at[e], sem.at[e]).start()

    acc_ref[...] = x1_ref[...]
    lane = lax.broadcasted_iota(I32, (tb, rw), 1)
    for e in range(N_EXPERTS):
        st = starts[e]
        window_copy(e, st, buf.at[e], sem.at[e]).wait()
        slot = slot_ref[:, e:e + 1]
        gate = gate_ref[:, e:e + 1]
        onehot = jnp.where(slot - st == lane, 1.0, 0.0).astype(BF16)
        acc_ref[...] += gate * jnp.dot(onehot, buf[e], preferred_element_type=F32)

        n_extra = (jnp.maximum(his[e] - (st + rw), 0) + rw - 1) // rw

        def extra(w, c, e=e, st=st, slot=slot, gate=gate):
            first = st + w * rw
            cst = pl.multiple_of(jnp.minimum(first, cap - rw), BF16_SUBLANES)
            cp = window_copy(e, cst, xbuf, xsem)
            cp.start()
            cp.wait()
            hit = jnp.logical_and(slot - cst == lane, slot >= first)
            onehot_x = jnp.where(hit, 1.0, 0.0).astype(BF16)
            acc_ref[...] += gate * jnp.dot(onehot_x, xbuf[...], preferred_element_type=F32)
            return c

        lax.fori_loop(1, 1 + n_extra, extra, 0)

    o_ref[...] = _rms(acc_ref[...], fn_ref[...], RMS_EPS)


def _combine(cnt, x1, slot_t, gate_t, fn, y, tb):
    m = x1.shape[0]
    nb = m // tb
    cap = y.shape[1]
    rw = LANES
    return pl.pallas_call(
        functools.partial(_combine_kernel, nb=nb, cap=cap, rw=rw),
        out_shape=jax.ShapeDtypeStruct((m, D_MODEL), F32),
        grid_spec=pltpu.PrefetchScalarGridSpec(
            num_scalar_prefetch=1,
            grid=(nb,),
            in_specs=[pl.BlockSpec((tb, D_MODEL), lambda b, c: (b, 0)),
                      pl.BlockSpec((tb, N_EXPERTS), lambda b, c: (b, 0)),
                      pl.BlockSpec((tb, N_EXPERTS), lambda b, c: (b, 0)),
                      pl.BlockSpec((1, D_MODEL), lambda b, c: (0, 0)),
                      pl.BlockSpec(memory_space=pl.ANY)],
            out_specs=pl.BlockSpec((tb, D_MODEL), lambda b, c: (b, 0)),
            scratch_shapes=[pltpu.VMEM((N_EXPERTS, rw, D_MODEL), BF16),
                            pltpu.SemaphoreType.DMA((N_EXPERTS,)),
                            pltpu.VMEM((rw, D_MODEL), BF16),
                            pltpu.SemaphoreType.DMA(()),
                            pltpu.VMEM((tb, D_MODEL), F32)]),
        compiler_params=_params(("arbitrary",)),
        name="moe_combine",
    )(cnt, x1, slot_t, gate_t, fn, y)


def _pick(n, prefs):
    for p in prefs:
        if n % p == 0:
            return p
    raise ValueError(f"no tile in {prefs} divides {n}")


def _trunk(x, tbl, sink, lam, subln_g, norm1, w_in, w_o1, w_o2, norm2, wr_t, wg, wu, wd, fn):
    bsz, s, d = x.shape
    m = bsz * s
    cap = CAPACITY_FACTOR * m // N_EXPERTS
    x2 = x.reshape(m, d)
    proj = _inproj(x2, norm1, w_in, _pick(m, (512, 256)))
    proj3 = proj.reshape(bsz, s, IN_COLS)
    t_att = _pick(s, (512, 256))
    out_a = _diff_attention(proj3, tbl, *lam, subln_g, t_att, t_att)
    out_b = _windowed_gqa(proj3, tbl, sink, _pick(s, (256,)))
    x1, h2, aff_t = _oproj(out_a.reshape(m, DA_V), out_b.reshape(m, WA_Q), x2, w_o1, w_o2,
                           norm2, wr_t, _pick(m, (512, 256)))
    slot, excl = _select(aff_t, cap)
    tb = 256
    nb = m // tb
    cnt = jnp.concatenate([excl[:, ::tb], jnp.full((N_EXPERTS, 1), cap, I32)], axis=1).reshape(-1)
    xg = _gather(cnt, h2, slot.reshape(N_EXPERTS, 1, m), cap, tb)
    y = _ffn(xg, wg, wu, wd, _pick(cap, (512, 256, 128)))
    out = _combine(cnt, x1, slot.T, aff_t.T, fn, y, tb)
    return out.reshape(bsz, s, d)


def kernel(x_prompt, x_sample, rel_bias, norm1, w_in, lam_q1, lam_k1, lam_q2, lam_k2, subln_g, sink,
           w_o, norm2, w_router, w_gate, w_up, w_down, final_norm):
    scale = HEAD_DIM ** -0.5
    col = jnp.arange(IN_COLS)
    is_q = (col < DA_HEADS * 2 * HEAD_DIM) | ((col >= 3 * DA_V) & (col < 3 * DA_V + WA_Q))
    w_in_s = (w_in[0] * jnp.where(is_q, scale, 1.0).astype(F32)).astype(BF16)
    args = dict(
        tbl=rel_bias, sink=sink,
        lam=(lam_q1, lam_k1, lam_q2, lam_k2), subln_g=subln_g,
        norm1=norm1, w_in=w_in_s,
        w_o1=w_o[0, :DA_V].astype(BF16), w_o2=w_o[0, DA_V:].astype(BF16),
        norm2=norm2, wr_t=w_router[0].T,
        wg=w_gate[0].astype(BF16), wu=w_up[0].astype(BF16), wd=w_down[0].astype(BF16),
        fn=final_norm.reshape(1, D_MODEL))
    return (_trunk(x_prompt, **args), _trunk(x_sample, **args))
```

```python
import functools
import math

import jax
import jax.numpy as jnp
from jax import lax
from jax.experimental import pallas as pl
from jax.experimental.pallas import tpu as pltpu

F32 = jnp.float32
BF16 = jnp.bfloat16
I32 = jnp.int32

D_MODEL = 1024
HEAD_DIM = 64
DA_HEADS = 4
WA_Q_HEADS = 8
WA_REP = 4
WINDOW = 128
IN_COLS = 2304
DA_V = 512
WA_Q = 512
N_BUCKETS = 32
MAX_DISTANCE = 128
N_EXPERTS = 16
EXPERT_FF = 1024
CAPACITY_FACTOR = 2
RMS_EPS = 1e-6
SUBLN_EPS = 1e-5
LAMBDA_INIT = 0.8 - 0.6 * math.exp(-0.3 * 0)
LOG2E = math.log2(math.e)

LANES = 128
BF16_SUBLANES = 16
VMEM_LIMIT = 56 * 1024 * 1024
NT_DIMS = (((1,), (1,)), ((), ()))


def _params(sem, vmem=VMEM_LIMIT):
    return pltpu.CompilerParams(dimension_semantics=sem, vmem_limit_bytes=vmem)


def _rms(x, g, eps):
    return x * lax.rsqrt(jnp.mean(x * x, axis=-1, keepdims=True) + eps) * g


def _inproj_kernel(x_ref, g_ref, w_ref, o_ref):
    h = _rms(x_ref[...], g_ref[...], RMS_EPS).astype(BF16)
    o_ref[...] = jnp.dot(h, w_ref[...], preferred_element_type=F32).astype(o_ref.dtype)


def _inproj(x2, g, w, tm):
    m = x2.shape[0]
    return pl.pallas_call(
        _inproj_kernel,
        out_shape=jax.ShapeDtypeStruct((m, IN_COLS), BF16),
        grid=(m // tm,),
        in_specs=[pl.BlockSpec((tm, D_MODEL), lambda i: (i, 0)),
                  pl.BlockSpec((1, D_MODEL), lambda i: (0, 0)),
                  pl.BlockSpec((D_MODEL, IN_COLS), lambda i: (0, 0))],
        out_specs=pl.BlockSpec((tm, IN_COLS), lambda i: (i, 0)),
        compiler_params=_params(("parallel",)),
        name="inproj",
    )(x2, g, w)


def _t5_bucket(rel):
    nb = N_BUCKETS // 2
    max_exact = nb // 2
    ret = jnp.where(rel > 0, nb, 0)
    n = jnp.abs(rel)
    nf = jnp.maximum(n, 1).astype(F32)
    large = max_exact + (jnp.log(nf / max_exact) / math.log(MAX_DISTANCE / max_exact)
                         * (nb - max_exact)).astype(I32)
    large = jnp.minimum(large, nb - 1)
    return ret + jnp.where(n < max_exact, n, large)


def _bias_lookup(bucket, tbl_ref, col):
    out = jnp.zeros(bucket.shape, F32)
    for b in range(N_BUCKETS):
        out = jnp.where(bucket == b, tbl_ref[b, col], out)
    return out


def _toeplitz(row, tq, tk):
    width = tq + tk
    rb = jnp.broadcast_to(row, (tq, width))
    rolled = pltpu.roll(rb, width - tq + 1, 1, stride=1, stride_axis=0)
    return rolled[:, :tk]


def _da_kernel(tbl_ref, lq1_ref, lk1_ref, lq2_ref, lk2_ref, q_ref, k_ref, v_ref, g_ref,
               o_ref, qm_ref, m_ref, l_ref, acc_ref, bias_ref, *, t, seq):
    h = pl.program_id(1)
    qi = pl.program_id(2)
    nk = seq // t

    @pl.when(qi == 0)
    def _():
        for d in range(3):
            jj = lax.broadcasted_iota(I32, (1, 2 * t), 1)
            rel = jnp.clip((d - 1) * t + jj - (t - 1), -MAX_DISTANCE, MAX_DISTANCE)
            bias_ref[d] = _toeplitz(_bias_lookup(_t5_bucket(rel), tbl_ref, h) * LOG2E, t, t)

    c_left = tbl_ref[N_BUCKETS // 2 - 1, h] * LOG2E
    c_right = tbl_ref[N_BUCKETS - 1, h] * LOG2E

    q = q_ref[0]
    lane = lax.broadcasted_iota(I32, q.shape, 1)
    qm_ref[0] = jnp.where(lane < HEAD_DIM, q, jnp.zeros_like(q))
    qm_ref[1] = jnp.where(lane >= HEAD_DIM, q, jnp.zeros_like(q))
    m_ref[...] = jnp.full(m_ref.shape, -jnp.inf, F32)
    l_ref[...] = jnp.zeros(l_ref.shape, F32)
    acc_ref[...] = jnp.zeros(acc_ref.shape, F32)

    def step(tile, width, bias_idx):
        start = pl.multiple_of(tile * t, t)
        k = k_ref[0, pl.ds(start, width), :]
        v = v_ref[0, pl.ds(start, width), :]
        reps = width // LANES
        logits = [lax.dot_general(qm_ref[mp], k, NT_DIMS, preferred_element_type=F32) for mp in range(2)]
        for mp in range(2):
            s = logits[mp]
            if bias_idx is not None:
                s = s + bias_ref[bias_idx]
            m_prev = m_ref[mp]
            m_new = jnp.maximum(m_prev, jnp.max(s, axis=1, keepdims=True))
            alpha = jnp.exp2(m_prev - m_new)
            p = jnp.exp2(s - jnp.tile(m_new, (1, reps)))
            psum = p[:, :LANES]
            for r in range(1, reps):
                psum = psum + p[:, r * LANES:(r + 1) * LANES]
            l_ref[mp] = alpha * l_ref[mp] + psum
            acc_ref[mp] = alpha * acc_ref[mp] + jnp.dot(p.astype(BF16), v, preferred_element_type=F32)
            m_ref[mp] = m_new

    def far(first, count):
        def pair(j, c):
            step(first + 2 * j, 2 * t, None)
            return c

        def single(j, c):
            step(first + count - 1, t, None)
            return c

        lax.fori_loop(0, count // 2, pair, 0)
        lax.fori_loop(0, count % 2, single, 0)

    def near(ki, c):
        step(ki, t, ki - qi + 1)
        return c

    first_near = jnp.maximum(qi - 1, 0)
    first_right = jnp.minimum(qi + 2, nk)
    far(0, first_near)
    m_ref[...] += c_left
    lax.fori_loop(first_near, first_right, near, 0)
    m_ref[...] -= c_right
    far(first_right, nk - first_right)

    outs =[acc_ref[mp] / jnp.sum(l_ref[mp], axis=1, keepdims=True) for mp in range(2)]
    lam = (jnp.exp(jnp.sum(lq1_ref[...] * lk1_ref[...], axis=1, keepdims=True))
           - jnp.exp(jnp.sum(lq2_ref[...] * lk2_ref[...], axis=1, keepdims=True))
           + LAMBDA_INIT)
    o = outs[0] - lam * outs[1]
    o = _rms(o, g_ref[...], SUBLN_EPS) * (1.0 - LAMBDA_INIT)
    o_ref[0] = o.astype(o_ref.dtype)


def _diff_attention(proj3, tbl, lq1, lk1, lq2, lk2, subln_g, t):
    b, s, _ = proj3.shape
    assert t >= MAX_DISTANCE and s % t == 0
    vec = lambda n: pl.BlockSpec((1, n), lambda bi, hi, qi: (0, 0))
    kern = functools.partial(_da_kernel, t=t, seq=s)
    return pl.pallas_call(
        kern,
        out_shape=jax.ShapeDtypeStruct((b, s, DA_V), BF16),
        grid=(b, DA_HEADS, s // t),
        in_specs=[pl.BlockSpec(memory_space=pltpu.SMEM),
                  vec(HEAD_DIM), vec(HEAD_DIM), vec(HEAD_DIM), vec(HEAD_DIM),
                  pl.BlockSpec((1, t, LANES), lambda bi, hi, qi: (bi, qi, hi)),
                  pl.BlockSpec((1, s, LANES), lambda bi, hi, qi: (bi, 0, DA_HEADS + hi)),
                  pl.BlockSpec((1, s, LANES), lambda bi, hi, qi: (bi, 0, 2 * DA_HEADS + hi)),
                  vec(2 * HEAD_DIM)],
        out_specs=pl.BlockSpec((1, t, LANES), lambda bi, hi, qi: (bi, qi, hi)),
        scratch_shapes=[pltpu.VMEM((2, t, LANES), BF16), pltpu.VMEM((2, t, LANES), F32),
                        pltpu.VMEM((2, t, LANES), F32), pltpu.VMEM((2, t, 2 * HEAD_DIM), F32),
                        pltpu.VMEM((3, t, t), F32)],
        compiler_params=_params(("parallel", "parallel", "arbitrary")),
        name="diff_attention",
    )(tbl, lq1, lk1, lq2, lk2, proj3, proj3, proj3, subln_g)


def _wa_kernel(tbl_ref, sink_ref, q_ref, k_ref, v_ref, o_ref, *, tq, seq):
    q0 = pl.program_id(1) * tq
    win = tq + 2 * WINDOW
    ws = pl.multiple_of(jnp.clip(q0 - WINDOW, 0, seq - win), LANES)
    kw = k_ref[0, pl.ds(ws, win), :]
    vw = v_ref[0, pl.ds(ws, win), :]
    q = q_ref[0]
    jj = lax.broadcasted_iota(I32, (1, win + tq), 1)
    rel = ws - q0 + jj - (tq - 1)
    valid = jnp.abs(rel) <= WINDOW
    bucket = _t5_bucket(jnp.clip(rel, -MAX_DISTANCE, MAX_DISTANCE))
    outs = []
    for hq in range(WA_Q_HEADS):
        g = hq // WA_REP
        row = jnp.where(valid, _bias_lookup(bucket, tbl_ref, DA_HEADS + hq), -jnp.inf)
        s = lax.dot_general(q[:, hq * HEAD_DIM:(hq + 1) * HEAD_DIM],
                            kw[:, g * HEAD_DIM:(g + 1) * HEAD_DIM], NT_DIMS,
                            preferred_element_type=F32)
        s = s + _toeplitz(row, tq, win)
        sk = sink_ref[0, hq]
        m = jnp.maximum(jnp.max(s, axis=1, keepdims=True), sk)
        e = jnp.exp(s - m)
        denom = jnp.sum(e, axis=1, keepdims=True) + jnp.exp(sk - m)
        pv = jnp.dot(e.astype(BF16), vw[:, g * HEAD_DIM:(g + 1) * HEAD_DIM],
                     preferred_element_type=F32)
        outs.append(pv / denom)
    o_ref[0] = jnp.concatenate(outs, axis=1).astype(o_ref.dtype)


def _windowed_gqa(proj3, tbl, sink, tq):
    b, s, _ = proj3.shape
    kern = functools.partial(_wa_kernel, tq=tq, seq=s)
    kv_blk = (IN_COLS - 2 * LANES) // LANES
    return pl.pallas_call(
        kern,
        out_shape=jax.ShapeDtypeStruct((b, s, WA_Q), BF16),
        grid=(b, s // tq),
        in_specs=[pl.BlockSpec(memory_space=pltpu.SMEM),
                  pl.BlockSpec(memory_space=pltpu.SMEM),
                  pl.BlockSpec((1, tq, WA_Q), lambda bi, qi: (bi, qi, 3)),
                  pl.BlockSpec((1, s, LANES), lambda bi, qi: (bi, 0, kv_blk)),
                  pl.BlockSpec((1, s, LANES), lambda bi, qi: (bi, 0, kv_blk + 1))],
        out_specs=pl.BlockSpec((1, tq, WA_Q), lambda bi, qi: (bi, qi, 0)),
        compiler_params=_params(("parallel", "arbitrary")),
        name="windowed_gqa",
    )(tbl, sink, proj3, proj3, proj3)


def _oproj_kernel(a_ref, b_ref, x_ref, w1_ref, w2_ref, g_ref, wr_ref, x1_ref, h_ref, aff_ref):
    x1 = (x_ref[...] + jnp.dot(a_ref[...], w1_ref[...], preferred_element_type=F32)
          + jnp.dot(b_ref[...], w2_ref[...], preferred_element_type=F32))
    x1_ref[...] = x1
    h = _rms(x1, g_ref[...], RMS_EPS)
    h_ref[...] = h.astype(h_ref.dtype)
    logits = lax.dot_general(wr_ref[...], h, NT_DIMS, precision=lax.Precision.HIGHEST,
                             preferred_element_type=F32)
    e = jnp.exp(logits - jnp.max(logits, axis=0, keepdims=True))
    aff_ref[...] = e / jnp.sum(e, axis=0, keepdims=True)


def _oproj(a, b, x2, w1, w2, g, wr_t, tm):
    m = x2.shape[0]
    row = lambda n: pl.BlockSpec((tm, n), lambda i: (i, 0))
    full = lambda r, c: pl.BlockSpec((r, c), lambda i: (0, 0))
    return pl.pallas_call(
        _oproj_kernel,
        out_shape=(jax.ShapeDtypeStruct((m, D_MODEL), F32),
                   jax.ShapeDtypeStruct((m, D_MODEL), BF16),
                   jax.ShapeDtypeStruct((N_EXPERTS, m), F32)),
        grid=(m // tm,),
        in_specs=[row(DA_V), row(WA_Q), row(D_MODEL), full(DA_V, D_MODEL), full(WA_Q, D_MODEL),
                  full(1, D_MODEL), full(N_EXPERTS, D_MODEL)],
        out_specs=(row(D_MODEL), row(D_MODEL), pl.BlockSpec((N_EXPERTS, tm), lambda i: (0, i))),
        compiler_params=_params(("parallel",)),
        name="oproj_router",
    )(a, b, x2, w1, w2, g, wr_t)


def _cumsum_lanes(x):
    n = x.shape[1]
    lane = lax.broadcasted_iota(I32, x.shape, 1)
    shift = 1
    while shift < n:
        x = x + jnp.where(lane >= shift, pltpu.roll(x, shift, 1), 0)
        shift *= 2
    return x


def _select_kernel(aff_ref, slot_ref, excl_ref, *, cap):
    bits = pltpu.bitcast(aff_ref[...], I32)

    def count(mask):
        return jnp.sum(mask.astype(F32), axis=1, keepdims=True)

    def body(i, prefix):
        cand = prefix | jnp.left_shift(jnp.int32(1), 30 - i)
        return jnp.where(count(bits >= cand) >= cap, cand, prefix)

    thr = lax.fori_loop(0, 31, body, jnp.zeros((bits.shape[0], 1), I32))
    gt = bits > thr
    eq = bits == thr
    need = cap - count(gt)
    eq_i = eq.astype(I32)
    eq_excl = (_cumsum_lanes(eq_i) - eq_i).astype(F32)
    sel = gt | (eq & (eq_excl < need))
    sel_i = sel.astype(I32)
    excl = _cumsum_lanes(sel_i) - sel_i
    slot_ref[...] = jnp.where(sel, excl, -1)
    excl_ref[...] = excl


def _select(aff_t, cap):
    shp = jax.ShapeDtypeStruct(aff_t.shape, I32)
    return pl.pallas_call(
        functools.partial(_select_kernel, cap=cap),
        out_shape=(shp, shp),
        compiler_params=_params(None),
        name="expert_select",
    )(aff_t)


def _gather_kernel(cnt_ref, x_ref, slot_ref, o_ref, *, nb, rw):
    e = pl.program_id(0)
    b = pl.program_id(1)

    @pl.when(b == 0)
    def _():
        o_ref[...] = jnp.zeros(o_ref.shape, o_ref.dtype)

    lo = cnt_ref[e * (nb + 1) + b]
    hi = cnt_ref[e * (nb + 1) + b + 1]
    slot = slot_ref[0]
    x = x_ref[...]
    rows = lax.broadcasted_iota(I32, (rw, slot.shape[1]), 0)

    def window(w, c):
        base = pl.multiple_of(w * rw, rw)
        onehot = jnp.where(slot - base == rows, 1.0, 0.0).astype(BF16)
        o_ref[0, pl.ds(base, rw), :] += jnp.dot(onehot, x, preferred_element_type=F32).astype(o_ref.dtype)
        return c

    lax.fori_loop(lo // rw, (hi + rw - 1) // rw, window, 0)


def _gather(cnt, h2, slot3, cap, tb):
    m = h2.shape[0]
    nb = m // tb
    return pl.pallas_call(
        functools.partial(_gather_kernel, nb=nb, rw=LANES),
        out_shape=jax.ShapeDtypeStruct((N_EXPERTS, cap, D_MODEL), BF16),
        grid_spec=pltpu.PrefetchScalarGridSpec(
            num_scalar_prefetch=1,
            grid=(N_EXPERTS, nb),
            in_specs=[pl.BlockSpec((tb, D_MODEL), lambda e, b, c: (b, 0)),
                      pl.BlockSpec((1, 1, tb), lambda e, b, c: (e, 0, b))],
            out_specs=pl.BlockSpec((1, cap, D_MODEL), lambda e, b, c: (e, 0, 0))),
        compiler_params=_params(("parallel", "arbitrary")),
        name="moe_gather",
    )(cnt, h2, slot3)


def _ffn_kernel(x_ref, wg_ref, wu_ref, wd_ref, y_ref):
    x = x_ref[0]
    g = jnp.dot(x, wg_ref[0], preferred_element_type=F32)
    u = jnp.dot(x, wu_ref[0], preferred_element_type=F32)
    h = (g * jax.nn.sigmoid(g) * u).astype(BF16)
    y_ref[0] = jnp.dot(h, wd_ref[0], preferred_element_type=F32).astype(y_ref.dtype)


def _ffn(xg, wg, wu, wd, tm):
    _, cap, _ = xg.shape
    wspec = lambda r, c: pl.BlockSpec((1, r, c), lambda e, i: (e, 0, 0))
    return pl.pallas_call(
        _ffn_kernel,
        out_shape=jax.ShapeDtypeStruct(xg.shape, BF16),
        grid=(N_EXPERTS, cap // tm),
        in_specs=[pl.BlockSpec((1, tm, D_MODEL), lambda e, i: (e, i, 0)),
                  wspec(D_MODEL, EXPERT_FF), wspec(D_MODEL, EXPERT_FF), wspec(EXPERT_FF, D_MODEL)],
        out_specs=pl.BlockSpec((1, tm, D_MODEL), lambda e, i: (e, i, 0)),
        compiler_params=_params(("parallel", "arbitrary")),
        name="moe_ffn",
    )(xg, wg, wu, wd)


def _combine_kernel(cnt_ref, x1_ref, slot_ref, gate_ref, fn_ref, y_hbm, o_ref,
                    buf, sem, xbuf, xsem, acc_ref, *, nb, cap, rw):
    b = pl.program_id(0)
    tb = x1_ref.shape[0]

    def window_copy(e, start, dst, dsem):
        return pltpu.make_async_copy(y_hbm.at[e, pl.ds(start, rw), :], dst, dsem)

    starts, his = [], []
    for e in range(N_EXPERTS):
        lo = cnt_ref[e * (nb + 1) + b]
        his.append(cnt_ref[e * (nb + 1) + b + 1])
        st = pl.multiple_of(jnp.minimum((lo // BF16_SUBLANES) * BF16_SUBLANES, cap - rw), BF16_SUBLANES)
        starts.append(st)
        window_copy(e, st, buf.at[e], sem.at[e]).start()

    acc_ref[...] = x1_ref[...]
    lane = lax.broadcasted_iota(I32, (tb, rw), 1)
    for e in range(N_EXPERTS):
        st = starts[e]
        window_copy(e, st, buf.at[e], sem.at[e]).wait()
        slot = slot_ref[:, e:e + 1]
        gate = gate_ref[:, e:e + 1]
        onehot = jnp.where(slot - st == lane, 1.0, 0.0).astype(BF16)
        acc_ref[...] += gate * jnp.dot(onehot, buf[e], preferred_element_type=F32)

        n_extra = (jnp.maximum(his[e] - (st + rw), 0) + rw - 1) // rw

        def extra(w, c, e=e, st=st, slot=slot, gate=gate):
            first = st + w * rw
            cst = pl.multiple_of(jnp.minimum(first, cap - rw), BF16_SUBLANES)
            cp = window_copy(e, cst, xbuf, xsem)
            cp.start()
            cp.wait()
            hit = jnp.logical_and(slot - cst == lane, slot >= first)
            onehot_x = jnp.where(hit, 1.0, 0.0).astype(BF16)
            acc_ref[...] += gate * jnp.dot(onehot_x, xbuf[...], preferred_element_type=F32)
            return c

        lax.fori_loop(1, 1 + n_extra, extra, 0)

    o_ref[...] = _rms(acc_ref[...], fn_ref[...], RMS_EPS)


def _combine(cnt, x1, slot_t, gate_t, fn, y, tb):
    m = x1.shape[0]
    nb = m // tb
    cap = y.shape[1]
    rw = LANES
    return pl.pallas_call(
        functools.partial(_combine_kernel, nb=nb, cap=cap, rw=rw),
        out_shape=jax.ShapeDtypeStruct((m, D_MODEL), F32),
        grid_spec=pltpu.PrefetchScalarGridSpec(
            num_scalar_prefetch=1,
            grid=(nb,),
            in_specs=[pl.BlockSpec((tb, D_MODEL), lambda b, c: (b, 0)),
                      pl.BlockSpec((tb, N_EXPERTS), lambda b, c: (b, 0)),
                      pl.BlockSpec((tb, N_EXPERTS), lambda b, c: (b, 0)),
                      pl.BlockSpec((1, D_MODEL), lambda b, c: (0, 0)),
                      pl.BlockSpec(memory_space=pl.ANY)],
            out_specs=pl.BlockSpec((tb, D_MODEL), lambda b, c: (b, 0)),
            scratch_shapes=[pltpu.VMEM((N_EXPERTS, rw, D_MODEL), BF16),
                            pltpu.SemaphoreType.DMA((N_EXPERTS,)),
                            pltpu.VMEM((rw, D_MODEL), BF16),
                            pltpu.SemaphoreType.DMA(()),
                            pltpu.VMEM((tb, D_MODEL), F32)]),
        compiler_params=_params(("arbitrary",)),
        name="moe_combine",
    )(cnt, x1, slot_t, gate_t, fn, y)


def _pick(n, prefs):
    for p in prefs:
        if n % p == 0:
            return p
    raise ValueError(f"no tile in {prefs} divides {n}")


def _trunk(x, tbl, sink, lam, subln_g, norm1, w_in, w_o1, w_o2, norm2, wr_t, wg, wu, wd, fn):
    bsz, s, d = x.shape
    m = bsz * s
    cap = CAPACITY_FACTOR * m // N_EXPERTS
    x2 = x.reshape(m, d)
    proj = _inproj(x2, norm1, w_in, _pick(m, (512, 256)))
    proj3 = proj.reshape(bsz, s, IN_COLS)
    out_a = _diff_attention(proj3, tbl, *lam, subln_g, _pick(s, (512, 256)))
    out_b = _windowed_gqa(proj3, tbl, sink, _pick(s, (256,)))
    x1, h2, aff_t = _oproj(out_a.reshape(m, DA_V), out_b.reshape(m, WA_Q), x2, w_o1, w_o2,
                           norm2, wr_t, _pick(m, (512, 256)))
    slot, excl = _select(aff_t, cap)
    tb = 256
    nb = m // tb
    cnt = jnp.concatenate([excl[:, ::tb], jnp.full((N_EXPERTS, 1), cap, I32)], axis=1).reshape(-1)
    xg = _gather(cnt, h2, slot.reshape(N_EXPERTS, 1, m), cap, tb)
    y = _ffn(xg, wg, wu, wd, _pick(cap, (512, 256, 128)))
    out = _combine(cnt, x1, slot.T, aff_t.T, fn, y, tb)
    return out.reshape(bsz, s, d)


def kernel(x_prompt, x_sample, rel_bias, norm1, w_in, lam_q1, lam_k1, lam_q2, lam_k2, subln_g, sink,
           w_o, norm2, w_router, w_gate, w_up, w_down, final_norm):
    scale = HEAD_DIM ** -0.5
    col = jnp.arange(IN_COLS)
    is_da_q = col < DA_HEADS * 2 * HEAD_DIM
    is_wa_q = (col >= 3 * DA_V) & (col < 3 * DA_V + WA_Q)
    col_scale = jnp.where(is_da_q, scale * LOG2E, jnp.where(is_wa_q, scale, 1.0)).astype(F32)
    w_in_s = (w_in[0] * col_scale).astype(BF16)
    args = dict(
        tbl=rel_bias, sink=sink,
        lam=(lam_q1, lam_k1, lam_q2, lam_k2), subln_g=subln_g,
        norm1=norm1, w_in=w_in_s,
        w_o1=w_o[0, :DA_V].astype(BF16), w_o2=w_o[0, DA_V:].astype(BF16),
        norm2=norm2, wr_t=w_router[0].T,
        wg=w_gate[0].astype(BF16), wu=w_up[0].astype(BF16), wd=w_down[0].astype(BF16),
        fn=final_norm.reshape(1, D_MODEL))
    return (_trunk(x_prompt, **args), _trunk(x_sample, **args))
```

```python
import functools
import math

import jax
import jax.numpy as jnp
from jax import lax
from jax.experimental import pallas as pl
from jax.experimental.pallas import tpu as pltpu

F32 = jnp.float32
BF16 = jnp.bfloat16
I32 = jnp.int32

D_MODEL = 1024
HEAD_DIM = 64
DA_HEADS = 4
WA_Q_HEADS = 8
WA_REP = 4
WINDOW = 128
IN_COLS = 2304
DA_V = 512
WA_Q = 512
N_BUCKETS = 32
MAX_DISTANCE = 128
N_EXPERTS = 16
EXPERT_FF = 1024
CAPACITY_FACTOR = 2
RMS_EPS = 1e-6
SUBLN_EPS = 1e-5
LAMBDA_INIT = 0.8 - 0.6 * math.exp(-0.3 * 0)
LOG2E = math.log2(math.e)

LANES = 128
BF16_SUBLANES = 16
VMEM_LIMIT = 56 * 1024 * 1024
NT_DIMS = (((1,), (1,)), ((), ()))


def _params(sem, vmem=VMEM_LIMIT):
    return pltpu.CompilerParams(dimension_semantics=sem, vmem_limit_bytes=vmem)


def _rms(x, g, eps):
    return x * lax.rsqrt(jnp.mean(x * x, axis=-1, keepdims=True) + eps) * g


def _inproj_kernel(x_ref, g_ref, w_ref, o_ref):
    h = _rms(x_ref[...], g_ref[...], RMS_EPS).astype(BF16)
    o_ref[...] = jnp.dot(h, w_ref[...], preferred_element_type=F32).astype(o_ref.dtype)


def _inproj(x2, g, w, tm):
    m = x2.shape[0]
    return pl.pallas_call(
        _inproj_kernel,
        out_shape=jax.ShapeDtypeStruct((m, IN_COLS), BF16),
        grid=(m // tm,),
        in_specs=[pl.BlockSpec((tm, D_MODEL), lambda i: (i, 0)),
                  pl.BlockSpec((1, D_MODEL), lambda i: (0, 0)),
                  pl.BlockSpec((D_MODEL, IN_COLS), lambda i: (0, 0))],
        out_specs=pl.BlockSpec((tm, IN_COLS), lambda i: (i, 0)),
        compiler_params=_params(("parallel",)),
        name="inproj",
    )(x2, g, w)


def _t5_bucket(rel):
    nb = N_BUCKETS // 2
    max_exact = nb // 2
    ret = jnp.where(rel > 0, nb, 0)
    n = jnp.abs(rel)
    nf = jnp.maximum(n, 1).astype(F32)
    large = max_exact + (jnp.log(nf / max_exact) / math.log(MAX_DISTANCE / max_exact)
                         * (nb - max_exact)).astype(I32)
    large = jnp.minimum(large, nb - 1)
    return ret + jnp.where(n < max_exact, n, large)


def _bias_lookup(bucket, tbl_ref, col):
    out = jnp.zeros(bucket.shape, F32)
    for b in range(N_BUCKETS):
        out = jnp.where(bucket == b, tbl_ref[b, col], out)
    return out


def _toeplitz(row, tq, tk):
    width = tq + tk
    rb = jnp.broadcast_to(row, (tq, width))
    rolled = pltpu.roll(rb, width - tq + 1, 1, stride=1, stride_axis=0)
    return rolled[:, :tk]


def _da_kernel(tbl_ref, lq1_ref, lk1_ref, lq2_ref, lk2_ref, q_ref, k_ref, v_ref, g_ref,
               o_ref, qm_ref, m_ref, l_ref, acc_ref, bias_ref, *, t, seq):
    h = pl.program_id(1)
    qi = pl.program_id(2)
    nk = seq // t

    @pl.when(qi == 0)
    def _():
        for d in range(3):
            jj = lax.broadcasted_iota(I32, (1, 2 * t), 1)
            rel = jnp.clip((d - 1) * t + jj - (t - 1), -MAX_DISTANCE, MAX_DISTANCE)
            bias_ref[d] = _toeplitz(_bias_lookup(_t5_bucket(rel), tbl_ref, h) * LOG2E, t, t)

    c_left = tbl_ref[N_BUCKETS // 2 - 1, h] * LOG2E
    c_right = tbl_ref[N_BUCKETS - 1, h] * LOG2E

    q = q_ref[0]
    lane = lax.broadcasted_iota(I32, q.shape, 1)
    qm_ref[0] = jnp.where(lane < HEAD_DIM, q, jnp.zeros_like(q))
    qm_ref[1] = jnp.where(lane >= HEAD_DIM, q, jnp.zeros_like(q))
    m_ref[...] = jnp.full(m_ref.shape, -jnp.inf, F32)
    l_ref[...] = jnp.zeros(l_ref.shape, F32)
    acc_ref[...] = jnp.zeros(acc_ref.shape, F32)

    def step(tile, width, bias_idx):
        start = pl.multiple_of(tile * t, t)
        k = k_ref[0, pl.ds(start, width), :]
        v = v_ref[0, pl.ds(start, width), :]
        reps = width // LANES
        logits = [lax.dot_general(qm_ref[mp], k, NT_DIMS, preferred_element_type=F32) for mp in range(2)]
        for mp in range(2):
            s = logits[mp]
            if bias_idx is not None:
                s = s + bias_ref[bias_idx]
            m_prev = m_ref[mp]
            m_new = jnp.maximum(m_prev, jnp.max(s, axis=1, keepdims=True))
            alpha = jnp.exp2(m_prev - m_new)
            p = jnp.exp2(s - jnp.tile(m_new, (1, reps)))
            psum = p[:, :LANES]
            for r in range(1, reps):
                psum = psum + p[:, r * LANES:(r + 1) * LANES]
            l_ref[mp] = alpha * l_ref[mp] + psum
            acc_ref[mp] = alpha * acc_ref[mp] + jnp.dot(p.astype(BF16), v, preferred_element_type=F32)
            m_ref[mp] = m_new

    def far(first, count):
        def pair(j, c):
            step(first + 2 * j, 2 * t, None)
            return c

        def single(j, c):
            step(first + count - 1, t, None)
            return c

        lax.fori_loop(0, count // 2, pair, 0)
        lax.fori_loop(0, count % 2, single, 0)

    def near(ki, c):
        step(ki, t, ki - qi + 1)
        return c

    first_near = jnp.maximum(qi - 1, 0)
    first_right = jnp.minimum(qi + 2, nk)
    far(0, first_near)
    m_ref[...] += c_left
    lax.fori_loop(first_near, first_right, near, 0)
    m_ref[...] -= c_right
    far(first_right, nk - first_right)

    outs =[acc_ref[mp] / jnp.sum(l_ref[mp], axis=1, keepdims=True) for mp in range(2)]
    lam = (jnp.exp(jnp.sum(lq1_ref[...] * lk1_ref[...], axis=1, keepdims=True))
           - jnp.exp(jnp.sum(lq2_ref[...] * lk2_ref[...], axis=1, keepdims=True))
           + LAMBDA_INIT)
    o = outs[0] - lam * outs[1]
    o = _rms(o, g_ref[...], SUBLN_EPS) * (1.0 - LAMBDA_INIT)
    o_ref[0] = o.astype(o_ref.dtype)


def _diff_attention(proj3, tbl, lq1, lk1, lq2, lk2, subln_g, t):
    b, s, _ = proj3.shape
    assert t >= MAX_DISTANCE and s % t == 0
    vec = lambda n: pl.BlockSpec((1, n), lambda bi, hi, qi: (0, 0))
    kern = functools.partial(_da_kernel, t=t, seq=s)
    return pl.pallas_call(
        kern,
        out_shape=jax.ShapeDtypeStruct((b, s, DA_V), BF16),
        grid=(b, DA_HEADS, s // t),
        in_specs=[pl.BlockSpec(memory_space=pltpu.SMEM),
                  vec(HEAD_DIM), vec(HEAD_DIM), vec(HEAD_DIM), vec(HEAD_DIM),
                  pl.BlockSpec((1, t, LANES), lambda bi, hi, qi: (bi, qi, hi)),
                  pl.BlockSpec((1, s, LANES), lambda bi, hi, qi: (bi, 0, DA_HEADS + hi)),
                  pl.BlockSpec((1, s, LANES), lambda bi, hi, qi: (bi, 0, 2 * DA_HEADS + hi)),
                  vec(2 * HEAD_DIM)],
        out_specs=pl.BlockSpec((1, t, LANES), lambda bi, hi, qi: (bi, qi, hi)),
        scratch_shapes=[pltpu.VMEM((2, t, LANES), BF16), pltpu.VMEM((2, t, LANES), F32),
                        pltpu.VMEM((2, t, LANES), F32), pltpu.VMEM((2, t, 2 * HEAD_DIM), F32),
                        pltpu.VMEM((3, t, t), F32)],
        compiler_params=_params(("parallel", "parallel", "arbitrary")),
        name="diff_attention",
    )(tbl, lq1, lk1, lq2, lk2, proj3, proj3, proj3, subln_g)


def _wa_kernel(tbl_ref, sink_ref, q_ref, k_ref, v_ref, o_ref, *, tq, seq):
    q0 = pl.program_id(1) * tq
    win = tq + 2 * WINDOW
    ws = pl.multiple_of(jnp.clip(q0 - WINDOW, 0, seq - win), LANES)
    kw = k_ref[0, pl.ds(ws, win), :]
    vw = v_ref[0, pl.ds(ws, win), :]
    q = q_ref[0]
    jj = lax.broadcasted_iota(I32, (1, win + tq), 1)
    rel = ws - q0 + jj - (tq - 1)
    valid = jnp.abs(rel) <= WINDOW
    bucket = _t5_bucket(jnp.clip(rel, -MAX_DISTANCE, MAX_DISTANCE))
    outs = []
    for hq in range(WA_Q_HEADS):
        g = hq // WA_REP
        row = jnp.where(valid, _bias_lookup(bucket, tbl_ref, DA_HEADS + hq), -jnp.inf)
        s = lax.dot_general(q[:, hq * HEAD_DIM:(hq + 1) * HEAD_DIM],
                            kw[:, g * HEAD_DIM:(g + 1) * HEAD_DIM], NT_DIMS,
                            preferred_element_type=F32)
        s = s + _toeplitz(row, tq, win)
        sk = sink_ref[0, hq]
        m = jnp.maximum(jnp.max(s, axis=1, keepdims=True), sk)
        e = jnp.exp(s - m)
        denom = jnp.sum(e, axis=1, keepdims=True) + jnp.exp(sk - m)
        pv = jnp.dot(e.astype(BF16), vw[:, g * HEAD_DIM:(g + 1) * HEAD_DIM],
                     preferred_element_type=F32)
        outs.append(pv / denom)
    o_ref[0] = jnp.concatenate(outs, axis=1).astype(o_ref.dtype)


def _windowed_gqa(proj3, tbl, sink, tq):
    b, s, _ = proj3.shape
    kern = functools.partial(_wa_kernel, tq=tq, seq=s)
    kv_blk = (IN_COLS - 2 * LANES) // LANES
    return pl.pallas_call(
        kern,
        out_shape=jax.ShapeDtypeStruct((b, s, WA_Q), BF16),
        grid=(b, s // tq),
        in_specs=[pl.BlockSpec(memory_space=pltpu.SMEM),
                  pl.BlockSpec(memory_space=pltpu.SMEM),
                  pl.BlockSpec((1, tq, WA_Q), lambda bi, qi: (bi, qi, 3)),
                  pl.BlockSpec((1, s, LANES), lambda bi, qi: (bi, 0, kv_blk)),
                  pl.BlockSpec((1, s, LANES), lambda bi, qi: (bi, 0, kv_blk + 1))],
        out_specs=pl.BlockSpec((1, tq, WA_Q), lambda bi, qi: (bi, qi, 0)),
        compiler_params=_params(("parallel", "arbitrary")),
        name="windowed_gqa",
    )(tbl, sink, proj3, proj3, proj3)


def _oproj_kernel(a_ref, b_ref, x_ref, w1_ref, w2_ref, g_ref, wr_ref, x1_ref, h_ref, aff_ref):
    x1 = (x_ref[...] + jnp.dot(a_ref[...], w1_ref[...], preferred_element_type=F32)
          + jnp.dot(b_ref[...], w2_ref[...], preferred_element_type=F32))
    x1_ref[...] = x1
    h = _rms(x1, g_ref[...], RMS_EPS)
    h_ref[...] = h.astype(h_ref.dtype)
    logits = lax.dot_general(wr_ref[...], h, NT_DIMS, precision=lax.Precision.HIGHEST,
                             preferred_element_type=F32)
    e = jnp.exp(logits - jnp.max(logits, axis=0, keepdims=True))
    aff_ref[...] = e / jnp.sum(e, axis=0, keepdims=True)


def _oproj(a, b, x2, w1, w2, g, wr_t, tm):
    m = x2.shape[0]
    row = lambda n: pl.BlockSpec((tm, n), lambda i: (i, 0))
    full = lambda r, c: pl.BlockSpec((r, c), lambda i: (0, 0))
    return pl.pallas_call(
        _oproj_kernel,
        out_shape=(jax.ShapeDtypeStruct((m, D_MODEL), F32),
                   jax.ShapeDtypeStruct((m, D_MODEL), BF16),
                   jax.ShapeDtypeStruct((N_EXPERTS, m), F32)),
        grid=(m // tm,),
        in_specs=[row(DA_V), row(WA_Q), row(D_MODEL), full(DA_V, D_MODEL), full(WA_Q, D_MODEL),
                  full(1, D_MODEL), full(N_EXPERTS, D_MODEL)],
        out_specs=(row(D_MODEL), row(D_MODEL), pl.BlockSpec((N_EXPERTS, tm), lambda i: (0, i))),
        compiler_params=_params(("parallel",)),
        name="oproj_router",
    )(a, b, x2, w1, w2, g, wr_t)


def _cumsum_lanes(x):
    n = x.shape[1]
    lane = lax.broadcasted_iota(I32, x.shape, 1)
    shift = 1
    while shift < n:
        x = x + jnp.where(lane >= shift, pltpu.roll(x, shift, 1), 0)
        shift *= 2
    return x


def _select_kernel(aff_ref, slot_ref, excl_ref, *, cap):
    bits = pltpu.bitcast(aff_ref[...], I32)

    def count(mask):
        return jnp.sum(mask.astype(F32), axis=1, keepdims=True)

    def body(i, prefix):
        cand = prefix | jnp.left_shift(jnp.int32(1), 30 - i)
        return jnp.where(count(bits >= cand) >= cap, cand, prefix)

    thr = lax.fori_loop(0, 31, body, jnp.zeros((bits.shape[0], 1), I32))
    gt = bits > thr
    eq = bits == thr
    need = cap - count(gt)
    eq_i = eq.astype(I32)
    eq_excl = (_cumsum_lanes(eq_i) - eq_i).astype(F32)
    sel = gt | (eq & (eq_excl < need))
    sel_i = sel.astype(I32)
    excl = _cumsum_lanes(sel_i) - sel_i
    slot_ref[...] = jnp.where(sel, excl, -1)
    excl_ref[...] = excl


def _select(aff_t, cap):
    shp = jax.ShapeDtypeStruct(aff_t.shape, I32)
    return pl.pallas_call(
        functools.partial(_select_kernel, cap=cap),
        out_shape=(shp, shp),
        compiler_params=_params(None),
        name="expert_select",
    )(aff_t)


def _gather_kernel(cnt_ref, x_ref, slot_ref, xg_hbm, obuf, sem, xo, xsem, *, nb, rw):
    b = pl.program_id(0)
    tb = x_ref.shape[0]
    cur = b % 2
    chunks = D_MODEL // LANES

    def lo_of(e, blk):
        return cnt_ref[e * (nb + 1) + blk]

    def out_copy(e, blk, slab):
        return pltpu.make_async_copy(obuf.at[slab, pl.ds(e * rw, rw)],
                                     xg_hbm.at[e, pl.ds(lo_of(e, blk), rw)], sem.at[slab, e])

    def pack(rel):
        rows = lax.broadcasted_iota(I32, (rw, tb), 0)
        return jnp.where(rel == rows, 1.0, 0.0).astype(BF16)

    @pl.when(b == 0)
    def _():
        xo[...] = jnp.zeros(xo.shape, xo.dtype)
        for e in range(N_EXPERTS):
            cp = pltpu.make_async_copy(xo, xg_hbm.at[e, pl.ds(xg_hbm.shape[1] - rw, rw)], xsem)
            cp.start()
            cp.wait()

    x = x_ref[...]
    onehot = jnp.concatenate([pack(slot_ref[e:e + 1, :] - lo_of(e, b)) for e in range(N_EXPERTS)], axis=0)
    packed = jnp.dot(onehot, x, preferred_element_type=F32)
    for j in range(chunks):
        obuf[cur, :, j, :] = packed[:, j * LANES:(j + 1) * LANES]

    @pl.when(b > 0)
    def _():
        for e in range(N_EXPERTS):
            out_copy(e, b - 1, 1 - cur).wait()

    for e in range(N_EXPERTS):
        out_copy(e, b, cur).start()

    for e in range(N_EXPERTS):
        lo = lo_of(e, b)
        n_extra = (jnp.maximum(lo_of(e, b + 1) - lo - rw, 0) + rw - 1) // rw

        def extra(w, c, e=e, lo=lo):
            first = lo + (w + 1) * rw
            more = jnp.dot(pack(slot_ref[e:e + 1, :] - first), x, preferred_element_type=F32)
            for j in range(chunks):
                xo[:, j, :] = more[:, j * LANES:(j + 1) * LANES]
            cp = pltpu.make_async_copy(xo, xg_hbm.at[e, pl.ds(first, rw)], xsem)
            cp.start()
            cp.wait()
            return c

        lax.fori_loop(0, n_extra, extra, 0)

    @pl.when(b == nb - 1)
    def _():
        for e in range(N_EXPERTS):
            out_copy(e, b, cur).wait()


def _gather(cnt, h2, slot, cap, tb):
    m = h2.shape[0]
    nb = m // tb
    rw = LANES // 2
    chunks = D_MODEL // LANES
    return pl.pallas_call(
        functools.partial(_gather_kernel, nb=nb, rw=rw),
        out_shape=jax.ShapeDtypeStruct((N_EXPERTS, cap + rw, chunks, LANES), F32),
        grid_spec=pltpu.PrefetchScalarGridSpec(
            num_scalar_prefetch=1,
            grid=(nb,),
            in_specs=[pl.BlockSpec((tb, D_MODEL), lambda b, c: (b, 0)),
                      pl.BlockSpec((N_EXPERTS, tb), lambda b, c: (0, b))],
            out_specs=pl.BlockSpec(memory_space=pl.ANY),
            scratch_shapes=[pltpu.VMEM((2, N_EXPERTS * rw, chunks, LANES), F32),
                            pltpu.SemaphoreType.DMA((2, N_EXPERTS)),
                            pltpu.VMEM((rw, chunks, LANES), F32),
                            pltpu.SemaphoreType.DMA(())]),
        compiler_params=_params(("arbitrary",)),
        name="moe_gather",
    )(cnt, h2, slot)


def _ffn_kernel(x_ref, wg_ref, wu_ref, wd_ref, y_ref):
    x = jnp.concatenate([x_ref[0, :, j, :] for j in range(x_ref.shape[2])], axis=1).astype(BF16)
    g = jnp.dot(x, wg_ref[0], preferred_element_type=F32)
    u = jnp.dot(x, wu_ref[0], preferred_element_type=F32)
    h = (g * jax.nn.sigmoid(g) * u).astype(BF16)
    y_ref[0] = jnp.dot(h, wd_ref[0], preferred_element_type=F32).astype(y_ref.dtype)


def _ffn(xg, wg, wu, wd, cap, tm):
    chunks = xg.shape[2]
    wspec = lambda r, c: pl.BlockSpec((1, r, c), lambda e, i: (e, 0, 0))
    return pl.pallas_call(
        _ffn_kernel,
        out_shape=jax.ShapeDtypeStruct((N_EXPERTS, cap, D_MODEL), BF16),
        grid=(N_EXPERTS, cap // tm),
        in_specs=[pl.BlockSpec((1, tm, chunks, LANES), lambda e, i: (e, i, 0, 0)),
                  wspec(D_MODEL, EXPERT_FF), wspec(D_MODEL, EXPERT_FF), wspec(EXPERT_FF, D_MODEL)],
        out_specs=pl.BlockSpec((1, tm, D_MODEL), lambda e, i: (e, i, 0)),
        compiler_params=_params(("parallel", "arbitrary")),
        name="moe_ffn",
    )(xg, wg, wu, wd)


def _combine_kernel(cnt_ref, x1_ref, slot_ref, gate_ref, fn_ref, y_hbm, o_ref,
                    buf, sem, xbuf, xsem, acc_ref, *, nb, cap, rw):
    b = pl.program_id(0)
    tb = x1_ref.shape[0]
    xr = xbuf.shape[0]

    def window_start(e, blk):
        lo = cnt_ref[e * (nb + 1) + blk]
        return pl.multiple_of(jnp.minimum((lo // BF16_SUBLANES) * BF16_SUBLANES, cap - rw), BF16_SUBLANES)

    def window_copy(e, start, slab):
        return pltpu.make_async_copy(y_hbm.at[e, pl.ds(start, rw), :],
                                     buf.at[slab, pl.ds(e * rw, rw), :], sem.at[slab, e])

    def fetch(blk, slab):
        for e in range(N_EXPERTS):
            window_copy(e, window_start(e, blk), slab).start()

    cur = b % 2

    @pl.when(b == 0)
    def _():
        fetch(0, 0)

    @pl.when(b + 1 < nb)
    def _():
        fetch(b + 1, 1 - cur)

    starts = [window_start(e, b) for e in range(N_EXPERTS)]
    for e in range(N_EXPERTS):
        window_copy(e, starts[e], cur).wait()

    width = N_EXPERTS * rw
    ex = (lax.broadcasted_iota(I32, (N_EXPERTS, width), 0)
          == lax.broadcasted_iota(I32, (N_EXPERTS, width), 1) // rw).astype(BF16)
    lane16 = lax.broadcasted_iota(I32, (1, N_EXPERTS), 1)
    st_vec = jnp.zeros((1, N_EXPERTS), I32)
    for e in range(N_EXPERTS):
        st_vec = jnp.where(lane16 == e, starts[e], st_vec)
    rel = jnp.clip(slot_ref[...] - st_vec, -1, rw).astype(F32).astype(BF16)
    row = (lax.broadcasted_iota(I32, (tb, width), 1) % rw).astype(F32)
    hit = jnp.dot(rel, ex, preferred_element_type=F32) == row
    gate = gate_ref[...]
    g_hi = gate.astype(BF16)
    g_lo = (gate - g_hi.astype(F32)).astype(BF16)
    p_hi = jnp.where(hit, jnp.dot(g_hi, ex, preferred_element_type=F32), 0.0).astype(BF16)
    p_lo = jnp.where(hit, jnp.dot(g_lo, ex, preferred_element_type=F32), 0.0).astype(BF16)
    y = buf[cur]
    acc_ref[...] = (x1_ref[...] + jnp.dot(p_hi, y, preferred_element_type=F32)
                    + jnp.dot(p_lo, y, preferred_element_type=F32))

    xlane = lax.broadcasted_iota(I32, (tb, xr), 1)
    for e in range(N_EXPERTS):
        done = starts[e] + rw
        n_extra = (jnp.maximum(cnt_ref[e * (nb + 1) + b + 1] - done, 0) + xr - 1) // xr

        def extra(w, c, e=e, done=done):
            first = done + w * xr
            cst = pl.multiple_of(jnp.minimum(first, cap - xr), BF16_SUBLANES)
            cp = pltpu.make_async_copy(y_hbm.at[e, pl.ds(cst, xr), :], xbuf, xsem)
            cp.start()
            cp.wait()
            slot = slot_ref[:, e:e + 1]
            hit = jnp.logical_and(slot - cst == xlane, slot >= first)
            onehot = jnp.where(hit, 1.0, 0.0).astype(BF16)
            acc_ref[...] += gate_ref[:, e:e + 1] * jnp.dot(onehot, xbuf[...], preferred_element_type=F32)
            return c

        lax.fori_loop(0, n_extra, extra, 0)

    o_ref[...] = _rms(acc_ref[...], fn_ref[...], RMS_EPS)


def _combine(cnt, x1, slot_t, gate_t, fn, y, tb):
    m = x1.shape[0]
    nb = m // tb
    cap = y.shape[1]
    rw = LANES // 2
    assert cap % BF16_SUBLANES == 0 and cap >= LANES
    return pl.pallas_call(
        functools.partial(_combine_kernel, nb=nb, cap=cap, rw=rw),
        out_shape=jax.ShapeDtypeStruct((m, D_MODEL), F32),
        grid_spec=pltpu.PrefetchScalarGridSpec(
            num_scalar_prefetch=1,
            grid=(nb,),
            in_specs=[pl.BlockSpec((tb, D_MODEL), lambda b, c: (b, 0)),
                      pl.BlockSpec((tb, N_EXPERTS), lambda b, c: (b, 0)),
                      pl.BlockSpec((tb, N_EXPERTS), lambda b, c: (b, 0)),
                      pl.BlockSpec((1, D_MODEL), lambda b, c: (0, 0)),
                      pl.BlockSpec(memory_space=pl.ANY)],
            out_specs=pl.BlockSpec((tb, D_MODEL), lambda b, c: (b, 0)),
            scratch_shapes=[pltpu.VMEM((2, N_EXPERTS * rw, D_MODEL), BF16),
                            pltpu.SemaphoreType.DMA((2, N_EXPERTS)),
                            pltpu.VMEM((LANES, D_MODEL), BF16),
                            pltpu.SemaphoreType.DMA(()),
                            pltpu.VMEM((tb, D_MODEL), F32)]),
        compiler_params=_params(("arbitrary",)),
        name="moe_combine",
    )(cnt, x1, slot_t, gate_t, fn, y)


def _pick(n, prefs):
    for p in prefs:
        if n % p == 0:
            return p
    raise ValueError(f"no tile in {prefs} divides {n}")


def _trunk(x, tbl, sink, lam, subln_g, norm1, w_in, w_o1, w_o2, norm2, wr_t, wg, wu, wd, fn):
    bsz, s, d = x.shape
    m = bsz * s
    cap = CAPACITY_FACTOR * m // N_EXPERTS
    x2 = x.reshape(m, d)
    proj = _inproj(x2, norm1, w_in, _pick(m, (512, 256)))
    proj3 = proj.reshape(bsz, s, IN_COLS)
    out_a = _diff_attention(proj3, tbl, *lam, subln_g, _pick(s, (512, 256)))
    out_b = _windowed_gqa(proj3, tbl, sink, _pick(s, (256,)))
    x1, h2, aff_t = _oproj(out_a.reshape(m, DA_V), out_b.reshape(m, WA_Q), x2, w_o1, w_o2,
                           norm2, wr_t, _pick(m, (512, 256)))
    slot, excl = _select(aff_t, cap)
    tb = 256
    nb = m // tb
    cnt = jnp.concatenate([excl[:, ::tb], jnp.full((N_EXPERTS, 1), cap, I32)], axis=1).reshape(-1)
    xg = _gather(cnt, h2, slot, cap, tb)
    y = _ffn(xg, wg, wu, wd, cap, _pick(cap, (512, 256, 128)))
    out = _combine(cnt, x1, slot.T, aff_t.T, fn, y, tb)
    return out.reshape(bsz, s, d)


def kernel(x_prompt, x_sample, rel_bias, norm1, w_in, lam_q1, lam_k1, lam_q2, lam_k2, subln_g, sink,
           w_o, norm2, w_router, w_gate, w_up, w_down, final_norm):
    scale = HEAD_DIM ** -0.5
    col = jnp.arange(IN_COLS)
    is_da_q = col < DA_HEADS * 2 * HEAD_DIM
    is_wa_q = (col >= 3 * DA_V) & (col < 3 * DA_V + WA_Q)
    col_scale = jnp.where(is_da_q, scale * LOG2E, jnp.where(is_wa_q, scale, 1.0)).astype(F32)
    w_in_s = (w_in[0] * col_scale).astype(BF16)
    args = dict(
        tbl=rel_bias, sink=sink,
        lam=(lam_q1, lam_k1, lam_q2, lam_k2), subln_g=subln_g,
        norm1=norm1, w_in=w_in_s,
        w_o1=w_o[0, :DA_V].astype(BF16), w_o2=w_o[0, DA_V:].astype(BF16),
        norm2=norm2, wr_t=w_router[0].T,
        wg=w_gate[0].astype(BF16), wu=w_up[0].astype(BF16), wd=w_down[0].astype(BF16),
        fn=final_norm.reshape(1, D_MODEL))
    return (_trunk(x_prompt, **args), _trunk(x_sample, **args))
```

```python
import functools
import math

import jax
import jax.numpy as jnp
from jax import lax
from jax.experimental import pallas as pl
from jax.experimental.pallas import tpu as pltpu

F32 = jnp.float32
BF16 = jnp.bfloat16
I32 = jnp.int32

D_MODEL = 1024
HEAD_DIM = 64
DA_HEADS = 4
WA_Q_HEADS = 8
WA_REP = 4
WINDOW = 128
IN_COLS = 2304
DA_V = 512
WA_Q = 512
N_BUCKETS = 32
MAX_DISTANCE = 128
N_EXPERTS = 16
EXPERT_FF = 1024
CAPACITY_FACTOR = 2
RMS_EPS = 1e-6
SUBLN_EPS = 1e-5
LAMBDA_INIT = 0.8 - 0.6 * math.exp(-0.3 * 0)
LOG2E = math.log2(math.e)

LANES = 128
BF16_SUBLANES = 16
VMEM_LIMIT = 56 * 1024 * 1024
NT_DIMS = (((1,), (1,)), ((), ()))


def _params(sem, vmem=VMEM_LIMIT):
    return pltpu.CompilerParams(dimension_semantics=sem, vmem_limit_bytes=vmem)


def _rms(x, g, eps):
    return x * lax.rsqrt(jnp.mean(x * x, axis=-1, keepdims=True) + eps) * g


def _inproj_kernel(x_ref, g_ref, w_ref, o_ref):
    h = _rms(x_ref[...], g_ref[...], RMS_EPS).astype(BF16)
    o_ref[...] = jnp.dot(h, w_ref[...], preferred_element_type=F32).astype(o_ref.dtype)


def _inproj(x2, g, w, tm):
    m = x2.shape[0]
    return pl.pallas_call(
        _inproj_kernel,
        out_shape=jax.ShapeDtypeStruct((m, IN_COLS), BF16),
        grid=(m // tm,),
        in_specs=[pl.BlockSpec((tm, D_MODEL), lambda i: (i, 0)),
                  pl.BlockSpec((1, D_MODEL), lambda i: (0, 0)),
                  pl.BlockSpec((D_MODEL, IN_COLS), lambda i: (0, 0))],
        out_specs=pl.BlockSpec((tm, IN_COLS), lambda i: (i, 0)),
        compiler_params=_params(("parallel",)),
        name="inproj",
    )(x2, g, w)


def _t5_bucket(rel):
    nb = N_BUCKETS // 2
    max_exact = nb // 2
    ret = jnp.where(rel > 0, nb, 0)
    n = jnp.abs(rel)
    nf = jnp.maximum(n, 1).astype(F32)
    large = max_exact + (jnp.log(nf / max_exact) / math.log(MAX_DISTANCE / max_exact)
                         * (nb - max_exact)).astype(I32)
    large = jnp.minimum(large, nb - 1)
    return ret + jnp.where(n < max_exact, n, large)


def _bias_lookup(bucket, tbl_ref, col):
    out = jnp.zeros(bucket.shape, F32)
    for b in range(N_BUCKETS):
        out = jnp.where(bucket == b, tbl_ref[b, col], out)
    return out


def _toeplitz(row, tq, tk):
    width = tq + tk
    rb = jnp.broadcast_to(row, (tq, width))
    rolled = pltpu.roll(rb, width - tq + 1, 1, stride=1, stride_axis=0)
    return rolled[:, :tk]


def _da_kernel(tbl_ref, lq1_ref, lk1_ref, lq2_ref, lk2_ref, q_ref, k_ref, v_ref, g_ref,
               o_ref, qm_ref, m_ref, l_ref, acc_ref, bias_ref, *, t, seq):
    h = pl.program_id(1)
    qi = pl.program_id(2)
    nk = seq // t

    @pl.when(qi == 0)
    def _():
        for d in range(3):
            jj = lax.broadcasted_iota(I32, (1, 2 * t), 1)
            rel = jnp.clip((d - 1) * t + jj - (t - 1), -MAX_DISTANCE, MAX_DISTANCE)
            bias_ref[d] = _toeplitz(_bias_lookup(_t5_bucket(rel), tbl_ref, h) * LOG2E, t, t)

    c_left = tbl_ref[N_BUCKETS // 2 - 1, h] * LOG2E
    c_right = tbl_ref[N_BUCKETS - 1, h] * LOG2E

    q = q_ref[0]
    lane = lax.broadcasted_iota(I32, q.shape, 1)
    qm_ref[0] = jnp.where(lane < HEAD_DIM, q, jnp.zeros_like(q))
    qm_ref[1] = jnp.where(lane >= HEAD_DIM, q, jnp.zeros_like(q))
    m_ref[...] = jnp.full(m_ref.shape, -jnp.inf, F32)
    l_ref[...] = jnp.zeros(l_ref.shape, F32)
    acc_ref[...] = jnp.zeros(acc_ref.shape, F32)

    def step(tile, width, bias_idx):
        start = pl.multiple_of(tile * t, t)
        k = k_ref[0, pl.ds(start, width), :]
        v = v_ref[0, pl.ds(start, width), :]
        reps = width // LANES
        logits = [lax.dot_general(qm_ref[mp], k, NT_DIMS, preferred_element_type=F32) for mp in range(2)]
        for mp in range(2):
            s = logits[mp]
            if bias_idx is not None:
                s = s + bias_ref[bias_idx]
            m_prev = m_ref[mp]
            m_new = jnp.maximum(m_prev, jnp.max(s, axis=1, keepdims=True))
            alpha = jnp.exp2(m_prev - m_new)
            p = jnp.exp2(s - jnp.tile(m_new, (1, reps)))
            psum = p[:, :LANES]
            for r in range(1, reps):
                psum = psum + p[:, r * LANES:(r + 1) * LANES]
            l_ref[mp] = alpha * l_ref[mp] + psum
            acc_ref[mp] = alpha * acc_ref[mp] + jnp.dot(p.astype(BF16), v, preferred_element_type=F32)
            m_ref[mp] = m_new

    def far(first, count):
        def pair(j, c):
            step(first + 2 * j, 2 * t, None)
            return c

        def single(j, c):
            step(first + count - 1, t, None)
            return c

        lax.fori_loop(0, count // 2, pair, 0)
        lax.fori_loop(0, count % 2, single, 0)

    def near(ki, c):
        step(ki, t, ki - qi + 1)
        return c

    first_near = jnp.maximum(qi - 1, 0)
    first_right = jnp.minimum(qi + 2, nk)
    far(0, first_near)
    m_ref[...] += c_left
    lax.fori_loop(first_near, first_right, near, 0)
    m_ref[...] -= c_right
    far(first_right, nk - first_right)

    outs = [acc_ref[mp] / jnp.sum(l_ref[mp], axis=1, keepdims=True) for mp in range(2)]
    lam = (jnp.exp(jnp.sum(lq1_ref[...] * lk1_ref[...], axis=1, keepdims=True))
           - jnp.exp(jnp.sum(lq2_ref[...] * lk2_ref[...], axis=1, keepdims=True))
           + LAMBDA_INIT)
    o = outs[0] - lam * outs[1]
    o = _rms(o, g_ref[...], SUBLN_EPS) * (1.0 - LAMBDA_INIT)
    o_ref[0] = o.astype(o_ref.dtype)


def _diff_attention(proj3, tbl, lq1, lk1, lq2, lk2, subln_g, t):
    b, s, _ = proj3.shape
    assert t >= MAX_DISTANCE and s % t == 0
    vec = lambda n: pl.BlockSpec((1, n), lambda bi, hi, qi: (0, 0))
    kern = functools.partial(_da_kernel, t=t, seq=s)
    return pl.pallas_call(
        kern,
        out_shape=jax.ShapeDtypeStruct((b, s, DA_V), BF16),
        grid=(b, DA_HEADS, s // t),
        in_specs=[pl.BlockSpec(memory_space=pltpu.SMEM),
                  vec(HEAD_DIM), vec(HEAD_DIM), vec(HEAD_DIM), vec(HEAD_DIM),
                  pl.BlockSpec((1, t, LANES), lambda bi, hi, qi: (bi, qi, hi)),
                  pl.BlockSpec((1, s, LANES), lambda bi, hi, qi: (bi, 0, DA_HEADS + hi)),
                  pl.BlockSpec((1, s, LANES), lambda bi, hi, qi: (bi, 0, 2 * DA_HEADS + hi)),
                  vec(2 * HEAD_DIM)],
        out_specs=pl.BlockSpec((1, t, LANES), lambda bi, hi, qi: (bi, qi, hi)),
        scratch_shapes=[pltpu.VMEM((2, t, LANES), BF16), pltpu.VMEM((2, t, LANES), F32),
                        pltpu.VMEM((2, t, LANES), F32), pltpu.VMEM((2, t, 2 * HEAD_DIM), F32),
                        pltpu.VMEM((3, t, t), F32)],
        compiler_params=_params(("parallel", "parallel", "arbitrary")),
        name="diff_attention",
    )(tbl, lq1, lk1, lq2, lk2, proj3, proj3, proj3, subln_g)


def _wa_kernel(tbl_ref, sink_ref, q_ref, k_ref, v_ref, o_ref, bias_ref, *, tq, seq):
    q0 = pl.program_id(1) * tq
    win = tq + 2 * WINDOW
    ws = pl.multiple_of(jnp.clip(q0 - WINDOW, 0, seq - win), LANES)

    @pl.when(jnp.logical_and(pl.program_id(0) == 0, pl.program_id(1) == 0))
    def _():
        jj = lax.broadcasted_iota(I32, (1, win + tq), 1)
        for d in range(3):
            rel = jj - d * WINDOW - (tq - 1)
            valid = jnp.abs(rel) <= WINDOW
            bucket = _t5_bucket(jnp.clip(rel, -MAX_DISTANCE, MAX_DISTANCE))
            for hq in range(WA_Q_HEADS):
                row = jnp.where(valid, _bias_lookup(bucket, tbl_ref, DA_HEADS + hq) * LOG2E, -jnp.inf)
                bias_ref[d, hq] = _toeplitz(row, tq, win)

    variant = (q0 - ws) // WINDOW
    kw = k_ref[0, pl.ds(ws, win), :]
    vw = v_ref[0, pl.ds(ws, win), :]
    q = q_ref[0]
    low = lax.broadcasted_iota(I32, (tq, LANES), 1) < HEAD_DIM
    kdup = [jnp.concatenate([kw[:, g * HEAD_DIM:(g + 1) * HEAD_DIM]] * 2, axis=1) for g in range(2)]
    ones = jnp.ones((win, LANES), BF16)
    vext = [jnp.concatenate([vw[:, g * HEAD_DIM:(g + 1) * HEAD_DIM]] * 2 + [ones], axis=1) for g in range(2)]
    logits = []
    for hq in range(WA_Q_HEADS):
        qp = q[:, (hq // 2) * LANES:(hq // 2 + 1) * LANES]
        qm = jnp.where(low if hq % 2 == 0 else jnp.logical_not(low), qp, jnp.zeros_like(qp))
        logits.append(lax.dot_general(qm, kdup[hq // WA_REP], NT_DIMS, preferred_element_type=F32))
    outs = []
    for pair in range(WA_Q_HEADS // 2):
        g = (2 * pair) // WA_REP
        halves = []
        for half in range(2):
            hq = 2 * pair + half
            s = logits[hq] + bias_ref[variant, hq]
            sk = sink_ref[0, hq] * LOG2E
            m = jnp.maximum(jnp.full((tq, LANES), sk, F32), jnp.max(s, axis=1, keepdims=True))
            e = jnp.exp2(s - jnp.tile(m, (1, win // LANES))).astype(BF16)
            pv = jnp.dot(e, vext[g], preferred_element_type=F32)
            halves.append(pv[:, :LANES] / (pv[:, LANES:] + jnp.exp2(sk - m)))
        outs.append(jnp.where(low, halves[0], halves[1]))
    o_ref[0] = jnp.concatenate(outs, axis=1).astype(o_ref.dtype)


def _windowed_gqa(proj3, tbl, sink, tq):
    b, s, _ = proj3.shape
    kern = functools.partial(_wa_kernel, tq=tq, seq=s)
    kv_blk = (IN_COLS - 2 * LANES) // LANES
    return pl.pallas_call(
        kern,
        out_shape=jax.ShapeDtypeStruct((b, s, WA_Q), BF16),
        grid=(b, s // tq),
        in_specs=[pl.BlockSpec(memory_space=pltpu.SMEM),
                  pl.BlockSpec(memory_space=pltpu.SMEM),
                  pl.BlockSpec((1, tq, WA_Q), lambda bi, qi: (bi, qi, 3)),
                  pl.BlockSpec((1, s, LANES), lambda bi, qi: (bi, 0, kv_blk)),
                  pl.BlockSpec((1, s, LANES), lambda bi, qi: (bi, 0, kv_blk + 1))],
        out_specs=pl.BlockSpec((1, tq, WA_Q), lambda bi, qi: (bi, qi, 0)),
        scratch_shapes=[pltpu.VMEM((3, WA_Q_HEADS, tq, tq + 2 * WINDOW), F32)],
        compiler_params=_params(("arbitrary", "arbitrary")),
        name="windowed_gqa",
    )(tbl, sink, proj3, proj3, proj3)


def _oproj_kernel(a_ref, b_ref, x_ref, w1_ref, w2_ref, g_ref, wr_ref, x1_ref, h_ref, aff_ref):
    x1 = (x_ref[...] + jnp.dot(a_ref[...], w1_ref[...], preferred_element_type=F32)
          + jnp.dot(b_ref[...], w2_ref[...], preferred_element_type=F32))
    x1_ref[...] = x1
    h = _rms(x1, g_ref[...], RMS_EPS)
    h_ref[...] = h.astype(h_ref.dtype)
    logits = lax.dot_general(wr_ref[...], h, NT_DIMS, precision=lax.Precision.HIGHEST,
                             preferred_element_type=F32)
    e = jnp.exp(logits - jnp.max(logits, axis=0, keepdims=True))
    aff_ref[...] = e / jnp.sum(e, axis=0, keepdims=True)


def _oproj(a, b, x2, w1, w2, g, wr_t, tm):
    m = x2.shape[0]
    row = lambda n: pl.BlockSpec((tm, n), lambda i: (i, 0))
    full = lambda r, c: pl.BlockSpec((r, c), lambda i: (0, 0))
    return pl.pallas_call(
        _oproj_kernel,
        out_shape=(jax.ShapeDtypeStruct((m, D_MODEL), F32),
                   jax.ShapeDtypeStruct((m, D_MODEL), BF16),
                   jax.ShapeDtypeStruct((N_EXPERTS, m), F32)),
        grid=(m // tm,),
        in_specs=[row(DA_V), row(WA_Q), row(D_MODEL), full(DA_V, D_MODEL), full(WA_Q, D_MODEL),
                  full(1, D_MODEL), full(N_EXPERTS, D_MODEL)],
        out_specs=(row(D_MODEL), row(D_MODEL), pl.BlockSpec((N_EXPERTS, tm), lambda i: (0, i))),
        compiler_params=_params(("parallel",)),
        name="oproj_router",
    )(a, b, x2, w1, w2, g, wr_t)


def _cumsum_lanes(x):
    n = x.shape[1]
    lane = lax.broadcasted_iota(I32, x.shape, 1)
    shift = 1
    while shift < n:
        x = x + jnp.where(lane >= shift, pltpu.roll(x, shift, 1), 0)
        shift *= 2
    return x


def _select_kernel(aff_ref, slot_ref, excl_ref, *, cap):
    bits = pltpu.bitcast(aff_ref[...], I32)

    def count(mask):
        return jnp.sum(mask.astype(F32), axis=1, keepdims=True)

    def body(i, prefix):
        cand = prefix | jnp.left_shift(jnp.int32(1), 30 - i)
        return jnp.where(count(bits >= cand) >= cap, cand, prefix)

    thr = lax.fori_loop(0, 31, body, jnp.zeros((bits.shape[0], 1), I32))
    gt = bits > thr
    eq = bits == thr
    need = cap - count(gt)
    eq_i = eq.astype(I32)
    eq_excl = (_cumsum_lanes(eq_i) - eq_i).astype(F32)
    sel = gt | (eq & (eq_excl < need))
    sel_i = sel.astype(I32)
    excl = _cumsum_lanes(sel_i) - sel_i
    slot_ref[...] = jnp.where(sel, excl, -1)
    excl_ref[...] = excl


def _select(aff_t, cap):
    shp = jax.ShapeDtypeStruct(aff_t.shape, I32)
    return pl.pallas_call(
        functools.partial(_select_kernel, cap=cap),
        out_shape=(shp, shp),
        compiler_params=_params(None),
        name="expert_select",
    )(aff_t)


def _gather_kernel(cnt_ref, x_ref, slot_ref, xg_hbm, obuf, sem, xo, xsem, *, nb, rw):
    b = pl.program_id(0)
    tb = x_ref.shape[0]
    cur = b % 2
    chunks = D_MODEL // LANES

    def lo_of(e, blk):
        return cnt_ref[e * (nb + 1) + blk]

    def out_copy(e, blk, slab):
        return pltpu.make_async_copy(obuf.at[slab, pl.ds(e * rw, rw)],
                                     xg_hbm.at[e, pl.ds(lo_of(e, blk), rw)], sem.at[slab, e])

    def pack(rel):
        rows = lax.broadcasted_iota(I32, (rw, tb), 0)
        return jnp.where(rel == rows, 1.0, 0.0).astype(BF16)

    @pl.when(b == 0)
    def _():
        xo[...] = jnp.zeros(xo.shape, xo.dtype)
        for e in range(N_EXPERTS):
            cp = pltpu.make_async_copy(xo, xg_hbm.at[e, pl.ds(xg_hbm.shape[1] - rw, rw)], xsem)
            cp.start()
            cp.wait()

    x = x_ref[...]
    onehot = jnp.concatenate([pack(slot_ref[e:e + 1, :] - lo_of(e, b)) for e in range(N_EXPERTS)], axis=0)
    packed = jnp.dot(onehot, x, preferred_element_type=F32)
    for j in range(chunks):
        obuf[cur, :, j, :] = packed[:, j * LANES:(j + 1) * LANES]

    @pl.when(b > 0)
    def _():
        for e in range(N_EXPERTS):
            out_copy(e, b - 1, 1 - cur).wait()

    for e in range(N_EXPERTS):
        out_copy(e, b, cur).start()

    for e in range(N_EXPERTS):
        lo = lo_of(e, b)
        n_extra = (jnp.maximum(lo_of(e, b + 1) - lo - rw, 0) + rw - 1) // rw

        def extra(w, c, e=e, lo=lo):
            first = lo + (w + 1) * rw
            more = jnp.dot(pack(slot_ref[e:e + 1, :] - first), x, preferred_element_type=F32)
            for j in range(chunks):
                xo[:, j, :] = more[:, j * LANES:(j + 1) * LANES]
            cp = pltpu.make_async_copy(xo, xg_hbm.at[e, pl.ds(first, rw)], xsem)
            cp.start()
            cp.wait()
            return c

        lax.fori_loop(0, n_extra, extra, 0)

    @pl.when(b == nb - 1)
    def _():
        for e in range(N_EXPERTS):
            out_copy(e, b, cur).wait()


def _gather(cnt, h2, slot, cap, tb):
    m = h2.shape[0]
    nb = m // tb
    rw = LANES // 2
    chunks = D_MODEL // LANES
    return pl.pallas_call(
        functools.partial(_gather_kernel, nb=nb, rw=rw),
        out_shape=jax.ShapeDtypeStruct((N_EXPERTS, cap + rw, chunks, LANES), F32),
        grid_spec=pltpu.PrefetchScalarGridSpec(
            num_scalar_prefetch=1,
            grid=(nb,),
            in_specs=[pl.BlockSpec((tb, D_MODEL), lambda b, c: (b, 0)),
                      pl.BlockSpec((N_EXPERTS, tb), lambda b, c: (0, b))],
            out_specs=pl.BlockSpec(memory_space=pl.ANY),
            scratch_shapes=[pltpu.VMEM((2, N_EXPERTS * rw, chunks, LANES), F32),
                            pltpu.SemaphoreType.DMA((2, N_EXPERTS)),
                            pltpu.VMEM((rw, chunks, LANES), F32),
                            pltpu.SemaphoreType.DMA(())]),
        compiler_params=_params(("arbitrary",)),
        name="moe_gather",
    )(cnt, h2, slot)


def _ffn_kernel(x_ref, wg_ref, wu_ref, wd_ref, y_ref):
    x = jnp.concatenate([x_ref[0, :, j, :] for j in range(x_ref.shape[2])], axis=1).astype(BF16)
    g = jnp.dot(x, wg_ref[0], preferred_element_type=F32)
    u = jnp.dot(x, wu_ref[0], preferred_element_type=F32)
    h = (g * jax.nn.sigmoid(g) * u).astype(BF16)
    y_ref[0] = jnp.dot(h, wd_ref[0], preferred_element_type=F32).astype(y_ref.dtype)


def _ffn(xg, wg, wu, wd, cap, tm):
    chunks = xg.shape[2]
    wspec = lambda r, c: pl.BlockSpec((1, r, c), lambda e, i: (e, 0, 0))
    return pl.pallas_call(
        _ffn_kernel,
        out_shape=jax.ShapeDtypeStruct((N_EXPERTS, cap, D_MODEL), BF16),
        grid=(N_EXPERTS, cap // tm),
        in_specs=[pl.BlockSpec((1, tm, chunks, LANES), lambda e, i: (e, i, 0, 0)),
                  wspec(D_MODEL, EXPERT_FF), wspec(D_MODEL, EXPERT_FF), wspec(EXPERT_FF, D_MODEL)],
        out_specs=pl.BlockSpec((1, tm, D_MODEL), lambda e, i: (e, i, 0)),
        compiler_params=_params(("parallel", "arbitrary")),
        name="moe_ffn",
    )(xg, wg, wu, wd)


def _combine_kernel(cnt_ref, x1_ref, slot_ref, gate_ref, fn_ref, y_hbm, o_ref,
                    buf, sem, xbuf, xsem, acc_ref, *, nb, cap, rw):
    b = pl.program_id(0)
    tb = x1_ref.shape[0]
    xr = xbuf.shape[0]

    def window_start(e, blk):
        lo = cnt_ref[e * (nb + 1) + blk]
        return pl.multiple_of(jnp.minimum((lo // BF16_SUBLANES) * BF16_SUBLANES, cap - rw), BF16_SUBLANES)

    def window_copy(e, start, slab):
        return pltpu.make_async_copy(y_hbm.at[e, pl.ds(start, rw), :],
                                     buf.at[slab, pl.ds(e * rw, rw), :], sem.at[slab, e])

    def fetch(blk, slab):
        for e in range(N_EXPERTS):
            window_copy(e, window_start(e, blk), slab).start()

    cur = b % 2

    @pl.when(b == 0)
    def _():
        fetch(0, 0)

    @pl.when(b + 1 < nb)
    def _():
        fetch(b + 1, 1 - cur)

    starts = [window_start(e, b) for e in range(N_EXPERTS)]
    for e in range(N_EXPERTS):
        window_copy(e, starts[e], cur).wait()

    width = N_EXPERTS * rw
    ex = (lax.broadcasted_iota(I32, (N_EXPERTS, width), 0)
          == lax.broadcasted_iota(I32, (N_EXPERTS, width), 1) // rw).astype(BF16)
    lane16 = lax.broadcasted_iota(I32, (1, N_EXPERTS), 1)
    st_vec = jnp.zeros((1, N_EXPERTS), I32)
    for e in range(N_EXPERTS):
        st_vec = jnp.where(lane16 == e, starts[e], st_vec)
    rel = jnp.clip(slot_ref[...] - st_vec, -1, rw).astype(F32).astype(BF16)
    row = (lax.broadcasted_iota(I32, (tb, width), 1) % rw).astype(F32)
    hit = jnp.dot(rel, ex, preferred_element_type=F32) == row
    gate = gate_ref[...]
    g_hi = gate.astype(BF16)
    g_lo = (gate - g_hi.astype(F32)).astype(BF16)
    p_hi = jnp.where(hit, jnp.dot(g_hi, ex, preferred_element_type=F32), 0.0).astype(BF16)
    p_lo = jnp.where(hit, jnp.dot(g_lo, ex, preferred_element_type=F32), 0.0).astype(BF16)
    y = buf[cur]
    acc_ref[...] = (x1_ref[...] + jnp.dot(p_hi, y, preferred_element_type=F32)
                    + jnp.dot(p_lo, y, preferred_element_type=F32))

    xlane = lax.broadcasted_iota(I32, (tb, xr), 1)
    for e in range(N_EXPERTS):
        done = starts[e] + rw
        n_extra = (jnp.maximum(cnt_ref[e * (nb + 1) + b + 1] - done, 0) + xr - 1) // xr

        def extra(w, c, e=e, done=done):
            first = done + w * xr
            cst = pl.multiple_of(jnp.minimum(first, cap - xr), BF16_SUBLANES)
            cp = pltpu.make_async_copy(y_hbm.at[e, pl.ds(cst, xr), :], xbuf, xsem)
            cp.start()
            cp.wait()
            slot = slot_ref[:, e:e + 1]
            hit = jnp.logical_and(slot - cst == xlane, slot >= first)
            onehot = jnp.where(hit, 1.0, 0.0).astype(BF16)
            acc_ref[...] += gate_ref[:, e:e + 1] * jnp.dot(onehot, xbuf[...], preferred_element_type=F32)
            return c

        lax.fori_loop(0, n_extra, extra, 0)

    o_ref[...] = _rms(acc_ref[...], fn_ref[...], RMS_EPS)


def _combine(cnt, x1, slot_t, gate_t, fn, y, tb):
    m = x1.shape[0]
    nb = m // tb
    cap = y.shape[1]
    rw = LANES // 2
    assert cap % BF16_SUBLANES == 0 and cap >= LANES
    return pl.pallas_call(
        functools.partial(_combine_kernel, nb=nb, cap=cap, rw=rw),
        out_shape=jax.ShapeDtypeStruct((m, D_MODEL), F32),
        grid_spec=pltpu.PrefetchScalarGridSpec(
            num_scalar_prefetch=1,
            grid=(nb,),
            in_specs=[pl.BlockSpec((tb, D_MODEL), lambda b, c: (b, 0)),
                      pl.BlockSpec((tb, N_EXPERTS), lambda b, c: (b, 0)),
                      pl.BlockSpec((tb, N_EXPERTS), lambda b, c: (b, 0)),
                      pl.BlockSpec((1, D_MODEL), lambda b, c: (0, 0)),
                      pl.BlockSpec(memory_space=pl.ANY)],
            out_specs=pl.BlockSpec((tb, D_MODEL), lambda b, c: (b, 0)),
            scratch_shapes=[pltpu.VMEM((2, N_EXPERTS * rw, D_MODEL), BF16),
                            pltpu.SemaphoreType.DMA((2, N_EXPERTS)),
                            pltpu.VMEM((LANES, D_MODEL), BF16),
                            pltpu.SemaphoreType.DMA(()),
                            pltpu.VMEM((tb, D_MODEL), F32)]),
        compiler_params=_params(("arbitrary",)),
        name="moe_combine",
    )(cnt, x1, slot_t, gate_t, fn, y)


def _pick(n, prefs):
    for p in prefs:
        if n % p == 0:
            return p
    raise ValueError(f"no tile in {prefs} divides {n}")


def _trunk(x, tbl, sink, lam, subln_g, norm1, w_in, w_o1, w_o2, norm2, wr_t, wg, wu, wd, fn):
    bsz, s, d = x.shape
    m = bsz * s
    cap = CAPACITY_FACTOR * m // N_EXPERTS
    x2 = x.reshape(m, d)
    proj = _inproj(x2, norm1, w_in, _pick(m, (512, 256)))
    proj3 = proj.reshape(bsz, s, IN_COLS)
    out_a = _diff_attention(proj3, tbl, *lam, subln_g, _pick(s, (512, 256)))
    out_b = _windowed_gqa(proj3, tbl, sink, _pick(s, (256,)))
    x1, h2, aff_t = _oproj(out_a.reshape(m, DA_V), out_b.reshape(m, WA_Q), x2, w_o1, w_o2,
                           norm2, wr_t, _pick(m, (512, 256)))
    slot, excl = _select(aff_t, cap)
    tb = 256
    nb = m // tb
    cnt = jnp.concatenate([excl[:, ::tb], jnp.full((N_EXPERTS, 1), cap, I32)], axis=1).reshape(-1)
    xg = _gather(cnt, h2, slot, cap, tb)
    y = _ffn(xg, wg, wu, wd, cap, _pick(cap, (512, 256, 128)))
    out = _combine(cnt, x1, slot.T, aff_t.T, fn, y, tb)
    return out.reshape(bsz, s, d)


def kernel(x_prompt, x_sample, rel_bias, norm1, w_in, lam_q1, lam_k1, lam_q2, lam_k2, subln_g, sink,
           w_o, norm2, w_router, w_gate, w_up, w_down, final_norm):
    col = jnp.arange(IN_COLS)
    is_q = (col < DA_HEADS * 2 * HEAD_DIM) | ((col >= 3 * DA_V) & (col < 3 * DA_V + WA_Q))
    col_scale = jnp.where(is_q, HEAD_DIM ** -0.5 * LOG2E, 1.0).astype(F32)
    w_in_s = (w_in[0] * col_scale).astype(BF16)
    args = dict(
        tbl=rel_bias, sink=sink,
        lam=(lam_q1, lam_k1, lam_q2, lam_k2), subln_g=subln_g,
        norm1=norm1, w_in=w_in_s,
        w_o1=w_o[0, :DA_V].astype(BF16), w_o2=w_o[0, DA_V:].astype(BF16),
        norm2=norm2, wr_t=w_router[0].T,
        wg=w_gate[0].astype(BF16), wu=w_up[0].astype(BF16), wd=w_down[0].astype(BF16),
        fn=final_norm.reshape(1, D_MODEL))
    return (_trunk(x_prompt, **args), _trunk(x_sample, **args))
```

```python
import functools
import math

import jax
import jax.numpy as jnp
from jax import lax
from jax.experimental import pallas as pl
from jax.experimental.pallas import tpu as pltpu

F32 = jnp.float32
BF16 = jnp.bfloat16
I32 = jnp.int32

D_MODEL = 1024
HEAD_DIM = 64
DA_HEADS = 4
WA_Q_HEADS = 8
WA_REP = 4
WINDOW = 128
IN_COLS = 2304
DA_V = 512
WA_Q = 512
N_BUCKETS = 32
MAX_DISTANCE = 128
N_EXPERTS = 16
EXPERT_FF = 1024
CAPACITY_FACTOR = 2
RMS_EPS = 1e-6
SUBLN_EPS = 1e-5
LAMBDA_INIT = 0.8 - 0.6 * math.exp(-0.3 * 0)
LOG2E = math.log2(math.e)

LANES = 128
BF16_SUBLANES = 16
VMEM_LIMIT = 56 * 1024 * 1024
NT_DIMS = (((1,), (1,)), ((), ()))


def _params(sem, vmem=VMEM_LIMIT):
    return pltpu.CompilerParams(dimension_semantics=sem, vmem_limit_bytes=vmem)


def _rms(x, g, eps):
    return x * lax.rsqrt(jnp.mean(x * x, axis=-1, keepdims=True) + eps) * g


def _inproj_kernel(x_ref, g_ref, w_ref, o_ref):
    h = _rms(x_ref[...], g_ref[...], RMS_EPS).astype(BF16)
    o_ref[...] = jnp.dot(h, w_ref[...], preferred_element_type=F32).astype(o_ref.dtype)


def _inproj(x2, g, w, tm):
    m = x2.shape[0]
    return pl.pallas_call(
        _inproj_kernel,
        out_shape=jax.ShapeDtypeStruct((m, IN_COLS), BF16),
        grid=(m // tm,),
        in_specs=[pl.BlockSpec((tm, D_MODEL), lambda i: (i, 0)),
                  pl.BlockSpec((1, D_MODEL), lambda i: (0, 0)),
                  pl.BlockSpec((D_MODEL, IN_COLS), lambda i: (0, 0))],
        out_specs=pl.BlockSpec((tm, IN_COLS), lambda i: (i, 0)),
        compiler_params=_params(("parallel",)),
        name="inproj",
    )(x2, g, w)


def _t5_bucket(rel):
    nb = N_BUCKETS // 2
    max_exact = nb // 2
    ret = jnp.where(rel > 0, nb, 0)
    n = jnp.abs(rel)
    nf = jnp.maximum(n, 1).astype(F32)
    frac = jnp.log2(nf / max_exact) * ((nb - max_exact) / math.log2(MAX_DISTANCE / max_exact))
    large = max_exact + jnp.where(frac >= 0, jnp.floor(frac), jnp.ceil(frac)).astype(I32)
    large = jnp.minimum(large, nb - 1)
    return ret + jnp.where(n < max_exact, n, large)


def _bias_lookup(bucket, tbl_ref, col):
    out = jnp.zeros(bucket.shape, F32)
    for b in range(N_BUCKETS):
        out = jnp.where(bucket == b, tbl_ref[b, col], out)
    return out


def _toeplitz(row, tq, tk):
    width = tq + tk
    rb = jnp.broadcast_to(row, (tq, width))
    rolled = pltpu.roll(rb, width - tq + 1, 1, stride=1, stride_axis=0)
    return rolled[:, :tk]


def _da_kernel(tbl_ref, lq1_ref, lk1_ref, lq2_ref, lk2_ref, q_ref, k_ref, v_ref, g_ref,
               o_ref, qm_ref, m_ref, l_ref, acc_ref, bias_ref, *, t, seq):
    h = pl.program_id(1)
    qi = pl.program_id(2)
    nk = seq // t

    @pl.when(qi == 0)
    def _():
        for d in range(3):
            jj = lax.broadcasted_iota(I32, (1, 2 * t), 1)
            rel = jnp.clip((d - 1) * t + jj - (t - 1), -MAX_DISTANCE, MAX_DISTANCE)
            bias_ref[d] = _toeplitz(_bias_lookup(_t5_bucket(rel), tbl_ref, h) * LOG2E, t, t)

    c_left = tbl_ref[N_BUCKETS // 2 - 1, h] * LOG2E
    c_right = tbl_ref[N_BUCKETS - 1, h] * LOG2E

    q = q_ref[0]
    lane = lax.broadcasted_iota(I32, q.shape, 1)
    qm_ref[0] = jnp.where(lane < HEAD_DIM, q, jnp.zeros_like(q))
    qm_ref[1] = jnp.where(lane >= HEAD_DIM, q, jnp.zeros_like(q))
    m_ref[...] = jnp.full(m_ref.shape, -jnp.inf, F32)
    l_ref[...] = jnp.zeros(l_ref.shape, F32)
    acc_ref[...] = jnp.zeros(acc_ref.shape, F32)

    def step(tile, width, bias_idx):
        start = pl.multiple_of(tile * t, t)
        k = k_ref[0, pl.ds(start, width), :]
        v = v_ref[0, pl.ds(start, width), :]
        reps = width // LANES
        logits = [lax.dot_general(qm_ref[mp], k, NT_DIMS, preferred_element_type=F32) for mp in range(2)]
        for mp in range(2):
            s = logits[mp]
            if bias_idx is not None:
                s = s + bias_ref[bias_idx]
            m_prev = m_ref[mp]
            m_new = jnp.maximum(m_prev, jnp.max(s, axis=1, keepdims=True))
            alpha = jnp.exp2(m_prev - m_new)
            p = jnp.exp2(s - jnp.tile(m_new, (1, reps)))
            psum = p[:, :LANES]
            for r in range(1, reps):
                psum = psum + p[:, r * LANES:(r + 1) * LANES]
            l_ref[mp] = alpha * l_ref[mp] + psum
            acc_ref[mp] = alpha * acc_ref[mp] + jnp.dot(p.astype(BF16), v, preferred_element_type=F32)
            m_ref[mp] = m_new

    def far(first, count):
        def pair(j, c):
            step(first + 2 * j, 2 * t, None)
            return c

        def single(j, c):
            step(first + count - 1, t, None)
            return c

        lax.fori_loop(0, count // 2, pair, 0)
        lax.fori_loop(0, count % 2, single, 0)

    def near(ki, c):
        step(ki, t, ki - qi + 1)
        return c

    first_near = jnp.maximum(qi - 1, 0)
    first_right = jnp.minimum(qi + 2, nk)
    far(0, first_near)
    m_ref[...] += c_left
    lax.fori_loop(first_near, first_right, near, 0)
    m_ref[...] -= c_right
    far(first_right, nk - first_right)

    outs = [acc_ref[mp] / jnp.sum(l_ref[mp], axis=1, keepdims=True) for mp in range(2)]
    lam = (jnp.exp(jnp.sum(lq1_ref[...] * lk1_ref[...], axis=1, keepdims=True))
           - jnp.exp(jnp.sum(lq2_ref[...] * lk2_ref[...], axis=1, keepdims=True))
           + LAMBDA_INIT)
    o = outs[0] - lam * outs[1]
    o = _rms(o, g_ref[...], SUBLN_EPS) * (1.0 - LAMBDA_INIT)
    o_ref[0] = o.astype(o_ref.dtype)


def _diff_attention(proj3, tbl, lq1, lk1, lq2, lk2, subln_g, t):
    b, s, _ = proj3.shape
    assert t >= MAX_DISTANCE and s % t == 0
    vec = lambda n: pl.BlockSpec((1, n), lambda bi, hi, qi: (0, 0))
    kern = functools.partial(_da_kernel, t=t, seq=s)
    return pl.pallas_call(
        kern,
        out_shape=jax.ShapeDtypeStruct((b, s, DA_V), BF16),
        grid=(b, DA_HEADS, s // t),
        in_specs=[pl.BlockSpec(memory_space=pltpu.SMEM),
                  vec(HEAD_DIM), vec(HEAD_DIM), vec(HEAD_DIM), vec(HEAD_DIM),
                  pl.BlockSpec((1, t, LANES), lambda bi, hi, qi: (bi, qi, hi)),
                  pl.BlockSpec((1, s, LANES), lambda bi, hi, qi: (bi, 0, DA_HEADS + hi)),
                  pl.BlockSpec((1, s, LANES), lambda bi, hi, qi: (bi, 0, 2 * DA_HEADS + hi)),
                  vec(2 * HEAD_DIM)],
        out_specs=pl.BlockSpec((1, t, LANES), lambda bi, hi, qi: (bi, qi, hi)),
        scratch_shapes=[pltpu.VMEM((2, t, LANES), BF16), pltpu.VMEM((2, t, LANES), F32),
                        pltpu.VMEM((2, t, LANES), F32), pltpu.VMEM((2, t, 2 * HEAD_DIM), F32),
                        pltpu.VMEM((3, t, t), F32)],
        compiler_params=_params(("parallel", "parallel", "arbitrary")),
        name="diff_attention",
    )(tbl, lq1, lk1, lq2, lk2, proj3, proj3, proj3, subln_g)


def _wa_kernel(tbl_ref, sink_ref, q_ref, k_ref, v_ref, o_ref, bias_ref, *, tq, seq):
    q0 = pl.program_id(1) * tq
    win = tq + 2 * WINDOW
    ws = pl.multiple_of(jnp.clip(q0 - WINDOW, 0, seq - win), LANES)

    @pl.when(jnp.logical_and(pl.program_id(0) == 0, pl.program_id(1) == 0))
    def _():
        jj = lax.broadcasted_iota(I32, (1, win + tq), 1)
        for d in range(3):
            rel = jj - d * WINDOW - (tq - 1)
            valid = jnp.abs(rel) <= WINDOW
            bucket = _t5_bucket(jnp.clip(rel, -MAX_DISTANCE, MAX_DISTANCE))
            for hq in range(WA_Q_HEADS):
                row = jnp.where(valid, _bias_lookup(bucket, tbl_ref, DA_HEADS + hq) * LOG2E, -jnp.inf)
                bias_ref[d, hq] = _toeplitz(row, tq, win)

    variant = (q0 - ws) // WINDOW
    kw = k_ref[0, pl.ds(ws, win), :]
    vw = v_ref[0, pl.ds(ws, win), :]
    q = q_ref[0]
    low = lax.broadcasted_iota(I32, (tq, LANES), 1) < HEAD_DIM
    kdup = [jnp.concatenate([kw[:, g * HEAD_DIM:(g + 1) * HEAD_DIM]] * 2, axis=1) for g in range(2)]
    ones = jnp.ones((win, LANES), BF16)
    vext = [jnp.concatenate([vw[:, g * HEAD_DIM:(g + 1) * HEAD_DIM]] * 2 + [ones], axis=1) for g in range(2)]
    logits = []
    for hq in range(WA_Q_HEADS):
        qp = q[:, (hq // 2) * LANES:(hq // 2 + 1) * LANES]
        qm = jnp.where(low if hq % 2 == 0 else jnp.logical_not(low), qp, jnp.zeros_like(qp))
        logits.append(lax.dot_general(qm, kdup[hq // WA_REP], NT_DIMS, preferred_element_type=F32))
    outs = []
    for pair in range(WA_Q_HEADS // 2):
        g = (2 * pair) // WA_REP
        halves = []
        for half in range(2):
            hq = 2 * pair + half
            s = logits[hq] + bias_ref[variant, hq]
            sk = sink_ref[0, hq] * LOG2E
            m = jnp.maximum(jnp.full((tq, LANES), sk, F32), jnp.max(s, axis=1, keepdims=True))
            e = jnp.exp2(s - jnp.tile(m, (1, win // LANES))).astype(BF16)
            pv = jnp.dot(e, vext[g], preferred_element_type=F32)
            halves.append(pv[:, :LANES] / (pv[:, LANES:] + jnp.exp2(sk - m)))
        outs.append(jnp.where(low, halves[0], halves[1]))
    o_ref[0] = jnp.concatenate(outs, axis=1).astype(o_ref.dtype)


def _windowed_gqa(proj3, tbl, sink, tq):
    b, s, _ = proj3.shape
    kern = functools.partial(_wa_kernel, tq=tq, seq=s)
    kv_blk = (IN_COLS - 2 * LANES) // LANES
    return pl.pallas_call(
        kern,
        out_shape=jax.ShapeDtypeStruct((b, s, WA_Q), BF16),
        grid=(b, s // tq),
        in_specs=[pl.BlockSpec(memory_space=pltpu.SMEM),
                  pl.BlockSpec(memory_space=pltpu.SMEM),
                  pl.BlockSpec((1, tq, WA_Q), lambda bi, qi: (bi, qi, 3)),
                  pl.BlockSpec((1, s, LANES), lambda bi, qi: (bi, 0, kv_blk)),
                  pl.BlockSpec((1, s, LANES), lambda bi, qi: (bi, 0, kv_blk + 1))],
        out_specs=pl.BlockSpec((1, tq, WA_Q), lambda bi, qi: (bi, qi, 0)),
        scratch_shapes=[pltpu.VMEM((3, WA_Q_HEADS, tq, tq + 2 * WINDOW), F32)],
        compiler_params=_params(("arbitrary", "arbitrary")),
        name="windowed_gqa",
    )(tbl, sink, proj3, proj3, proj3)


def _oproj_kernel(a_ref, b_ref, x_ref, w1_ref, w2_ref, g_ref, wr_ref, x1_ref, h_ref, aff_ref):
    x1 = (x_ref[...] + jnp.dot(a_ref[...], w1_ref[...], preferred_element_type=F32)
          + jnp.dot(b_ref[...], w2_ref[...], preferred_element_type=F32))
    x1_ref[...] = x1
    h = _rms(x1, g_ref[...], RMS_EPS)
    h_ref[...] = h.astype(h_ref.dtype)
    logits = lax.dot_general(wr_ref[...], h, NT_DIMS, precision=lax.Precision.HIGHEST,
                             preferred_element_type=F32)
    e = jnp.exp(logits - jnp.max(logits, axis=0, keepdims=True))
    aff_ref[...] = e / jnp.sum(e, axis=0, keepdims=True)


def _oproj(a, b, x2, w1, w2, g, wr_t, tm):
    m = x2.shape[0]
    row = lambda n: pl.BlockSpec((tm, n), lambda i: (i, 0))
    full = lambda r, c: pl.BlockSpec((r, c), lambda i: (0, 0))
    return pl.pallas_call(
        _oproj_kernel,
        out_shape=(jax.ShapeDtypeStruct((m, D_MODEL), F32),
                   jax.ShapeDtypeStruct((m, D_MODEL), BF16),
                   jax.ShapeDtypeStruct((N_EXPERTS, m), F32)),
        grid=(m // tm,),
        in_specs=[row(DA_V), row(WA_Q), row(D_MODEL), full(DA_V, D_MODEL), full(WA_Q, D_MODEL),
                  full(1, D_MODEL), full(N_EXPERTS, D_MODEL)],
        out_specs=(row(D_MODEL), row(D_MODEL), pl.BlockSpec((N_EXPERTS, tm), lambda i: (0, i))),
        compiler_params=_params(("parallel",)),
        name="oproj_router",
    )(a, b, x2, w1, w2, g, wr_t)


def _cumsum_lanes(x):
    n = x.shape[1]
    lane = lax.broadcasted_iota(I32, x.shape, 1)
    shift = 1
    while shift < n:
        x = x + jnp.where(lane >= shift, pltpu.roll(x, shift, 1), 0)
        shift *= 2
    return x


def _select_kernel(aff_ref, slot_ref, excl_ref, *, cap):
    aff = aff_ref[...]

    def count(mask):
        return jnp.sum(mask.astype(F32), axis=1, keepdims=True)

    def body(i, prefix):
        cand = prefix | jnp.left_shift(jnp.int32(1), 30 - i)
        return jnp.where(count(aff >= pltpu.bitcast(cand, F32)) >= cap, cand, prefix)

    thr = lax.fori_loop(0, 31, body, jnp.zeros((aff.shape[0], 1), I32))
    above = aff >= pltpu.bitcast(thr + 1, F32)
    edge = jnp.logical_and(aff >= pltpu.bitcast(thr, F32), jnp.logical_not(above))
    need = cap - count(above)
    edge_i = edge.astype(I32)
    edge_excl = (_cumsum_lanes(edge_i) - edge_i).astype(F32)
    sel = above | (edge & (edge_excl < need))
    sel_i = sel.astype(I32)
    excl = _cumsum_lanes(sel_i) - sel_i
    slot_ref[...] = jnp.where(sel, excl, -1)
    excl_ref[...] = excl


def _select(aff_t, cap):
    shp = jax.ShapeDtypeStruct(aff_t.shape, I32)
    return pl.pallas_call(
        functools.partial(_select_kernel, cap=cap),
        out_shape=(shp, shp),
        compiler_params=_params(None),
        name="expert_select",
    )(aff_t)


def _gather_kernel(cnt_ref, x_ref, slot_ref, xg_hbm, res_ref, obuf, sem, carry_ref, xres_ref, xo, xsem,
                   *, nb, rw):
    b = pl.program_id(0)
    tb = x_ref.shape[0]
    cur = b % 2
    grp = BF16_SUBLANES
    groups = rw // grp

    def lo_of(e, blk):
        return cnt_ref[e * (nb + 1) + blk]

    def base_of(e, blk):
        return pl.multiple_of((lo_of(e, blk) // grp) * grp, grp)

    def out_copy(e, blk, slab):
        return pltpu.make_async_copy(obuf.at[slab, pl.ds(e * rw, rw), :],
                                     xg_hbm.at[e, pl.ds(base_of(e, blk), rw), :], sem.at[slab, e])

    def pack(rel):
        rows = lax.broadcasted_iota(I32, (rw, tb), 0)
        return jnp.where(rel == rows, 1.0, 0.0).astype(BF16)

    @pl.when(b == 0)
    def _():
        carry_ref[...] = jnp.zeros(carry_ref.shape, carry_ref.dtype)
        xo[...] = jnp.zeros(xo.shape, xo.dtype)
        for e in range(N_EXPERTS):
            cp = pltpu.make_async_copy(xo, xg_hbm.at[e, pl.ds(xg_hbm.shape[1] - rw, rw), :], xsem)
            cp.start()
            cp.wait()

    x = x_ref[...]
    onehot = jnp.concatenate([pack(slot_ref[e:e + 1, :] - base_of(e, b)) for e in range(N_EXPERTS)], axis=0)
    res_ref[...] = jnp.dot(onehot, x, preferred_element_type=F32)
    for e in range(N_EXPERTS):
        res_ref[pl.ds(e * rw, grp), :] += carry_ref[e]
        g_hi = (lo_of(e, b + 1) - base_of(e, b)) // grp
        row0 = pl.multiple_of(e * rw + jnp.minimum(g_hi, groups - 1) * grp, grp)
        carry_ref[e] = jnp.where(g_hi < groups, res_ref[pl.ds(row0, grp), :], 0.0)
    obuf[cur] = res_ref[...].astype(BF16)

    @pl.when(b > 0)
    def _():
        for e in range(N_EXPERTS):
            out_copy(e, b - 1, 1 - cur).wait()

    for e in range(N_EXPERTS):
        out_copy(e, b, cur).start()

    for e in range(N_EXPERTS):
        base = base_of(e, b)
        hi = lo_of(e, b + 1)
        n_extra = (jnp.maximum(hi - base - rw, 0) + rw - 1) // rw

        def extra(w, c, e=e, base=base, hi=hi):
            first = pl.multiple_of(base + (w + 1) * rw, grp)
            xres_ref[...] = jnp.dot(pack(slot_ref[e:e + 1, :] - first), x, preferred_element_type=F32)
            xo[...] = xres_ref[...].astype(BF16)
            cp = pltpu.make_async_copy(xo, xg_hbm.at[e, pl.ds(first, rw), :], xsem)
            cp.start()
            cp.wait()
            g_hi = (hi - first) // grp
            row0 = pl.multiple_of(jnp.clip(g_hi, 0, groups - 1) * grp, grp)
            carry_ref[e] = jnp.where(jnp.logical_and(g_hi >= 0, g_hi < groups),
                                     xres_ref[pl.ds(row0, grp), :], carry_ref[e])
            return c

        lax.fori_loop(0, n_extra, extra, 0)

    @pl.when(b == nb - 1)
    def _():
        for e in range(N_EXPERTS):
            out_copy(e, b, cur).wait()


def _gather(cnt, h2, slot, cap, tb):
    m = h2.shape[0]
    nb = m // tb
    rw = 5 * BF16_SUBLANES
    assert cap % BF16_SUBLANES == 0
    return pl.pallas_call(
        functools.partial(_gather_kernel, nb=nb, rw=rw),
        out_shape=jax.ShapeDtypeStruct((N_EXPERTS, cap + rw, D_MODEL), BF16),
        grid_spec=pltpu.PrefetchScalarGridSpec(
            num_scalar_prefetch=1,
            grid=(nb,),
            in_specs=[pl.BlockSpec((tb, D_MODEL), lambda b, c: (b, 0)),
                      pl.BlockSpec((N_EXPERTS, tb), lambda b, c: (0, b))],
            out_specs=pl.BlockSpec(memory_space=pl.ANY),
            scratch_shapes=[pltpu.VMEM((N_EXPERTS * rw, D_MODEL), F32),
                            pltpu.VMEM((2, N_EXPERTS * rw, D_MODEL), BF16),
                            pltpu.SemaphoreType.DMA((2, N_EXPERTS)),
                            pltpu.VMEM((N_EXPERTS, BF16_SUBLANES, D_MODEL), F32),
                            pltpu.VMEM((rw, D_MODEL), F32),
                            pltpu.VMEM((rw, D_MODEL), BF16),
                            pltpu.SemaphoreType.DMA(())]),
        compiler_params=_params(("arbitrary",)),
        name="moe_gather",
    )(cnt, h2, slot)


def _ffn_kernel(x_ref, wg_ref, wu_ref, wd_ref, y_ref):
    x = x_ref[0]
    g = jnp.dot(x, wg_ref[0], preferred_element_type=F32)
    u = jnp.dot(x, wu_ref[0], preferred_element_type=F32)
    h = (g * jax.nn.sigmoid(g) * u).astype(BF16)
    y_ref[0] = jnp.dot(h, wd_ref[0], preferred_element_type=F32).astype(y_ref.dtype)


def _ffn(xg, wg, wu, wd, cap, tm):
    wspec = lambda r, c: pl.BlockSpec((1, r, c), lambda e, i: (e, 0, 0))
    return pl.pallas_call(
        _ffn_kernel,
        out_shape=jax.ShapeDtypeStruct((N_EXPERTS, cap, D_MODEL), BF16),
        grid=(N_EXPERTS, cap // tm),
        in_specs=[pl.BlockSpec((1, tm, D_MODEL), lambda e, i: (e, i, 0)),
                  wspec(D_MODEL, EXPERT_FF), wspec(D_MODEL, EXPERT_FF), wspec(EXPERT_FF, D_MODEL)],
        out_specs=pl.BlockSpec((1, tm, D_MODEL), lambda e, i: (e, i, 0)),
        compiler_params=_params(("parallel", "arbitrary")),
        name="moe_ffn",
    )(xg, wg, wu, wd)


def _combine_kernel(cnt_ref, x1_ref, slot_ref, gate_ref, fn_ref, y_hbm, o_ref,
                    buf, sem, xbuf, xsem, acc_ref, *, nb, cap, rw):
    b = pl.program_id(0)
    tb = x1_ref.shape[0]
    xr = xbuf.shape[0]

    def window_start(e, blk):
        lo = cnt_ref[e * (nb + 1) + blk]
        return pl.multiple_of(jnp.minimum((lo // BF16_SUBLANES) * BF16_SUBLANES, cap - rw), BF16_SUBLANES)

    def window_copy(e, start, slab):
        return pltpu.make_async_copy(y_hbm.at[e, pl.ds(start, rw), :],
                                     buf.at[slab, pl.ds(e * rw, rw), :], sem.at[slab, e])

    def fetch(blk, slab):
        for e in range(N_EXPERTS):
            window_copy(e, window_start(e, blk), slab).start()

    cur = b % 2

    @pl.when(b == 0)
    def _():
        fetch(0, 0)

    @pl.when(b + 1 < nb)
    def _():
        fetch(b + 1, 1 - cur)

    starts = [window_start(e, b) for e in range(N_EXPERTS)]
    for e in range(N_EXPERTS):
        window_copy(e, starts[e], cur).wait()

    width = N_EXPERTS * rw
    ex = (lax.broadcasted_iota(I32, (N_EXPERTS, width), 0)
          == lax.broadcasted_iota(I32, (N_EXPERTS, width), 1) // rw).astype(BF16)
    lane16 = lax.broadcasted_iota(I32, (1, N_EXPERTS), 1)
    st_vec = jnp.zeros((1, N_EXPERTS), I32)
    for e in range(N_EXPERTS):
        st_vec = jnp.where(lane16 == e, starts[e], st_vec)
    rel = jnp.clip(slot_ref[...] - st_vec, -1, rw).astype(F32).astype(BF16)
    row = (lax.broadcasted_iota(I32, (tb, width), 1) % rw).astype(F32)
    hit = jnp.dot(rel, ex, preferred_element_type=F32) == row
    gate = gate_ref[...]
    g_hi = gate.astype(BF16)
    g_lo = (gate - g_hi.astype(F32)).astype(BF16)
    p_hi = jnp.where(hit, jnp.dot(g_hi, ex, preferred_element_type=F32), 0.0).astype(BF16)
    p_lo = jnp.where(hit, jnp.dot(g_lo, ex, preferred_element_type=F32), 0.0).astype(BF16)
    y = buf[cur]
    acc_ref[...] = (x1_ref[...] + jnp.dot(p_hi, y, preferred_element_type=F32)
                    + jnp.dot(p_lo, y, preferred_element_type=F32))

    xlane = lax.broadcasted_iota(I32, (tb, xr), 1)
    for e in range(N_EXPERTS):
        done = starts[e] + rw
        n_extra = (jnp.maximum(cnt_ref[e * (nb + 1) + b + 1] - done, 0) + xr - 1) // xr

        def extra(w, c, e=e, done=done):
            first = done + w * xr
            cst = pl.multiple_of(jnp.minimum(first, cap - xr), BF16_SUBLANES)
            cp = pltpu.make_async_copy(y_hbm.at[e, pl.ds(cst, xr), :], xbuf, xsem)
            cp.start()
            cp.wait()
            slot = slot_ref[:, e:e + 1]
            hit = jnp.logical_and(slot - cst == xlane, slot >= first)
            onehot = jnp.where(hit, 1.0, 0.0).astype(BF16)
            acc_ref[...] += gate_ref[:, e:e + 1] * jnp.dot(onehot, xbuf[...], preferred_element_type=F32)
            return c

        lax.fori_loop(0, n_extra, extra, 0)

    o_ref[...] = _rms(acc_ref[...], fn_ref[...], RMS_EPS)


def _combine(cnt, x1, slot_t, gate_t, fn, y, tb):
    m = x1.shape[0]
    nb = m // tb
    cap = y.shape[1]
    rw = LANES // 2
    assert cap % BF16_SUBLANES == 0 and cap >= LANES
    return pl.pallas_call(
        functools.partial(_combine_kernel, nb=nb, cap=cap, rw=rw),
        out_shape=jax.ShapeDtypeStruct((m, D_MODEL), F32),
        grid_spec=pltpu.PrefetchScalarGridSpec(
            num_scalar_prefetch=1,
            grid=(nb,),
            in_specs=[pl.BlockSpec((tb, D_MODEL), lambda b, c: (b, 0)),
                      pl.BlockSpec((tb, N_EXPERTS), lambda b, c: (b, 0)),
                      pl.BlockSpec((tb, N_EXPERTS), lambda b, c: (b, 0)),
                      pl.BlockSpec((1, D_MODEL), lambda b, c: (0, 0)),
                      pl.BlockSpec(memory_space=pl.ANY)],
            out_specs=pl.BlockSpec((tb, D_MODEL), lambda b, c: (b, 0)),
            scratch_shapes=[pltpu.VMEM((2, N_EXPERTS * rw, D_MODEL), BF16),
                            pltpu.SemaphoreType.DMA((2, N_EXPERTS)),
                            pltpu.VMEM((LANES, D_MODEL), BF16),
                            pltpu.SemaphoreType.DMA(()),
                            pltpu.VMEM((tb, D_MODEL), F32)]),
        compiler_params=_params(("arbitrary",)),
        name="moe_combine",
    )(cnt, x1, slot_t, gate_t, fn, y)


def _pick(n, prefs):
    for p in prefs:
        if n % p == 0:
            return p
    raise ValueError(f"no tile in {prefs} divides {n}")


def _trunk(x, tbl, sink, lam, subln_g, norm1, w_in, w_o1, w_o2, norm2, wr_t, wg, wu, wd, fn):
    bsz, s, d = x.shape
    m = bsz * s
    cap = CAPACITY_FACTOR * m // N_EXPERTS
    x2 = x.reshape(m, d)
    proj = _inproj(x2, norm1, w_in, _pick(m, (512, 256)))
    proj3 = proj.reshape(bsz, s, IN_COLS)
    out_a = _diff_attention(proj3, tbl, *lam, subln_g, _pick(s, (512, 256)))
    out_b = _windowed_gqa(proj3, tbl, sink, _pick(s, (256,)))
    x1, h2, aff_t = _oproj(out_a.reshape(m, DA_V), out_b.reshape(m, WA_Q), x2, w_o1, w_o2,
                           norm2, wr_t, _pick(m, (512, 256)))
    slot, excl = _select(aff_t, cap)
    tb = 256
    nb = m // tb
    cnt = jnp.concatenate([excl[:, ::tb], jnp.full((N_EXPERTS, 1), cap, I32)], axis=1).reshape(-1)
    xg = _gather(cnt, h2, slot, cap, tb)
    y = _ffn(xg, wg, wu, wd, cap, _pick(cap, (512, 256, 128)))
    out = _combine(cnt, x1, slot.T, aff_t.T, fn, y, tb)
    return out.reshape(bsz, s, d)


def kernel(x_prompt, x_sample, rel_bias, norm1, w_in, lam_q1, lam_k1, lam_q2, lam_k2, subln_g, sink,
           w_o, norm2, w_router, w_gate, w_up, w_down, final_norm):
    col = jnp.arange(IN_COLS)
    is_q = (col < DA_HEADS * 2 * HEAD_DIM) | ((col >= 3 * DA_V) & (col < 3 * DA_V + WA_Q))
    col_scale = jnp.where(is_q, HEAD_DIM ** -0.5 * LOG2E, 1.0).astype(F32)
    w_in_s = (w_in[0] * col_scale).astype(BF16)
    args = dict(
        tbl=rel_bias, sink=sink,
        lam=(lam_q1, lam_k1, lam_q2, lam_k2), subln_g=subln_g,
        norm1=norm1, w_in=w_in_s,
        w_o1=w_o[0, :DA_V].astype(BF16), w_o2=w_o[0, DA_V:].astype(BF16),
        norm2=norm2, wr_t=w_router[0].T,
        wg=w_gate[0].astype(BF16), wu=w_up[0].astype(BF16), wd=w_down[0].astype(BF16),
        fn=final_norm.reshape(1, D_MODEL))
    return (_trunk(x_prompt, **args), _trunk(x_sample, **args))
```

```python
import functools
import math

import jax
import jax.numpy as jnp
from jax import lax
from jax.experimental import pallas as pl
from jax.experimental.pallas import tpu as pltpu

F32 = jnp.float32
BF16 = jnp.bfloat16
I32 = jnp.int32

D_MODEL = 1024
HEAD_DIM = 64
DA_HEADS = 4
WA_Q_HEADS = 8
WA_REP = 4
WINDOW = 128
IN_COLS = 2304
DA_V = 512
WA_Q = 512
N_BUCKETS = 32
MAX_DISTANCE = 128
N_EXPERTS = 16
EXPERT_FF = 1024
CAPACITY_FACTOR = 2
RMS_EPS = 1e-6
SUBLN_EPS = 1e-5
LAMBDA_INIT = 0.8 - 0.6 * math.exp(-0.3 * 0)
LOG2E = math.log2(math.e)

LANES = 128
BF16_SUBLANES = 16
VMEM_LIMIT = 56 * 1024 * 1024
NT_DIMS = (((1,), (1,)), ((), ()))


def _params(sem, vmem=VMEM_LIMIT):
    return pltpu.CompilerParams(dimension_semantics=sem, vmem_limit_bytes=vmem)


def _rms(x, g, eps):
    return x * lax.rsqrt(jnp.mean(x * x, axis=-1, keepdims=True) + eps) * g


def _inproj_kernel(x_ref, g_ref, w_ref, o_ref):
    h = _rms(x_ref[...], g_ref[...], RMS_EPS).astype(BF16)
    o_ref[...] = jnp.dot(h, w_ref[...], preferred_element_type=F32).astype(o_ref.dtype)


def _inproj(x2, g, w, tm):
    m = x2.shape[0]
    return pl.pallas_call(
        _inproj_kernel,
        out_shape=jax.ShapeDtypeStruct((m, IN_COLS), BF16),
        grid=(m // tm,),
        in_specs=[pl.BlockSpec((tm, D_MODEL), lambda i: (i, 0)),
                  pl.BlockSpec((1, D_MODEL), lambda i: (0, 0)),
                  pl.BlockSpec((D_MODEL, IN_COLS), lambda i: (0, 0))],
        out_specs=pl.BlockSpec((tm, IN_COLS), lambda i: (i, 0)),
        compiler_params=_params(("parallel",)),
        name="inproj",
    )(x2, g, w)


def _t5_bucket(rel):
    nb = N_BUCKETS // 2
    max_exact = nb // 2
    ret = jnp.where(rel > 0, nb, 0)
    n = jnp.abs(rel)
    nf = jnp.maximum(n, 1).astype(F32)
    frac = jnp.log2(nf / max_exact) * ((nb - max_exact) / math.log2(MAX_DISTANCE / max_exact))
    large = max_exact + jnp.where(frac >= 0, jnp.floor(frac), jnp.ceil(frac)).astype(I32)
    large = jnp.minimum(large, nb - 1)
    return ret + jnp.where(n < max_exact, n, large)


def _bias_lookup(bucket, tbl_ref, col):
    out = jnp.zeros(bucket.shape, F32)
    for b in range(N_BUCKETS):
        out = jnp.where(bucket == b, tbl_ref[b, col], out)
    return out


def _toeplitz(row, tq, tk):
    width = tq + tk
    rb = jnp.broadcast_to(row, (tq, width))
    rolled = pltpu.roll(rb, width - tq + 1, 1, stride=1, stride_axis=0)
    return rolled[:, :tk]


def _da_kernel(tbl_ref, lq1_ref, lk1_ref, lq2_ref, lk2_ref, q_ref, k_ref, v_ref, g_ref,
               o_ref, qm_ref, m_ref, l_ref, acc_ref, bias_ref, *, t, seq):
    h = pl.program_id(1)
    qi = pl.program_id(2)
    nk = seq // t

    @pl.when(qi == 0)
    def _():
        for d in range(3):
            jj = lax.broadcasted_iota(I32, (1, 2 * t), 1)
            rel = jnp.clip((d - 1) * t + jj - (t - 1), -MAX_DISTANCE, MAX_DISTANCE)
            bias_ref[d] = _toeplitz(_bias_lookup(_t5_bucket(rel), tbl_ref, h) * LOG2E, t, t)

    c_left = tbl_ref[N_BUCKETS // 2 - 1, h] * LOG2E
    c_right = tbl_ref[N_BUCKETS - 1, h] * LOG2E

    q = q_ref[0]
    lane = lax.broadcasted_iota(I32, q.shape, 1)
    qm_ref[0] = jnp.where(lane < HEAD_DIM, q, jnp.zeros_like(q))
    qm_ref[1] = jnp.where(lane >= HEAD_DIM, q, jnp.zeros_like(q))
    m_ref[...] = jnp.full(m_ref.shape, -jnp.inf, F32)
    l_ref[...] = jnp.zeros(l_ref.shape, F32)
    acc_ref[...] = jnp.zeros(acc_ref.shape, F32)

    def step(tile, width, bias_idx):
        start = pl.multiple_of(tile * t, t)
        k = k_ref[0, pl.ds(start, width), :]
        v = v_ref[0, pl.ds(start, width), :]
        reps = width // LANES
        logits = [lax.dot_general(qm_ref[mp], k, NT_DIMS, preferred_element_type=F32) for mp in range(2)]
        for mp in range(2):
            s = logits[mp]
            if bias_idx is not None:
                s = s + jnp.concatenate([bias_ref[bias_idx + i] for i in range(width // t)], axis=1)
            m_prev = m_ref[mp]
            m_new = jnp.maximum(m_prev, jnp.max(s, axis=1, keepdims=True))
            alpha = jnp.exp2(m_prev - m_new)
            p = jnp.exp2(s - jnp.tile(m_new, (1, reps)))
            psum = p[:, :LANES]
            for r in range(1, reps):
                psum = psum + p[:, r * LANES:(r + 1) * LANES]
            l_ref[mp] = alpha * l_ref[mp] + psum
            acc_ref[mp] = alpha * acc_ref[mp] + jnp.dot(p.astype(BF16), v, preferred_element_type=F32)
            m_ref[mp] = m_new

    def far(first, count):
        def quad(j, c):
            step(first + 4 * j, 4 * t, None)
            return c

        def pair(j, c):
            step(first + (count // 4) * 4, 2 * t, None)
            return c

        def single(j, c):
            step(first + count - 1, t, None)
            return c

        lax.fori_loop(0, count // 4, quad, 0)
        lax.fori_loop(0, (count % 4) // 2, pair, 0)
        lax.fori_loop(0, count % 2, single, 0)

    def near_pair(j, c):
        step(first_near, 2 * t, first_near - qi + 1)
        return c

    def near_single(j, c):
        step(first_right - 1, t, first_right - qi)
        return c

    first_near = jnp.maximum(qi - 1, 0)
    first_right = jnp.minimum(qi + 2, nk)
    far(0, first_near)
    m_ref[...] += c_left
    n_near = first_right - first_near
    lax.fori_loop(0, n_near // 2, near_pair, 0)
    lax.fori_loop(0, n_near % 2, near_single, 0)
    m_ref[...] -= c_right
    far(first_right, nk - first_right)

    outs = [acc_ref[mp] / jnp.sum(l_ref[mp], axis=1, keepdims=True) for mp in range(2)]
    lam = (jnp.exp(jnp.sum(lq1_ref[...] * lk1_ref[...], axis=1, keepdims=True))
           - jnp.exp(jnp.sum(lq2_ref[...] * lk2_ref[...], axis=1, keepdims=True))
           + LAMBDA_INIT)
    o = outs[0] - lam * outs[1]
    o = _rms(o, g_ref[...], SUBLN_EPS) * (1.0 - LAMBDA_INIT)
    o_ref[0] = o.astype(o_ref.dtype)


def _diff_attention(proj3, tbl, lq1, lk1, lq2, lk2, subln_g, t):
    b, s, _ = proj3.shape
    assert t >= MAX_DISTANCE and s % t == 0
    vec = lambda n: pl.BlockSpec((1, n), lambda bi, hi, qi: (0, 0))
    kern = functools.partial(_da_kernel, t=t, seq=s)
    return pl.pallas_call(
        kern,
        out_shape=jax.ShapeDtypeStruct((b, s, DA_V), BF16),
        grid=(b, DA_HEADS, s // t),
        in_specs=[pl.BlockSpec(memory_space=pltpu.SMEM),
                  vec(HEAD_DIM), vec(HEAD_DIM), vec(HEAD_DIM), vec(HEAD_DIM),
                  pl.BlockSpec((1, t, LANES), lambda bi, hi, qi: (bi, qi, hi)),
                  pl.BlockSpec((1, s, LANES), lambda bi, hi, qi: (bi, 0, DA_HEADS + hi)),
                  pl.BlockSpec((1, s, LANES), lambda bi, hi, qi: (bi, 0, 2 * DA_HEADS + hi)),
                  vec(2 * HEAD_DIM)],
        out_specs=pl.BlockSpec((1, t, LANES), lambda bi, hi, qi: (bi, qi, hi)),
        scratch_shapes=[pltpu.VMEM((2, t, LANES), BF16), pltpu.VMEM((2, t, LANES), F32),
                        pltpu.VMEM((2, t, LANES), F32), pltpu.VMEM((2, t, 2 * HEAD_DIM), F32),
                        pltpu.VMEM((3, t, t), F32)],
        compiler_params=_params(("parallel", "parallel", "arbitrary")),
        name="diff_attention",
    )(tbl, lq1, lk1, lq2, lk2, proj3, proj3, proj3, subln_g)


def _wa_kernel(tbl_ref, sink_ref, q_ref, k_ref, v_ref, o_ref, bias_ref, *, tq, seq):
    q0 = pl.program_id(1) * tq
    win = tq + 2 * WINDOW
    ws = pl.multiple_of(jnp.clip(q0 - WINDOW, 0, seq - win), LANES)

    @pl.when(jnp.logical_and(pl.program_id(0) == 0, pl.program_id(1) == 0))
    def _():
        jj = lax.broadcasted_iota(I32, (1, win + tq), 1)
        for d in range(3):
            rel = jj - d * WINDOW - (tq - 1)
            valid = jnp.abs(rel) <= WINDOW
            bucket = _t5_bucket(jnp.clip(rel, -MAX_DISTANCE, MAX_DISTANCE))
            for hq in range(WA_Q_HEADS):
                row = jnp.where(valid, _bias_lookup(bucket, tbl_ref, DA_HEADS + hq) * LOG2E, -jnp.inf)
                bias_ref[d, hq] = _toeplitz(row, tq, win)

    variant = (q0 - ws) // WINDOW
    kw = k_ref[0, pl.ds(ws, win), :]
    vw = v_ref[0, pl.ds(ws, win), :]
    q = q_ref[0]
    low = lax.broadcasted_iota(I32, (tq, LANES), 1) < HEAD_DIM
    kdup = [jnp.concatenate([kw[:, g * HEAD_DIM:(g + 1) * HEAD_DIM]] * 2, axis=1) for g in range(2)]
    ones = jnp.ones((win, LANES), BF16)
    vext = [jnp.concatenate([vw[:, g * HEAD_DIM:(g + 1) * HEAD_DIM]] * 2 + [ones], axis=1) for g in range(2)]
    logits = []
    for hq in range(WA_Q_HEADS):
        qp = q[:, (hq // 2) * LANES:(hq // 2 + 1) * LANES]
        qm = jnp.where(low if hq % 2 == 0 else jnp.logical_not(low), qp, jnp.zeros_like(qp))
        logits.append(lax.dot_general(qm, kdup[hq // WA_REP], NT_DIMS, preferred_element_type=F32))
    outs = []
    for pair in range(WA_Q_HEADS // 2):
        g = (2 * pair) // WA_REP
        halves = []
        for half in range(2):
            hq = 2 * pair + half
            s = logits[hq] + bias_ref[variant, hq]
            sk = sink_ref[0, hq] * LOG2E
            m = jnp.maximum(jnp.full((tq, LANES), sk, F32), jnp.max(s, axis=1, keepdims=True))
            e = jnp.exp2(s - jnp.tile(m, (1, win // LANES))).astype(BF16)
            pv = jnp.dot(e, vext[g], preferred_element_type=F32)
            halves.append(pv[:, :LANES] / (pv[:, LANES:] + jnp.exp2(sk - m)))
        outs.append(jnp.where(low, halves[0], halves[1]))
    o_ref[0] = jnp.concatenate(outs, axis=1).astype(o_ref.dtype)


def _windowed_gqa(proj3, tbl, sink, tq):
    b, s, _ = proj3.shape
    kern = functools.partial(_wa_kernel, tq=tq, seq=s)
    kv_blk = (IN_COLS - 2 * LANES) // LANES
    return pl.pallas_call(
        kern,
        out_shape=jax.ShapeDtypeStruct((b, s, WA_Q), BF16),
        grid=(b, s // tq),
        in_specs=[pl.BlockSpec(memory_space=pltpu.SMEM),
                  pl.BlockSpec(memory_space=pltpu.SMEM),
                  pl.BlockSpec((1, tq, WA_Q), lambda bi, qi: (bi, qi, 3)),
                  pl.BlockSpec((1, s, LANES), lambda bi, qi: (bi, 0, kv_blk)),
                  pl.BlockSpec((1, s, LANES), lambda bi, qi: (bi, 0, kv_blk + 1))],
        out_specs=pl.BlockSpec((1, tq, WA_Q), lambda bi, qi: (bi, qi, 0)),
        scratch_shapes=[pltpu.VMEM((3, WA_Q_HEADS, tq, tq + 2 * WINDOW), F32)],
        compiler_params=_params(("arbitrary", "arbitrary")),
        name="windowed_gqa",
    )(tbl, sink, proj3, proj3, proj3)


def _oproj_kernel(a_ref, b_ref, x_ref, w1_ref, w2_ref, g_ref, wr_ref, x1_ref, h_ref, aff_ref):
    x1 = (x_ref[...] + jnp.dot(a_ref[...], w1_ref[...], preferred_element_type=F32)
          + jnp.dot(b_ref[...], w2_ref[...], preferred_element_type=F32))
    x1_ref[...] = x1
    h = _rms(x1, g_ref[...], RMS_EPS)
    h_hi = h.astype(BF16)
    h_ref[...] = h_hi
    h_lo = (h - h_hi.astype(F32)).astype(BF16)
    wr = wr_ref[...]
    w_hi = wr.astype(BF16)
    w_lo = (wr - w_hi.astype(F32)).astype(BF16)
    hh = jnp.dot(h_hi, jnp.concatenate([w_hi, w_lo], axis=1), preferred_element_type=F32)
    logits = (hh[:, :N_EXPERTS] + hh[:, N_EXPERTS:]
              + jnp.dot(h_lo, w_hi, preferred_element_type=F32))
    e = jnp.exp(logits - jnp.max(logits, axis=1, keepdims=True))
    aff_ref[...] = e / jnp.sum(e, axis=1, keepdims=True)


def _oproj(a, b, x2, w1, w2, g, wr, tm):
    m = x2.shape[0]
    row = lambda n: pl.BlockSpec((tm, n), lambda i: (i, 0))
    full = lambda r, c: pl.BlockSpec((r, c), lambda i: (0, 0))
    return pl.pallas_call(
        _oproj_kernel,
        out_shape=(jax.ShapeDtypeStruct((m, D_MODEL), F32),
                   jax.ShapeDtypeStruct((m, D_MODEL), BF16),
                   jax.ShapeDtypeStruct((m, N_EXPERTS), F32)),
        grid=(m // tm,),
        in_specs=[row(DA_V), row(WA_Q), row(D_MODEL), full(DA_V, D_MODEL), full(WA_Q, D_MODEL),
                  full(1, D_MODEL), full(D_MODEL, N_EXPERTS)],
        out_specs=(row(D_MODEL), row(D_MODEL), row(N_EXPERTS)),
        compiler_params=_params(("parallel",)),
        name="oproj_router",
    )(a, b, x2, w1, w2, g, wr)


def _cumsum_lanes(x):
    n = x.shape[1]
    lane = lax.broadcasted_iota(I32, x.shape, 1)
    shift = 1
    while shift < n:
        x = x + jnp.where(lane >= shift, pltpu.roll(x, shift, 1), 0)
        shift *= 2
    return x


def _select_kernel(aff_ref, slot_ref, excl_ref, *, cap):
    aff = aff_ref[...]

    def count(mask):
        return jnp.sum(mask.astype(F32), axis=1, keepdims=True)

    def body(i, prefix):
        cand = prefix | jnp.left_shift(jnp.int32(1), 30 - i)
        return jnp.where(count(aff >= pltpu.bitcast(cand, F32)) >= cap, cand, prefix)

    thr = lax.fori_loop(0, 31, body, jnp.zeros((aff.shape[0], 1), I32))
    above = aff >= pltpu.bitcast(thr + 1, F32)
    edge = jnp.logical_and(aff >= pltpu.bitcast(thr, F32), jnp.logical_not(above))
    need = cap - count(above)
    edge_i = edge.astype(I32)
    edge_excl = (_cumsum_lanes(edge_i) - edge_i).astype(F32)
    sel = above | (edge & (edge_excl < need))
    sel_i = sel.astype(I32)
    excl = _cumsum_lanes(sel_i) - sel_i
    slot_ref[...] = jnp.where(sel, excl, -1)
    excl_ref[...] = excl


def _select(aff_t, cap):
    shp = jax.ShapeDtypeStruct(aff_t.shape, I32)
    return pl.pallas_call(
        functools.partial(_select_kernel, cap=cap),
        out_shape=(shp, shp),
        compiler_params=_params(None),
        name="expert_select",
    )(aff_t)


def _gather_kernel(cnt_ref, x_ref, slot_ref, xg_hbm, res_ref, obuf, sem, carry_ref, xres_ref, xo, xsem,
                   *, nb, rw):
    b = pl.program_id(0)
    tb = x_ref.shape[0]
    cur = b % 2
    grp = BF16_SUBLANES
    groups = rw // grp

    def lo_of(e, blk):
        return cnt_ref[e * (nb + 1) + blk]

    def base_of(e, blk):
        return pl.multiple_of((lo_of(e, blk) // grp) * grp, grp)

    def out_copy(e, blk, slab):
        return pltpu.make_async_copy(obuf.at[slab, pl.ds(e * rw, rw), :],
                                     xg_hbm.at[e, pl.ds(base_of(e, blk), rw), :], sem.at[slab, e])

    def pack(rel):
        rows = lax.broadcasted_iota(I32, (rw, tb), 0)
        return jnp.where(rel == rows, 1.0, 0.0).astype(BF16)

    @pl.when(b == 0)
    def _():
        carry_ref[...] = jnp.zeros(carry_ref.shape, carry_ref.dtype)
        xo[...] = jnp.zeros(xo.shape, xo.dtype)
        for e in range(N_EXPERTS):
            cp = pltpu.make_async_copy(xo, xg_hbm.at[e, pl.ds(xg_hbm.shape[1] - rw, rw), :], xsem)
            cp.start()
            cp.wait()

    x = x_ref[...]
    onehot = jnp.concatenate([pack(slot_ref[e:e + 1, :] - base_of(e, b)) for e in range(N_EXPERTS)], axis=0)
    res_ref[...] = jnp.dot(onehot, x, preferred_element_type=F32)
    for e in range(N_EXPERTS):
        res_ref[pl.ds(e * rw, grp), :] += carry_ref[e]
        g_hi = (lo_of(e, b + 1) - base_of(e, b)) // grp
        row0 = pl.multiple_of(e * rw + jnp.minimum(g_hi, groups - 1) * grp, grp)
        carry_ref[e] = jnp.where(g_hi < groups, res_ref[pl.ds(row0, grp), :], 0.0)
    obuf[cur] = res_ref[...].astype(BF16)

    @pl.when(b > 0)
    def _():
        for e in range(N_EXPERTS):
            out_copy(e, b - 1, 1 - cur).wait()

    for e in range(N_EXPERTS):
        out_copy(e, b, cur).start()

    for e in range(N_EXPERTS):
        base = base_of(e, b)
        hi = lo_of(e, b + 1)
        n_extra = (jnp.maximum(hi - base - rw, 0) + rw - 1) // rw

        def extra(w, c, e=e, base=base, hi=hi):
            first = pl.multiple_of(base + (w + 1) * rw, grp)
            xres_ref[...] = jnp.dot(pack(slot_ref[e:e + 1, :] - first), x, preferred_element_type=F32)
            xo[...] = xres_ref[...].astype(BF16)
            cp = pltpu.make_async_copy(xo, xg_hbm.at[e, pl.ds(first, rw), :], xsem)
            cp.start()
            cp.wait()
            g_hi = (hi - first) // grp
            row0 = pl.multiple_of(jnp.clip(g_hi, 0, groups - 1) * grp, grp)
            carry_ref[e] = jnp.where(jnp.logical_and(g_hi >= 0, g_hi < groups),
                                     xres_ref[pl.ds(row0, grp), :], carry_ref[e])
            return c

        lax.fori_loop(0, n_extra, extra, 0)

    @pl.when(b == nb - 1)
    def _():
        for e in range(N_EXPERTS):
            out_copy(e, b, cur).wait()


def _gather(cnt, h2, slot, cap, tb):
    m = h2.shape[0]
    nb = m // tb
    rw = 5 * BF16_SUBLANES
    assert cap % BF16_SUBLANES == 0
    return pl.pallas_call(
        functools.partial(_gather_kernel, nb=nb, rw=rw),
        out_shape=jax.ShapeDtypeStruct((N_EXPERTS, cap + rw, D_MODEL), BF16),
        grid_spec=pltpu.PrefetchScalarGridSpec(
            num_scalar_prefetch=1,
            grid=(nb,),
            in_specs=[pl.BlockSpec((tb, D_MODEL), lambda b, c: (b, 0)),
                      pl.BlockSpec((N_EXPERTS, tb), lambda b, c: (0, b))],
            out_specs=pl.BlockSpec(memory_space=pl.ANY),
            scratch_shapes=[pltpu.VMEM((N_EXPERTS * rw, D_MODEL), F32),
                            pltpu.VMEM((2, N_EXPERTS * rw, D_MODEL), BF16),
                            pltpu.SemaphoreType.DMA((2, N_EXPERTS)),
                            pltpu.VMEM((N_EXPERTS, BF16_SUBLANES, D_MODEL), F32),
                            pltpu.VMEM((rw, D_MODEL), F32),
                            pltpu.VMEM((rw, D_MODEL), BF16),
                            pltpu.SemaphoreType.DMA(())]),
        compiler_params=_params(("arbitrary",)),
        name="moe_gather",
    )(cnt, h2, slot)


def _ffn_kernel(x_ref, wg_ref, wu_ref, wd_ref, y_ref):
    x = x_ref[0]
    g = jnp.dot(x, wg_ref[0], preferred_element_type=F32)
    u = jnp.dot(x, wu_ref[0], preferred_element_type=F32)
    h = (g * jax.nn.sigmoid(g) * u).astype(BF16)
    y_ref[0] = jnp.dot(h, wd_ref[0], preferred_element_type=F32).astype(y_ref.dtype)


def _ffn(xg, wg, wu, wd, cap, tm):
    wspec = lambda r, c: pl.BlockSpec((1, r, c), lambda e, i: (e, 0, 0))
    return pl.pallas_call(
        _ffn_kernel,
        out_shape=jax.ShapeDtypeStruct((N_EXPERTS, cap, D_MODEL), BF16),
        grid=(N_EXPERTS, cap // tm),
        in_specs=[pl.BlockSpec((1, tm, D_MODEL), lambda e, i: (e, i, 0)),
                  wspec(D_MODEL, EXPERT_FF), wspec(D_MODEL, EXPERT_FF), wspec(EXPERT_FF, D_MODEL)],
        out_specs=pl.BlockSpec((1, tm, D_MODEL), lambda e, i: (e, i, 0)),
        compiler_params=_params(("parallel", "arbitrary")),
        name="moe_ffn",
    )(xg, wg, wu, wd)


def _combine_kernel(cnt_ref, x1_ref, slot_ref, gate_ref, fn_ref, y_hbm, o_ref,
                    buf, sem, xbuf, xsem, acc_ref, *, nb, cap, rw):
    b = pl.program_id(0)
    tb = x1_ref.shape[0]
    xr = xbuf.shape[0]

    def window_start(e, blk):
        lo = cnt_ref[e * (nb + 1) + blk]
        return pl.multiple_of(jnp.minimum((lo // BF16_SUBLANES) * BF16_SUBLANES, cap - rw), BF16_SUBLANES)

    def window_copy(e, start, slab):
        return pltpu.make_async_copy(y_hbm.at[e, pl.ds(start, rw), :],
                                     buf.at[slab, pl.ds(e * rw, rw), :], sem.at[slab, e])

    def fetch(blk, slab):
        for e in range(N_EXPERTS):
            window_copy(e, window_start(e, blk), slab).start()

    cur = b % 2

    @pl.when(b == 0)
    def _():
        fetch(0, 0)

    @pl.when(b + 1 < nb)
    def _():
        fetch(b + 1, 1 - cur)

    starts = [window_start(e, b) for e in range(N_EXPERTS)]
    for e in range(N_EXPERTS):
        window_copy(e, starts[e], cur).wait()

    width = N_EXPERTS * rw
    ex = (lax.broadcasted_iota(I32, (N_EXPERTS, width), 0)
          == lax.broadcasted_iota(I32, (N_EXPERTS, width), 1) // rw).astype(BF16)
    lane16 = lax.broadcasted_iota(I32, (1, N_EXPERTS), 1)
    st_vec = jnp.zeros((1, N_EXPERTS), I32)
    for e in range(N_EXPERTS):
        st_vec = jnp.where(lane16 == e, starts[e], st_vec)
    rel = jnp.clip(slot_ref[...] - st_vec, -1, rw).astype(F32).astype(BF16)
    row = (lax.broadcasted_iota(I32, (tb, width), 1) % rw).astype(F32)
    hit = jnp.dot(rel, ex, preferred_element_type=F32) == row
    gate = gate_ref[...]
    g_hi = gate.astype(BF16)
    g_lo = (gate - g_hi.astype(F32)).astype(BF16)
    p_hi = jnp.where(hit, jnp.dot(g_hi, ex, preferred_element_type=F32), 0.0).astype(BF16)
    p_lo = jnp.where(hit, jnp.dot(g_lo, ex, preferred_element_type=F32), 0.0).astype(BF16)
    y = buf[cur]
    acc_ref[...] = (x1_ref[...] + jnp.dot(p_hi, y, preferred_element_type=F32)
                    + jnp.dot(p_lo, y, preferred_element_type=F32))

    xlane = lax.broadcasted_iota(I32, (tb, xr), 1)
    for e in range(N_EXPERTS):
        done = starts[e] + rw
        n_extra = (jnp.maximum(cnt_ref[e * (nb + 1) + b + 1] - done, 0) + xr - 1) // xr

        def extra(w, c, e=e, done=done):
            first = done + w * xr
            cst = pl.multiple_of(jnp.minimum(first, cap - xr), BF16_SUBLANES)
            cp = pltpu.make_async_copy(y_hbm.at[e, pl.ds(cst, xr), :], xbuf, xsem)
            cp.start()
            cp.wait()
            slot = slot_ref[:, e:e + 1]
            hit = jnp.logical_and(slot - cst == xlane, slot >= first)
            onehot = jnp.where(hit, 1.0, 0.0).astype(BF16)
            acc_ref[...] += gate_ref[:, e:e + 1] * jnp.dot(onehot, xbuf[...], preferred_element_type=F32)
            return c

        lax.fori_loop(0, n_extra, extra, 0)

    o_ref[...] = _rms(acc_ref[...], fn_ref[...], RMS_EPS)


def _combine(cnt, x1, slot_t, gate_t, fn, y, tb):
    m = x1.shape[0]
    nb = m // tb
    cap = y.shape[1]
    rw = LANES // 2
    assert cap % BF16_SUBLANES == 0 and cap >= LANES
    return pl.pallas_call(
        functools.partial(_combine_kernel, nb=nb, cap=cap, rw=rw),
        out_shape=jax.ShapeDtypeStruct((m, D_MODEL), F32),
        grid_spec=pltpu.PrefetchScalarGridSpec(
            num_scalar_prefetch=1,
            grid=(nb,),
            in_specs=[pl.BlockSpec((tb, D_MODEL), lambda b, c: (b, 0)),
                      pl.BlockSpec((tb, N_EXPERTS), lambda b, c: (b, 0)),
                      pl.BlockSpec((tb, N_EXPERTS), lambda b, c: (b, 0)),
                      pl.BlockSpec((1, D_MODEL), lambda b, c: (0, 0)),
                      pl.BlockSpec(memory_space=pl.ANY)],
            out_specs=pl.BlockSpec((tb, D_MODEL), lambda b, c: (b, 0)),
            scratch_shapes=[pltpu.VMEM((2, N_EXPERTS * rw, D_MODEL), BF16),
                            pltpu.SemaphoreType.DMA((2, N_EXPERTS)),
                            pltpu.VMEM((LANES, D_MODEL), BF16),
                            pltpu.SemaphoreType.DMA(()),
                            pltpu.VMEM((tb, D_MODEL), F32)]),
        compiler_params=_params(("arbitrary",)),
        name="moe_combine",
    )(cnt, x1, slot_t, gate_t, fn, y)


def _pick(n, prefs):
    for p in prefs:
        if n % p == 0:
            return p
    raise ValueError(f"no tile in {prefs} divides {n}")


def _trunk(x, tbl, sink, lam, subln_g, norm1, w_in, w_o1, w_o2, norm2, wr, wg, wu, wd, fn):
    bsz, s, d = x.shape
    m = bsz * s
    cap = CAPACITY_FACTOR * m // N_EXPERTS
    x2 = x.reshape(m, d)
    proj = _inproj(x2, norm1, w_in, _pick(m, (512, 256)))
    proj3 = proj.reshape(bsz, s, IN_COLS)
    out_a = _diff_attention(proj3, tbl, *lam, subln_g, _pick(s, (512, 256)))
    out_b = _windowed_gqa(proj3, tbl, sink, _pick(s, (256,)))
    x1, h2, aff = _oproj(out_a.reshape(m, DA_V), out_b.reshape(m, WA_Q), x2, w_o1, w_o2,
                         norm2, wr, _pick(m, (512, 256)))
    slot, excl = _select(aff.T, cap)
    tb = 256
    nb = m // tb
    cnt = jnp.concatenate([excl[:, ::tb], jnp.full((N_EXPERTS, 1), cap, I32)], axis=1).reshape(-1)
    xg = _gather(cnt, h2, slot, cap, tb)
    y = _ffn(xg, wg, wu, wd, cap, _pick(cap, (512, 256, 128)))
    out = _combine(cnt, x1, slot.T, aff, fn, y, tb)
    return out.reshape(bsz, s, d)


def kernel(x_prompt, x_sample, rel_bias, norm1, w_in, lam_q1, lam_k1, lam_q2, lam_k2, subln_g, sink,
           w_o, norm2, w_router, w_gate, w_up, w_down, final_norm):
    col = jnp.arange(IN_COLS)
    is_q = (col < DA_HEADS * 2 * HEAD_DIM) | ((col >= 3 * DA_V) & (col < 3 * DA_V + WA_Q))
    col_scale = jnp.where(is_q, HEAD_DIM ** -0.5 * LOG2E, 1.0).astype(F32)
    w_in_s = (w_in[0] * col_scale).astype(BF16)
    args = dict(
        tbl=rel_bias, sink=sink,
        lam=(lam_q1, lam_k1, lam_q2, lam_k2), subln_g=subln_g,
        norm1=norm1, w_in=w_in_s,
        w_o1=w_o[0, :DA_V].astype(BF16), w_o2=w_o[0, DA_V:].astype(BF16),
        norm2=norm2, wr=w_router[0],
        wg=w_gate[0].astype(BF16), wu=w_up[0].astype(BF16), wd=w_down[0].astype(BF16),
        fn=final_norm.reshape(1, D_MODEL))
    return (_trunk(x_prompt, **args), _trunk(x_sample, **args))
```

```python
import functools
import math

import jax
import jax.numpy as jnp
from jax import lax
from jax.experimental import pallas as pl
from jax.experimental.pallas import tpu as pltpu

F32 = jnp.float32
BF16 = jnp.bfloat16
I32 = jnp.int32

D_MODEL = 1024
HEAD_DIM = 64
DA_HEADS = 4
WA_Q_HEADS = 8
WA_REP = 4
WINDOW = 128
IN_COLS = 2304
DA_V = 512
WA_Q = 512
N_BUCKETS = 32
MAX_DISTANCE = 128
N_EXPERTS = 16
EXPERT_FF = 1024
CAPACITY_FACTOR = 2
RMS_EPS = 1e-6
SUBLN_EPS = 1e-5
LAMBDA_INIT = 0.8 - 0.6 * math.exp(-0.3 * 0)
LOG2E = math.log2(math.e)

LANES = 128
BF16_SUBLANES = 16
VMEM_LIMIT = 56 * 1024 * 1024
NT_DIMS = (((1,), (1,)), ((), ()))


def _params(sem, vmem=VMEM_LIMIT):
    return pltpu.CompilerParams(dimension_semantics=sem, vmem_limit_bytes=vmem)


def _rms(x, g, eps):
    return x * lax.rsqrt(jnp.mean(x * x, axis=-1, keepdims=True) + eps) * g


def _inproj_kernel(x_ref, g_ref, w_ref, o_ref):
    h = _rms(x_ref[...], g_ref[...], RMS_EPS).astype(BF16)
    o_ref[...] = jnp.dot(h, w_ref[...], preferred_element_type=F32).astype(o_ref.dtype)


def _inproj(x2, g, w, tm):
    m = x2.shape[0]
    return pl.pallas_call(
        _inproj_kernel,
        out_shape=jax.ShapeDtypeStruct((m, IN_COLS), BF16),
        grid=(m // tm,),
        in_specs=[pl.BlockSpec((tm, D_MODEL), lambda i: (i, 0)),
                  pl.BlockSpec((1, D_MODEL), lambda i: (0, 0)),
                  pl.BlockSpec((D_MODEL, IN_COLS), lambda i: (0, 0))],
        out_specs=pl.BlockSpec((tm, IN_COLS), lambda i: (i, 0)),
        compiler_params=_params(("parallel",)),
        name="inproj",
    )(x2, g, w)


def _t5_bucket(rel):
    nb = N_BUCKETS // 2
    max_exact = nb // 2
    ret = jnp.where(rel > 0, nb, 0)
    n = jnp.abs(rel)
    nf = jnp.maximum(n, 1).astype(F32)
    frac = jnp.log2(nf / max_exact) * ((nb - max_exact) / math.log2(MAX_DISTANCE / max_exact))
    large = max_exact + jnp.where(frac >= 0, jnp.floor(frac), jnp.ceil(frac)).astype(I32)
    large = jnp.minimum(large, nb - 1)
    return ret + jnp.where(n < max_exact, n, large)


def _bias_lookup(bucket, tbl_ref, col):
    out = jnp.zeros(bucket.shape, F32)
    for b in range(N_BUCKETS):
        out = jnp.where(bucket == b, tbl_ref[b, col], out)
    return out


def _toeplitz(row, tq, tk):
    width = tq + tk
    rb = jnp.broadcast_to(row, (tq, width))
    rolled = pltpu.roll(rb, width - tq + 1, 1, stride=1, stride_axis=0)
    return rolled[:, :tk]


def _da_kernel(tbl_ref, lq1_ref, lk1_ref, lq2_ref, lk2_ref, q_ref, k_ref, v_ref, g_ref,
               o_ref, qm_ref, m_ref, l_ref, acc_ref, bias_ref, *, t, seq):
    h = pl.program_id(1)
    qi = pl.program_id(2)
    nk = seq // t

    @pl.when(qi == 0)
    def _():
        for d in range(3):
            jj = lax.broadcasted_iota(I32, (1, 2 * t), 1)
            rel = jnp.clip((d - 1) * t + jj - (t - 1), -MAX_DISTANCE, MAX_DISTANCE)
            bias_ref[d] = _toeplitz(_bias_lookup(_t5_bucket(rel), tbl_ref, h) * LOG2E, t, t)

    c_left = tbl_ref[N_BUCKETS // 2 - 1, h] * LOG2E
    c_right = tbl_ref[N_BUCKETS - 1, h] * LOG2E

    q = q_ref[0]
    lane = lax.broadcasted_iota(I32, q.shape, 1)
    qm_ref[0] = jnp.where(lane < HEAD_DIM, q, jnp.zeros_like(q))
    qm_ref[1] = jnp.where(lane >= HEAD_DIM, q, jnp.zeros_like(q))
    m_ref[...] = jnp.full(m_ref.shape, -jnp.inf, F32)
    l_ref[...] = jnp.zeros(l_ref.shape, F32)
    acc_ref[...] = jnp.zeros(acc_ref.shape, F32)

    def step(tile, width, bias_idx):
        start = pl.multiple_of(tile * t, t)
        k = k_ref[0, pl.ds(start, width), :]
        v = v_ref[0, pl.ds(start, width), :]
        reps = width // LANES
        logits = [lax.dot_general(qm_ref[mp], k, NT_DIMS, preferred_element_type=F32) for mp in range(2)]
        for mp in range(2):
            s = logits[mp]
            if bias_idx is not None:
                s = s + jnp.concatenate([bias_ref[bias_idx + i] for i in range(width // t)], axis=1)
            m_prev = m_ref[mp]
            m_new = jnp.maximum(m_prev, jnp.max(s, axis=1, keepdims=True))
            alpha = jnp.exp2(m_prev - m_new)
            p = jnp.exp2(s - jnp.tile(m_new, (1, reps)))
            psum = p[:, :LANES]
            for r in range(1, reps):
                psum = psum + p[:, r * LANES:(r + 1) * LANES]
            l_ref[mp] = alpha * l_ref[mp] + psum
            acc_ref[mp] = alpha * acc_ref[mp] + jnp.dot(p.astype(BF16), v, preferred_element_type=F32)
            m_ref[mp] = m_new

    def far(first, count):
        def quad(j, c):
            step(first + 4 * j, 4 * t, None)
            return c

        def pair(j, c):
            step(first + (count // 4) * 4, 2 * t, None)
            return c

        def single(j, c):
            step(first + count - 1, t, None)
            return c

        lax.fori_loop(0, count // 4, quad, 0)
        lax.fori_loop(0, (count % 4) // 2, pair, 0)
        lax.fori_loop(0, count % 2, single, 0)

    def near_pair(j, c):
        step(first_near, 2 * t, first_near - qi + 1)
        return c

    def near_single(j, c):
        step(first_right - 1, t, first_right - qi)
        return c

    first_near = jnp.maximum(qi - 1, 0)
    first_right = jnp.minimum(qi + 2, nk)
    far(0, first_near)
    m_ref[...] += c_left
    n_near = first_right - first_near
    lax.fori_loop(0, n_near // 2, near_pair, 0)
    lax.fori_loop(0, n_near % 2, near_single, 0)
    m_ref[...] -= c_right
    far(first_right, nk - first_right)

    outs = [acc_ref[mp] / jnp.sum(l_ref[mp], axis=1, keepdims=True) for mp in range(2)]
    lam = (jnp.exp(jnp.sum(lq1_ref[...] * lk1_ref[...], axis=1, keepdims=True))
           - jnp.exp(jnp.sum(lq2_ref[...] * lk2_ref[...], axis=1, keepdims=True))
           + LAMBDA_INIT)
    o = outs[0] - lam * outs[1]
    o = _rms(o, g_ref[...], SUBLN_EPS) * (1.0 - LAMBDA_INIT)
    o_ref[0] = o.astype(o_ref.dtype)


def _diff_attention(proj3, tbl, lq1, lk1, lq2, lk2, subln_g, t):
    b, s, _ = proj3.shape
    assert t >= MAX_DISTANCE and s % t == 0
    vec = lambda n: pl.BlockSpec((1, n), lambda bi, hi, qi: (0, 0))
    kern = functools.partial(_da_kernel, t=t, seq=s)
    return pl.pallas_call(
        kern,
        out_shape=jax.ShapeDtypeStruct((b, s, DA_V), BF16),
        grid=(b, DA_HEADS, s // t),
        in_specs=[pl.BlockSpec(memory_space=pltpu.SMEM),
                  vec(HEAD_DIM), vec(HEAD_DIM), vec(HEAD_DIM), vec(HEAD_DIM),
                  pl.BlockSpec((1, t, LANES), lambda bi, hi, qi: (bi, qi, hi)),
                  pl.BlockSpec((1, s, LANES), lambda bi, hi, qi: (bi, 0, DA_HEADS + hi)),
                  pl.BlockSpec((1, s, LANES), lambda bi, hi, qi: (bi, 0, 2 * DA_HEADS + hi)),
                  vec(2 * HEAD_DIM)],
        out_specs=pl.BlockSpec((1, t, LANES), lambda bi, hi, qi: (bi, qi, hi)),
        scratch_shapes=[pltpu.VMEM((2, t, LANES), BF16), pltpu.VMEM((2, t, LANES), F32),
                        pltpu.VMEM((2, t, LANES), F32), pltpu.VMEM((2, t, 2 * HEAD_DIM), F32),
                        pltpu.VMEM((3, t, t), F32)],
        compiler_params=_params(("parallel", "parallel", "arbitrary")),
        name="diff_attention",
    )(tbl, lq1, lk1, lq2, lk2, proj3, proj3, proj3, subln_g)


def _wa_kernel(tbl_ref, sink_ref, q_ref, k_ref, v_ref, o_ref, bias_ref, *, tq, seq):
    q0 = pl.program_id(1) * tq
    win = tq + 2 * WINDOW
    ws = pl.multiple_of(jnp.clip(q0 - WINDOW, 0, seq - win), LANES)

    @pl.when(jnp.logical_and(pl.program_id(0) == 0, pl.program_id(1) == 0))
    def _():
        jj = lax.broadcasted_iota(I32, (1, win + tq), 1)
        for d in range(3):
            rel = jj - d * WINDOW - (tq - 1)
            valid = jnp.abs(rel) <= WINDOW
            bucket = _t5_bucket(jnp.clip(rel, -MAX_DISTANCE, MAX_DISTANCE))
            for hq in range(WA_Q_HEADS):
                row = jnp.where(valid, _bias_lookup(bucket, tbl_ref, DA_HEADS + hq) * LOG2E, -jnp.inf)
                bias_ref[d, hq] = _toeplitz(row, tq, win)

    variant = (q0 - ws) // WINDOW
    kw = k_ref[0, pl.ds(ws, win), :]
    vw = v_ref[0, pl.ds(ws, win), :]
    q = q_ref[0]
    low = lax.broadcasted_iota(I32, (tq, LANES), 1) < HEAD_DIM
    kdup = [jnp.concatenate([kw[:, g * HEAD_DIM:(g + 1) * HEAD_DIM]] * 2, axis=1) for g in range(2)]
    ones = jnp.ones((win, LANES), BF16)
    vext = [jnp.concatenate([vw[:, g * HEAD_DIM:(g + 1) * HEAD_DIM]] * 2 + [ones], axis=1) for g in range(2)]
    logits = []
    for hq in range(WA_Q_HEADS):
        qp = q[:, (hq // 2) * LANES:(hq // 2 + 1) * LANES]
        qm = jnp.where(low if hq % 2 == 0 else jnp.logical_not(low), qp, jnp.zeros_like(qp))
        logits.append(lax.dot_general(qm, kdup[hq // WA_REP], NT_DIMS, preferred_element_type=F32))
    outs = []
    for pair in range(WA_Q_HEADS // 2):
        g = (2 * pair) // WA_REP
        halves = []
        for half in range(2):
            hq = 2 * pair + half
            s = logits[hq] + bias_ref[variant, hq]
            sk = sink_ref[0, hq] * LOG2E
            m = jnp.maximum(jnp.full((tq, LANES), sk, F32), jnp.max(s, axis=1, keepdims=True))
            e = jnp.exp2(s - jnp.tile(m, (1, win // LANES))).astype(BF16)
            pv = jnp.dot(e, vext[g], preferred_element_type=F32)
            halves.append(pv[:, :LANES] / (pv[:, LANES:] + jnp.exp2(sk - m)))
        outs.append(jnp.where(low, halves[0], halves[1]))
    o_ref[0] = jnp.concatenate(outs, axis=1).astype(o_ref.dtype)


def _windowed_gqa(proj3, tbl, sink, tq):
    b, s, _ = proj3.shape
    kern = functools.partial(_wa_kernel, tq=tq, seq=s)
    kv_blk = (IN_COLS - 2 * LANES) // LANES
    return pl.pallas_call(
        kern,
        out_shape=jax.ShapeDtypeStruct((b, s, WA_Q), BF16),
        grid=(b, s // tq),
        in_specs=[pl.BlockSpec(memory_space=pltpu.SMEM),
                  pl.BlockSpec(memory_space=pltpu.SMEM),
                  pl.BlockSpec((1, tq, WA_Q), lambda bi, qi: (bi, qi, 3)),
                  pl.BlockSpec((1, s, LANES), lambda bi, qi: (bi, 0, kv_blk)),
                  pl.BlockSpec((1, s, LANES), lambda bi, qi: (bi, 0, kv_blk + 1))],
        out_specs=pl.BlockSpec((1, tq, WA_Q), lambda bi, qi: (bi, qi, 0)),
        scratch_shapes=[pltpu.VMEM((3, WA_Q_HEADS, tq, tq + 2 * WINDOW), F32)],
        compiler_params=_params(("arbitrary", "arbitrary")),
        name="windowed_gqa",
    )(tbl, sink, proj3, proj3, proj3)


def _oproj_kernel(a_ref, b_ref, x_ref, w1_ref, w2_ref, g_ref, wr_ref, x1_ref, h_ref, aff_ref):
    x1 = (x_ref[...] + jnp.dot(a_ref[...], w1_ref[...], preferred_element_type=F32)
          + jnp.dot(b_ref[...], w2_ref[...], preferred_element_type=F32))
    x1_ref[...] = x1
    h = _rms(x1, g_ref[...], RMS_EPS)
    h_hi = h.astype(BF16)
    h_ref[...] = h_hi
    h_lo = (h - h_hi.astype(F32)).astype(BF16)
    wr = wr_ref[...]
    w_hi = wr.astype(BF16)
    w_lo = (wr - w_hi.astype(F32)).astype(BF16)
    hh = jnp.dot(h_hi, jnp.concatenate([w_hi, w_lo], axis=1), preferred_element_type=F32)
    logits = (hh[:, :N_EXPERTS] + hh[:, N_EXPERTS:]
              + jnp.dot(h_lo, w_hi, preferred_element_type=F32))
    e = jnp.exp(logits - jnp.max(logits, axis=1, keepdims=True))
    aff_ref[...] = e / jnp.sum(e, axis=1, keepdims=True)


def _oproj(a, b, x2, w1, w2, g, wr, tm):
    m = x2.shape[0]
    row = lambda n: pl.BlockSpec((tm, n), lambda i: (i, 0))
    full = lambda r, c: pl.BlockSpec((r, c), lambda i: (0, 0))
    return pl.pallas_call(
        _oproj_kernel,
        out_shape=(jax.ShapeDtypeStruct((m, D_MODEL), F32),
                   jax.ShapeDtypeStruct((m, D_MODEL), BF16),
                   jax.ShapeDtypeStruct((m, N_EXPERTS), F32)),
        grid=(m // tm,),
        in_specs=[row(DA_V), row(WA_Q), row(D_MODEL), full(DA_V, D_MODEL), full(WA_Q, D_MODEL),
                  full(1, D_MODEL), full(D_MODEL, N_EXPERTS)],
        out_specs=(row(D_MODEL), row(D_MODEL), row(N_EXPERTS)),
        compiler_params=_params(("parallel",)),
        name="oproj_router",
    )(a, b, x2, w1, w2, g, wr)


def _cumsum_lanes(x):
    n = x.shape[1]
    lane = lax.broadcasted_iota(I32, x.shape, 1)
    shift = 1
    while shift < n:
        x = x + jnp.where(lane >= shift, pltpu.roll(x, shift, 1), 0)
        shift *= 2
    return x


def _select_kernel(aff_ref, slot_ref, excl_ref, *, cap):
    aff = aff_ref[...]

    def count(mask):
        return jnp.sum(mask.astype(F32), axis=1, keepdims=True)

    def body(i, prefix):
        cand = prefix | jnp.left_shift(jnp.int32(1), 30 - i)
        return jnp.where(count(aff >= pltpu.bitcast(cand, F32)) >= cap, cand, prefix)

    thr = lax.fori_loop(0, 31, body, jnp.zeros((aff.shape[0], 1), I32))
    above = aff >= pltpu.bitcast(thr + 1, F32)
    edge = jnp.logical_and(aff >= pltpu.bitcast(thr, F32), jnp.logical_not(above))
    need = cap - count(above)
    edge_i = edge.astype(I32)
    edge_excl = (_cumsum_lanes(edge_i) - edge_i).astype(F32)
    sel = above | (edge & (edge_excl < need))
    sel_i = sel.astype(I32)
    excl = _cumsum_lanes(sel_i) - sel_i
    slot_ref[...] = jnp.where(sel, excl, -1)
    excl_ref[...] = excl


def _select(aff_t, cap):
    shp = jax.ShapeDtypeStruct(aff_t.shape, I32)
    return pl.pallas_call(
        functools.partial(_select_kernel, cap=cap),
        out_shape=(shp, shp),
        compiler_params=_params(None),
        name="expert_select",
    )(aff_t)


def _gather_kernel(cnt_ref, x_ref, slot_ref, xg_hbm, res_ref, obuf, sem, carry_ref, xres_ref, xo, xsem,
                   *, nb, rw):
    b = pl.program_id(0)
    tb = x_ref.shape[0]
    cur = b % 2
    grp = BF16_SUBLANES
    groups = rw // grp

    def lo_of(e, blk):
        return cnt_ref[e * (nb + 1) + blk]

    def base_of(e, blk):
        return pl.multiple_of((lo_of(e, blk) // grp) * grp, grp)

    def out_copy(e, blk, slab):
        return pltpu.make_async_copy(obuf.at[slab, pl.ds(e * rw, rw), :],
                                     xg_hbm.at[e, pl.ds(base_of(e, blk), rw), :], sem.at[slab, e])

    def pack(rel):
        rows = lax.broadcasted_iota(I32, (rw, tb), 0)
        return jnp.where(rel == rows, 1.0, 0.0).astype(BF16)

    @pl.when(b == 0)
    def _():
        carry_ref[...] = jnp.zeros(carry_ref.shape, carry_ref.dtype)
        xo[...] = jnp.zeros(xo.shape, xo.dtype)
        for e in range(N_EXPERTS):
            cp = pltpu.make_async_copy(xo, xg_hbm.at[e, pl.ds(xg_hbm.shape[1] - rw, rw), :], xsem)
            cp.start()
            cp.wait()

    x = x_ref[...]
    onehot = jnp.concatenate([pack(slot_ref[e:e + 1, :] - base_of(e, b)) for e in range(N_EXPERTS)], axis=0)
    res_ref[...] = jnp.dot(onehot, x, preferred_element_type=F32)
    for e in range(N_EXPERTS):
        res_ref[pl.ds(e * rw, grp), :] += carry_ref[e]
        g_hi = (lo_of(e, b + 1) - base_of(e, b)) // grp
        row0 = pl.multiple_of(e * rw + jnp.minimum(g_hi, groups - 1) * grp, grp)
        carry_ref[e] = jnp.where(g_hi < groups, res_ref[pl.ds(row0, grp), :], 0.0)
    obuf[cur] = res_ref[...].astype(BF16)

    @pl.when(b > 0)
    def _():
        for e in range(N_EXPERTS):
            out_copy(e, b - 1, 1 - cur).wait()

    for e in range(N_EXPERTS):
        out_copy(e, b, cur).start()

    for e in range(N_EXPERTS):
        base = base_of(e, b)
        hi = lo_of(e, b + 1)
        n_extra = (jnp.maximum(hi - base - rw, 0) + rw - 1) // rw

        def extra(w, c, e=e, base=base, hi=hi):
            first = pl.multiple_of(base + (w + 1) * rw, grp)
            xres_ref[...] = jnp.dot(pack(slot_ref[e:e + 1, :] - first), x, preferred_element_type=F32)
            xo[...] = xres_ref[...].astype(BF16)
            cp = pltpu.make_async_copy(xo, xg_hbm.at[e, pl.ds(first, rw), :], xsem)
            cp.start()
            cp.wait()
            g_hi = (hi - first) // grp
            row0 = pl.multiple_of(jnp.clip(g_hi, 0, groups - 1) * grp, grp)
            carry_ref[e] = jnp.where(jnp.logical_and(g_hi >= 0, g_hi < groups),
                                     xres_ref[pl.ds(row0, grp), :], carry_ref[e])
            return c

        lax.fori_loop(0, n_extra, extra, 0)

    @pl.when(b == nb - 1)
    def _():
        for e in range(N_EXPERTS):
            out_copy(e, b, cur).wait()


def _gather(cnt, h2, slot, cap, tb):
    m = h2.shape[0]
    nb = m // tb
    rw = 5 * BF16_SUBLANES
    assert cap % BF16_SUBLANES == 0
    return pl.pallas_call(
        functools.partial(_gather_kernel, nb=nb, rw=rw),
        out_shape=jax.ShapeDtypeStruct((N_EXPERTS, cap + rw, D_MODEL), BF16),
        grid_spec=pltpu.PrefetchScalarGridSpec(
            num_scalar_prefetch=1,
            grid=(nb,),
            in_specs=[pl.BlockSpec((tb, D_MODEL), lambda b, c: (b, 0)),
                      pl.BlockSpec((N_EXPERTS, tb), lambda b, c: (0, b))],
            out_specs=pl.BlockSpec(memory_space=pl.ANY),
            scratch_shapes=[pltpu.VMEM((N_EXPERTS * rw, D_MODEL), F32),
                            pltpu.VMEM((2, N_EXPERTS * rw, D_MODEL), BF16),
                            pltpu.SemaphoreType.DMA((2, N_EXPERTS)),
                            pltpu.VMEM((N_EXPERTS, BF16_SUBLANES, D_MODEL), F32),
                            pltpu.VMEM((rw, D_MODEL), F32),
                            pltpu.VMEM((rw, D_MODEL), BF16),
                            pltpu.SemaphoreType.DMA(())]),
        compiler_params=_params(("arbitrary",)),
        name="moe_gather",
    )(cnt, h2, slot)


def _ffn_kernel(x_ref, wg_ref, wu_ref, wd_ref, y_ref, wg_s, wu_s, wd_s):
    @pl.when(pl.program_id(1) == 0)
    def _():
        wg_s[...] = wg_ref[0].astype(BF16)
        wu_s[...] = wu_ref[0].astype(BF16)
        wd_s[...] = wd_ref[0].astype(BF16)

    x = x_ref[0]
    g = jnp.dot(x, wg_s[...], preferred_element_type=F32)
    u = jnp.dot(x, wu_s[...], preferred_element_type=F32)
    h = (g * jax.nn.sigmoid(g) * u).astype(BF16)
    y_ref[0] = jnp.dot(h, wd_s[...], preferred_element_type=F32).astype(y_ref.dtype)


def _ffn(xg, wg, wu, wd, cap, tm):
    wspec = lambda r, c: pl.BlockSpec((1, r, c), lambda e, i: (e, 0, 0))
    return pl.pallas_call(
        _ffn_kernel,
        out_shape=jax.ShapeDtypeStruct((N_EXPERTS, cap, D_MODEL), BF16),
        grid=(N_EXPERTS, cap // tm),
        in_specs=[pl.BlockSpec((1, tm, D_MODEL), lambda e, i: (e, i, 0)),
                  wspec(D_MODEL, EXPERT_FF), wspec(D_MODEL, EXPERT_FF), wspec(EXPERT_FF, D_MODEL)],
        out_specs=pl.BlockSpec((1, tm, D_MODEL), lambda e, i: (e, i, 0)),
        scratch_shapes=[pltpu.VMEM((D_MODEL, EXPERT_FF), BF16), pltpu.VMEM((D_MODEL, EXPERT_FF), BF16),
                        pltpu.VMEM((EXPERT_FF, D_MODEL), BF16)],
        compiler_params=_params(("arbitrary", "arbitrary")),
        name="moe_ffn",
    )(xg, wg, wu, wd)


def _combine_kernel(cnt_ref, x1_ref, slot_ref, gate_ref, fn_ref, y_hbm, o_ref,
                    buf, sem, xbuf, xsem, acc_ref, *, nb, cap, rw):
    b = pl.program_id(0)
    tb = x1_ref.shape[0]
    xr = xbuf.shape[0]

    def window_start(e, blk):
        lo = cnt_ref[e * (nb + 1) + blk]
        return pl.multiple_of(jnp.minimum((lo // BF16_SUBLANES) * BF16_SUBLANES, cap - rw), BF16_SUBLANES)

    def window_copy(e, start, slab):
        return pltpu.make_async_copy(y_hbm.at[e, pl.ds(start, rw), :],
                                     buf.at[slab, pl.ds(e * rw, rw), :], sem.at[slab, e])

    def fetch(blk, slab):
        for e in range(N_EXPERTS):
            window_copy(e, window_start(e, blk), slab).start()

    cur = b % 2

    @pl.when(b == 0)
    def _():
        fetch(0, 0)

    @pl.when(b + 1 < nb)
    def _():
        fetch(b + 1, 1 - cur)

    starts = [window_start(e, b) for e in range(N_EXPERTS)]
    for e in range(N_EXPERTS):
        window_copy(e, starts[e], cur).wait()

    width = N_EXPERTS * rw
    ex = (lax.broadcasted_iota(I32, (N_EXPERTS, width), 0)
          == lax.broadcasted_iota(I32, (N_EXPERTS, width), 1) // rw).astype(BF16)
    lane16 = lax.broadcasted_iota(I32, (1, N_EXPERTS), 1)
    st_vec = jnp.zeros((1, N_EXPERTS), I32)
    for e in range(N_EXPERTS):
        st_vec = jnp.where(lane16 == e, starts[e], st_vec)
    rel = jnp.clip(slot_ref[...] - st_vec, -1, rw).astype(F32).astype(BF16)
    row = (lax.broadcasted_iota(I32, (tb, width), 1) % rw).astype(F32)
    hit = jnp.dot(rel, ex, preferred_element_type=F32) == row
    gate = gate_ref[...]
    g_hi = gate.astype(BF16)
    g_lo = (gate - g_hi.astype(F32)).astype(BF16)
    p_hi = jnp.where(hit, jnp.dot(g_hi, ex, preferred_element_type=F32), 0.0).astype(BF16)
    p_lo = jnp.where(hit, jnp.dot(g_lo, ex, preferred_element_type=F32), 0.0).astype(BF16)
    y = buf[cur]
    acc_ref[...] = (x1_ref[...] + jnp.dot(p_hi, y, preferred_element_type=F32)
                    + jnp.dot(p_lo, y, preferred_element_type=F32))

    xlane = lax.broadcasted_iota(I32, (tb, xr), 1)
    for e in range(N_EXPERTS):
        done = starts[e] + rw
        n_extra = (jnp.maximum(cnt_ref[e * (nb + 1) + b + 1] - done, 0) + xr - 1) // xr

        def extra(w, c, e=e, done=done):
            first = done + w * xr
            cst = pl.multiple_of(jnp.minimum(first, cap - xr), BF16_SUBLANES)
            cp = pltpu.make_async_copy(y_hbm.at[e, pl.ds(cst, xr), :], xbuf, xsem)
            cp.start()
            cp.wait()
            slot = slot_ref[:, e:e + 1]
            hit = jnp.logical_and(slot - cst == xlane, slot >= first)
            onehot = jnp.where(hit, 1.0, 0.0).astype(BF16)
            acc_ref[...] += gate_ref[:, e:e + 1] * jnp.dot(onehot, xbuf[...], preferred_element_type=F32)
            return c

        lax.fori_loop(0, n_extra, extra, 0)

    o_ref[...] = _rms(acc_ref[...], fn_ref[...], RMS_EPS)


def _combine(cnt, x1, slot_t, gate_t, fn, y, tb):
    m = x1.shape[0]
    nb = m // tb
    cap = y.shape[1]
    rw = LANES // 2
    assert cap % BF16_SUBLANES == 0 and cap >= LANES
    return pl.pallas_call(
        functools.partial(_combine_kernel, nb=nb, cap=cap, rw=rw),
        out_shape=jax.ShapeDtypeStruct((m, D_MODEL), F32),
        grid_spec=pltpu.PrefetchScalarGridSpec(
            num_scalar_prefetch=1,
            grid=(nb,),
            in_specs=[pl.BlockSpec((tb, D_MODEL), lambda b, c: (b, 0)),
                      pl.BlockSpec((tb, N_EXPERTS), lambda b, c: (b, 0)),
                      pl.BlockSpec((tb, N_EXPERTS), lambda b, c: (b, 0)),
                      pl.BlockSpec((1, D_MODEL), lambda b, c: (0, 0)),
                      pl.BlockSpec(memory_space=pl.ANY)],
            out_specs=pl.BlockSpec((tb, D_MODEL), lambda b, c: (b, 0)),
            scratch_shapes=[pltpu.VMEM((2, N_EXPERTS * rw, D_MODEL), BF16),
                            pltpu.SemaphoreType.DMA((2, N_EXPERTS)),
                            pltpu.VMEM((LANES, D_MODEL), BF16),
                            pltpu.SemaphoreType.DMA(()),
                            pltpu.VMEM((tb, D_MODEL), F32)]),
        compiler_params=_params(("arbitrary",)),
        name="moe_combine",
    )(cnt, x1, slot_t, gate_t, fn, y)


def _pick(n, prefs):
    for p in prefs:
        if n % p == 0:
            return p
    raise ValueError(f"no tile in {prefs} divides {n}")


def _trunk(x, tbl, sink, lam, subln_g, norm1, w_in, w_o1, w_o2, norm2, wr, wg, wu, wd, fn):
    bsz, s, d = x.shape
    m = bsz * s
    cap = CAPACITY_FACTOR * m // N_EXPERTS
    x2 = x.reshape(m, d)
    proj = _inproj(x2, norm1, w_in, _pick(m, (1024, 512, 256)))
    proj3 = proj.reshape(bsz, s, IN_COLS)
    out_a = _diff_attention(proj3, tbl, *lam, subln_g, _pick(s, (512, 256)))
    out_b = _windowed_gqa(proj3, tbl, sink, _pick(s, (256,)))
    x1, h2, aff = _oproj(out_a.reshape(m, DA_V), out_b.reshape(m, WA_Q), x2, w_o1, w_o2,
                         norm2, wr, _pick(m, (1024, 512, 256)))
    slot, excl = _select(aff.T, cap)
    tb = 256
    nb = m // tb
    cnt = jnp.concatenate([excl[:, ::tb], jnp.full((N_EXPERTS, 1), cap, I32)], axis=1).reshape(-1)
    xg = _gather(cnt, h2, slot, cap, tb)
    y = _ffn(xg, wg, wu, wd, cap, _pick(cap, (512, 256, 128)))
    out = _combine(cnt, x1, slot.T, aff, fn, y, tb)
    return out.reshape(bsz, s, d)


def kernel(x_prompt, x_sample, rel_bias, norm1, w_in, lam_q1, lam_k1, lam_q2, lam_k2, subln_g, sink,
           w_o, norm2, w_router, w_gate, w_up, w_down, final_norm):
    col = jnp.arange(IN_COLS)
    is_q = (col < DA_HEADS * 2 * HEAD_DIM) | ((col >= 3 * DA_V) & (col < 3 * DA_V + WA_Q))
    col_scale = jnp.where(is_q, HEAD_DIM ** -0.5 * LOG2E, 1.0).astype(F32)
    w_in_s = (w_in[0] * col_scale).astype(BF16)
    args = dict(
        tbl=rel_bias, sink=sink,
        lam=(lam_q1, lam_k1, lam_q2, lam_k2), subln_g=subln_g,
        norm1=norm1, w_in=w_in_s,
        w_o1=w_o[0, :DA_V].astype(BF16), w_o2=w_o[0, DA_V:].astype(BF16),
        norm2=norm2, wr=w_router[0],
        wg=w_gate[0], wu=w_up[0], wd=w_down[0],
        fn=final_norm.reshape(1, D_MODEL))
    return (_trunk(x_prompt, **args), _trunk(x_sample, **args))
```

```python
import functools
import math

import jax
import jax.numpy as jnp
from jax import lax
from jax.experimental import pallas as pl
from jax.experimental.pallas import tpu as pltpu

F32 = jnp.float32
BF16 = jnp.bfloat16
I32 = jnp.int32

D_MODEL = 1024
HEAD_DIM = 64
DA_HEADS = 4
WA_Q_HEADS = 8
WA_REP = 4
WINDOW = 128
IN_COLS = 2304
DA_V = 512
WA_Q = 512
N_BUCKETS = 32
MAX_DISTANCE = 128
N_EXPERTS = 16
EXPERT_FF = 1024
CAPACITY_FACTOR = 2
RMS_EPS = 1e-6
SUBLN_EPS = 1e-5
LAMBDA_INIT = 0.8 - 0.6 * math.exp(-0.3 * 0)
LOG2E = math.log2(math.e)

LANES = 128
BF16_SUBLANES = 16
VMEM_LIMIT = 56 * 1024 * 1024
NT_DIMS = (((1,), (1,)), ((), ()))


def _params(sem, vmem=VMEM_LIMIT):
    return pltpu.CompilerParams(dimension_semantics=sem, vmem_limit_bytes=vmem)


def _rms(x, g, eps):
    return x * lax.rsqrt(jnp.mean(x * x, axis=-1, keepdims=True) + eps) * g


def _inproj_kernel(x_ref, g_ref, w_ref, o_ref):
    h = _rms(x_ref[...], g_ref[...], RMS_EPS).astype(BF16)
    o_ref[...] = jnp.dot(h, w_ref[...], preferred_element_type=F32).astype(o_ref.dtype)


def _inproj(x2, g, w, tm):
    m = x2.shape[0]
    return pl.pallas_call(
        _inproj_kernel,
        out_shape=jax.ShapeDtypeStruct((m, IN_COLS), BF16),
        grid=(m // tm,),
        in_specs=[pl.BlockSpec((tm, D_MODEL), lambda i: (i, 0)),
                  pl.BlockSpec((1, D_MODEL), lambda i: (0, 0)),
                  pl.BlockSpec((D_MODEL, IN_COLS), lambda i: (0, 0))],
        out_specs=pl.BlockSpec((tm, IN_COLS), lambda i: (i, 0)),
        compiler_params=_params(("parallel",)),
        name="inproj",
    )(x2, g, w)


def _t5_bucket(rel):
    nb = N_BUCKETS // 2
    max_exact = nb // 2
    ret = jnp.where(rel > 0, nb, 0)
    n = jnp.abs(rel)
    nf = jnp.maximum(n, 1).astype(F32)
    frac = jnp.log2(nf / max_exact) * ((nb - max_exact) / math.log2(MAX_DISTANCE / max_exact))
    large = max_exact + jnp.where(frac >= 0, jnp.floor(frac), jnp.ceil(frac)).astype(I32)
    large = jnp.minimum(large, nb - 1)
    return ret + jnp.where(n < max_exact, n, large)


def _bias_lookup(bucket, tbl_ref, col):
    out = jnp.zeros(bucket.shape, F32)
    for b in range(N_BUCKETS):
        out = jnp.where(bucket == b, tbl_ref[b, col], out)
    return out


def _toeplitz(row, tq, tk):
    width = tq + tk
    rb = jnp.broadcast_to(row, (tq, width))
    rolled = pltpu.roll(rb, width - tq + 1, 1, stride=1, stride_axis=0)
    return rolled[:, :tk]


def _da_kernel(tbl_ref, lq1_ref, lk1_ref, lq2_ref, lk2_ref, q_ref, k_ref, v_ref, g_ref,
               o_ref, qm_ref, m_ref, l_ref, acc_ref, bias_ref, *, t, seq):
    h = pl.program_id(1)
    qi = pl.program_id(2)
    nk = seq // t

    @pl.when(qi == 0)
    def _():
        for d in range(3):
            jj = lax.broadcasted_iota(I32, (1, 2 * t), 1)
            rel = jnp.clip((d - 1) * t + jj - (t - 1), -MAX_DISTANCE, MAX_DISTANCE)
            bias_ref[d] = _toeplitz(_bias_lookup(_t5_bucket(rel), tbl_ref, h) * LOG2E, t, t)

    c_left = tbl_ref[N_BUCKETS // 2 - 1, h] * LOG2E
    c_right = tbl_ref[N_BUCKETS - 1, h] * LOG2E

    q = q_ref[0]
    lane = lax.broadcasted_iota(I32, q.shape, 1)
    qm_ref[0] = jnp.where(lane < HEAD_DIM, q, jnp.zeros_like(q))
    qm_ref[1] = jnp.where(lane >= HEAD_DIM, q, jnp.zeros_like(q))
    m_ref[...] = jnp.full(m_ref.shape, -jnp.inf, F32)
    l_ref[...] = jnp.zeros(l_ref.shape, F32)
    acc_ref[...] = jnp.zeros(acc_ref.shape, F32)

    def step(tile, width, bias_idx):
        start = pl.multiple_of(tile * t, t)
        k = k_ref[0, pl.ds(start, width), :]
        v = v_ref[0, pl.ds(start, width), :]
        reps = width // LANES
        logits = [lax.dot_general(qm_ref[mp], k, NT_DIMS, preferred_element_type=F32) for mp in range(2)]
        for mp in range(2):
            s = logits[mp]
            if bias_idx is not None:
                s = s + jnp.concatenate([bias_ref[bias_idx + i] for i in range(width // t)], axis=1)
            m_prev = m_ref[mp]
            m_new = jnp.maximum(m_prev, jnp.max(s, axis=1, keepdims=True))
            alpha = jnp.exp2(m_prev - m_new)
            p = jnp.exp2(s - jnp.tile(m_new, (1, reps)))
            psum = p[:, :LANES]
            for r in range(1, reps):
                psum = psum + p[:, r * LANES:(r + 1) * LANES]
            l_ref[mp] = alpha * l_ref[mp] + psum
            acc_ref[mp] = alpha * acc_ref[mp] + jnp.dot(p.astype(BF16), v, preferred_element_type=F32)
            m_ref[mp] = m_new

    def far(first, count):
        def quad(j, c):
            step(first + 4 * j, 4 * t, None)
            return c

        def pair(j, c):
            step(first + (count // 4) * 4, 2 * t, None)
            return c

        def single(j, c):
            step(first + count - 1, t, None)
            return c

        lax.fori_loop(0, count // 4, quad, 0)
        lax.fori_loop(0, (count % 4) // 2, pair, 0)
        lax.fori_loop(0, count % 2, single, 0)

    def near_pair(j, c):
        step(first_near, 2 * t, first_near - qi + 1)
        return c

    def near_single(j, c):
        step(first_right - 1, t, first_right - qi)
        return c

    first_near = jnp.maximum(qi - 1, 0)
    first_right = jnp.minimum(qi + 2, nk)
    far(0, first_near)
    m_ref[...] += c_left
    n_near = first_right - first_near
    lax.fori_loop(0, n_near // 2, near_pair, 0)
    lax.fori_loop(0, n_near % 2, near_single, 0)
    m_ref[...] -= c_right
    far(first_right, nk - first_right)

    outs = [acc_ref[mp] / jnp.sum(l_ref[mp], axis=1, keepdims=True) for mp in range(2)]
    lam = (jnp.exp(jnp.sum(lq1_ref[...] * lk1_ref[...], axis=1, keepdims=True))
           - jnp.exp(jnp.sum(lq2_ref[...] * lk2_ref[...], axis=1, keepdims=True))
           + LAMBDA_INIT)
    o = outs[0] - lam * outs[1]
    o = _rms(o, g_ref[...], SUBLN_EPS) * (1.0 - LAMBDA_INIT)
    o_ref[0] = o.astype(o_ref.dtype)


def _diff_attention(proj3, tbl, lq1, lk1, lq2, lk2, subln_g, t):
    b, s, _ = proj3.shape
    assert t >= MAX_DISTANCE and s % t == 0
    vec = lambda n: pl.BlockSpec((1, n), lambda bi, hi, qi: (0, 0))
    kern = functools.partial(_da_kernel, t=t, seq=s)
    return pl.pallas_call(
        kern,
        out_shape=jax.ShapeDtypeStruct((b, s, DA_V), BF16),
        grid=(b, DA_HEADS, s // t),
        in_specs=[pl.BlockSpec(memory_space=pltpu.SMEM),
                  vec(HEAD_DIM), vec(HEAD_DIM), vec(HEAD_DIM), vec(HEAD_DIM),
                  pl.BlockSpec((1, t, LANES), lambda bi, hi, qi: (bi, qi, hi)),
                  pl.BlockSpec((1, s, LANES), lambda bi, hi, qi: (bi, 0, DA_HEADS + hi)),
                  pl.BlockSpec((1, s, LANES), lambda bi, hi, qi: (bi, 0, 2 * DA_HEADS + hi)),
                  vec(2 * HEAD_DIM)],
        out_specs=pl.BlockSpec((1, t, LANES), lambda bi, hi, qi: (bi, qi, hi)),
        scratch_shapes=[pltpu.VMEM((2, t, LANES), BF16), pltpu.VMEM((2, t, LANES), F32),
                        pltpu.VMEM((2, t, LANES), F32), pltpu.VMEM((2, t, 2 * HEAD_DIM), F32),
                        pltpu.VMEM((3, t, t), F32)],
        compiler_params=_params(("parallel", "parallel", "arbitrary")),
        name="diff_attention",
    )(tbl, lq1, lk1, lq2, lk2, proj3, proj3, proj3, subln_g)


def _wa_kernel(tbl_ref, sink_ref, q_ref, k_ref, v_ref, o_ref, bias_ref, *, tq, seq):
    q0 = pl.program_id(1) * tq
    win = tq + 2 * WINDOW
    ws = pl.multiple_of(jnp.clip(q0 - WINDOW, 0, seq - win), LANES)

    @pl.when(jnp.logical_and(pl.program_id(0) == 0, pl.program_id(1) == 0))
    def _():
        jj = lax.broadcasted_iota(I32, (1, win + tq), 1)
        for d in range(3):
            rel = jj - d * WINDOW - (tq - 1)
            valid = jnp.abs(rel) <= WINDOW
            bucket = _t5_bucket(jnp.clip(rel, -MAX_DISTANCE, MAX_DISTANCE))
            for hq in range(WA_Q_HEADS):
                row = jnp.where(valid, _bias_lookup(bucket, tbl_ref, DA_HEADS + hq) * LOG2E, -jnp.inf)
                bias_ref[d, hq] = _toeplitz(row, tq, win)

    variant = (q0 - ws) // WINDOW
    kw = k_ref[0, pl.ds(ws, win), :]
    vw = v_ref[0, pl.ds(ws, win), :]
    q = q_ref[0]
    low = lax.broadcasted_iota(I32, (tq, LANES), 1) < HEAD_DIM
    kdup = [jnp.concatenate([kw[:, g * HEAD_DIM:(g + 1) * HEAD_DIM]] * 2, axis=1) for g in range(2)]
    ones = jnp.ones((win, LANES), BF16)
    vext = [jnp.concatenate([vw[:, g * HEAD_DIM:(g + 1) * HEAD_DIM]] * 2 + [ones], axis=1) for g in range(2)]
    logits = []
    for hq in range(WA_Q_HEADS):
        qp = q[:, (hq // 2) * LANES:(hq // 2 + 1) * LANES]
        qm = jnp.where(low if hq % 2 == 0 else jnp.logical_not(low), qp, jnp.zeros_like(qp))
        logits.append(lax.dot_general(qm, kdup[hq // WA_REP], NT_DIMS, preferred_element_type=F32))
    outs = []
    for pair in range(WA_Q_HEADS // 2):
        g = (2 * pair) // WA_REP
        halves = []
        for half in range(2):
            hq = 2 * pair + half
            s = logits[hq] + bias_ref[variant, hq]
            sk = sink_ref[0, hq] * LOG2E
            m = jnp.maximum(jnp.full((tq, LANES), sk, F32), jnp.max(s, axis=1, keepdims=True))
            e = jnp.exp2(s - jnp.tile(m, (1, win // LANES))).astype(BF16)
            pv = jnp.dot(e, vext[g], preferred_element_type=F32)
            halves.append(pv[:, :LANES] / (pv[:, LANES:] + jnp.exp2(sk - m)))
        outs.append(jnp.where(low, halves[0], halves[1]))
    o_ref[0] = jnp.concatenate(outs, axis=1).astype(o_ref.dtype)


def _windowed_gqa(proj3, tbl, sink, tq):
    b, s, _ = proj3.shape
    kern = functools.partial(_wa_kernel, tq=tq, seq=s)
    kv_blk = (IN_COLS - 2 * LANES) // LANES
    return pl.pallas_call(
        kern,
        out_shape=jax.ShapeDtypeStruct((b, s, WA_Q), BF16),
        grid=(b, s // tq),
        in_specs=[pl.BlockSpec(memory_space=pltpu.SMEM),
                  pl.BlockSpec(memory_space=pltpu.SMEM),
                  pl.BlockSpec((1, tq, WA_Q), lambda bi, qi: (bi, qi, 3)),
                  pl.BlockSpec((1, s, LANES), lambda bi, qi: (bi, 0, kv_blk)),
                  pl.BlockSpec((1, s, LANES), lambda bi, qi: (bi, 0, kv_blk + 1))],
        out_specs=pl.BlockSpec((1, tq, WA_Q), lambda bi, qi: (bi, qi, 0)),
        scratch_shapes=[pltpu.VMEM((3, WA_Q_HEADS, tq, tq + 2 * WINDOW), F32)],
        compiler_params=_params(("arbitrary", "arbitrary")),
        name="windowed_gqa",
    )(tbl, sink, proj3, proj3, proj3)


def _oproj_kernel(a_ref, b_ref, x_ref, w1_ref, w2_ref, g_ref, wr_ref, x1_ref, h_ref, aff_ref):
    x1 = (x_ref[...] + jnp.dot(a_ref[...], w1_ref[...], preferred_element_type=F32)
          + jnp.dot(b_ref[...], w2_ref[...], preferred_element_type=F32))
    x1_ref[...] = x1
    h = _rms(x1, g_ref[...], RMS_EPS)
    h_hi = h.astype(BF16)
    h_ref[...] = h_hi
    h_lo = (h - h_hi.astype(F32)).astype(BF16)
    wr = wr_ref[...]
    w_hi = wr.astype(BF16)
    w_lo = (wr - w_hi.astype(F32)).astype(BF16)
    hh = jnp.dot(h_hi, jnp.concatenate([w_hi, w_lo], axis=1), preferred_element_type=F32)
    logits = (hh[:, :N_EXPERTS] + hh[:, N_EXPERTS:]
              + jnp.dot(h_lo, w_hi, preferred_element_type=F32))
    e = jnp.exp(logits - jnp.max(logits, axis=1, keepdims=True))
    aff_ref[...] = e / jnp.sum(e, axis=1, keepdims=True)


def _oproj(a, b, x2, w1, w2, g, wr, tm):
    m = x2.shape[0]
    row = lambda n: pl.BlockSpec((tm, n), lambda i: (i, 0))
    full = lambda r, c: pl.BlockSpec((r, c), lambda i: (0, 0))
    return pl.pallas_call(
        _oproj_kernel,
        out_shape=(jax.ShapeDtypeStruct((m, D_MODEL), F32),
                   jax.ShapeDtypeStruct((m, D_MODEL), BF16),
                   jax.ShapeDtypeStruct((m, N_EXPERTS), F32)),
        grid=(m // tm,),
        in_specs=[row(DA_V), row(WA_Q), row(D_MODEL), full(DA_V, D_MODEL), full(WA_Q, D_MODEL),
                  full(1, D_MODEL), full(D_MODEL, N_EXPERTS)],
        out_specs=(row(D_MODEL), row(D_MODEL), row(N_EXPERTS)),
        compiler_params=_params(("parallel",)),
        name="oproj_router",
    )(a, b, x2, w1, w2, g, wr)


def _cumsum_lanes(x):
    n = x.shape[1]
    lane = lax.broadcasted_iota(I32, x.shape, 1)
    shift = 1
    while shift < n:
        x = x + jnp.where(lane >= shift, pltpu.roll(x, shift, 1), 0)
        shift *= 2
    return x


def _select_kernel(aff_ref, slot_ref, excl_ref, *, cap):
    aff = aff_ref[...]

    def count(mask):
        return jnp.sum(mask.astype(F32), axis=1, keepdims=True)

    def body(i, prefix):
        cand = prefix | jnp.left_shift(jnp.int32(1), 30 - i)
        return jnp.where(count(aff >= pltpu.bitcast(cand, F32)) >= cap, cand, prefix)

    thr = lax.fori_loop(0, 31, body, jnp.zeros((aff.shape[0], 1), I32))
    above = aff >= pltpu.bitcast(thr + 1, F32)
    edge = jnp.logical_and(aff >= pltpu.bitcast(thr, F32), jnp.logical_not(above))
    need = cap - count(above)
    edge_i = edge.astype(I32)
    edge_excl = (_cumsum_lanes(edge_i) - edge_i).astype(F32)
    sel = above | (edge & (edge_excl < need))
    sel_i = sel.astype(I32)
    excl = _cumsum_lanes(sel_i) - sel_i
    slot_ref[...] = jnp.where(sel, excl, -1)
    excl_ref[...] = excl


def _select(aff_t, cap):
    shp = jax.ShapeDtypeStruct(aff_t.shape, I32)
    return pl.pallas_call(
        functools.partial(_select_kernel, cap=cap),
        out_shape=(shp, shp),
        compiler_params=_params(None),
        name="expert_select",
    )(aff_t)


def _gather_kernel(cnt_ref, x_ref, slot_ref, xg_hbm, res_ref, obuf, sem, carry_ref, xres_ref, xo, xsem,
                   *, nb, rw):
    b = pl.program_id(0)
    tb = x_ref.shape[0]
    cur = b % 2
    grp = BF16_SUBLANES
    groups = rw // grp

    def lo_of(e, blk):
        return cnt_ref[e * (nb + 1) + blk]

    def base_of(e, blk):
        return pl.multiple_of((lo_of(e, blk) // grp) * grp, grp)

    def out_copy(e, blk, slab):
        return pltpu.make_async_copy(obuf.at[slab, pl.ds(e * rw, rw), :],
                                     xg_hbm.at[e, pl.ds(base_of(e, blk), rw), :], sem.at[slab, e])

    def pack(rel):
        rows = lax.broadcasted_iota(I32, (rw, tb), 0)
        return jnp.where(rel == rows, 1.0, 0.0).astype(BF16)

    @pl.when(b == 0)
    def _():
        carry_ref[...] = jnp.zeros(carry_ref.shape, carry_ref.dtype)
        xo[...] = jnp.zeros(xo.shape, xo.dtype)
        for e in range(N_EXPERTS):
            cp = pltpu.make_async_copy(xo, xg_hbm.at[e, pl.ds(xg_hbm.shape[1] - rw, rw), :], xsem)
            cp.start()
            cp.wait()

    x = x_ref[...]
    onehot = jnp.concatenate([pack(slot_ref[e:e + 1, :] - base_of(e, b)) for e in range(N_EXPERTS)], axis=0)
    res_ref[...] = jnp.dot(onehot, x, preferred_element_type=F32)
    for e in range(N_EXPERTS):
        res_ref[pl.ds(e * rw, grp), :] += carry_ref[e]
        g_hi = (lo_of(e, b + 1) - base_of(e, b)) // grp
        row0 = pl.multiple_of(e * rw + jnp.minimum(g_hi, groups - 1) * grp, grp)
        carry_ref[e] = jnp.where(g_hi < groups, res_ref[pl.ds(row0, grp), :], 0.0)
    obuf[cur] = res_ref[...].astype(BF16)

    @pl.when(b > 0)
    def _():
        for e in range(N_EXPERTS):
            out_copy(e, b - 1, 1 - cur).wait()

    for e in range(N_EXPERTS):
        out_copy(e, b, cur).start()

    for e in range(N_EXPERTS):
        base = base_of(e, b)
        hi = lo_of(e, b + 1)
        n_extra = (jnp.maximum(hi - base - rw, 0) + rw - 1) // rw

        def extra(w, c, e=e, base=base, hi=hi):
            first = pl.multiple_of(base + (w + 1) * rw, grp)
            xres_ref[...] = jnp.dot(pack(slot_ref[e:e + 1, :] - first), x, preferred_element_type=F32)
            xo[...] = xres_ref[...].astype(BF16)
            cp = pltpu.make_async_copy(xo, xg_hbm.at[e, pl.ds(first, rw), :], xsem)
            cp.start()
            cp.wait()
            g_hi = (hi - first) // grp
            row0 = pl.multiple_of(jnp.clip(g_hi, 0, groups - 1) * grp, grp)
            carry_ref[e] = jnp.where(jnp.logical_and(g_hi >= 0, g_hi < groups),
                                     xres_ref[pl.ds(row0, grp), :], carry_ref[e])
            return c

        lax.fori_loop(0, n_extra, extra, 0)

    @pl.when(b == nb - 1)
    def _():
        for e in range(N_EXPERTS):
            out_copy(e, b, cur).wait()


def _gather(cnt, h2, slot, cap, tb):
    m = h2.shape[0]
    nb = m // tb
    rw = 5 * BF16_SUBLANES
    assert cap % BF16_SUBLANES == 0
    return pl.pallas_call(
        functools.partial(_gather_kernel, nb=nb, rw=rw),
        out_shape=jax.ShapeDtypeStruct((N_EXPERTS, cap + rw, D_MODEL), BF16),
        grid_spec=pltpu.PrefetchScalarGridSpec(
            num_scalar_prefetch=1,
            grid=(nb,),
            in_specs=[pl.BlockSpec((tb, D_MODEL), lambda b, c: (b, 0)),
                      pl.BlockSpec((N_EXPERTS, tb), lambda b, c: (0, b))],
            out_specs=pl.BlockSpec(memory_space=pl.ANY),
            scratch_shapes=[pltpu.VMEM((N_EXPERTS * rw, D_MODEL), F32),
                            pltpu.VMEM((2, N_EXPERTS * rw, D_MODEL), BF16),
                            pltpu.SemaphoreType.DMA((2, N_EXPERTS)),
                            pltpu.VMEM((N_EXPERTS, BF16_SUBLANES, D_MODEL), F32),
                            pltpu.VMEM((rw, D_MODEL), F32),
                            pltpu.VMEM((rw, D_MODEL), BF16),
                            pltpu.SemaphoreType.DMA(())]),
        compiler_params=_params(("arbitrary",)),
        name="moe_gather",
    )(cnt, h2, slot)


def _ffn_kernel(x_ref, wg_ref, wu_ref, wd_ref, y_ref, wg_s, wu_s, wd_s):
    @pl.when(pl.program_id(1) == 0)
    def _():
        wg_s[...] = wg_ref[0].astype(BF16)
        wu_s[...] = wu_ref[0].astype(BF16)
        wd_s[...] = wd_ref[0].astype(BF16)

    x = x_ref[0]
    g = jnp.dot(x, wg_s[...], preferred_element_type=F32)
    u = jnp.dot(x, wu_s[...], preferred_element_type=F32)
    h = (g * jax.nn.sigmoid(g) * u).astype(BF16)
    y_ref[0] = jnp.dot(h, wd_s[...], preferred_element_type=F32).astype(y_ref.dtype)


def _ffn(xg, wg, wu, wd, cap, tm):
    wspec = lambda r, c: pl.BlockSpec((1, r, c), lambda e, i: (e, 0, 0))
    return pl.pallas_call(
        _ffn_kernel,
        out_shape=jax.ShapeDtypeStruct((N_EXPERTS, cap, D_MODEL), BF16),
        grid=(N_EXPERTS, cap // tm),
        in_specs=[pl.BlockSpec((1, tm, D_MODEL), lambda e, i: (e, i, 0)),
                  wspec(D_MODEL, EXPERT_FF), wspec(D_MODEL, EXPERT_FF), wspec(EXPERT_FF, D_MODEL)],
        out_specs=pl.BlockSpec((1, tm, D_MODEL), lambda e, i: (e, i, 0)),
        scratch_shapes=[pltpu.VMEM((D_MODEL, EXPERT_FF), BF16), pltpu.VMEM((D_MODEL, EXPERT_FF), BF16),
                        pltpu.VMEM((EXPERT_FF, D_MODEL), BF16)],
        compiler_params=_params(("arbitrary", "arbitrary")),
        name="moe_ffn",
    )(xg, wg, wu, wd)


def _combine_kernel(cnt_ref, x1_ref, slot_ref, gate_ref, fn_ref, y_hbm, o_ref,
                    buf, sem, xbuf, xsem, acc_ref, *, nb, cap, rw):
    b = pl.program_id(0)
    tb = x1_ref.shape[0]
    xr = xbuf.shape[0]

    def window_start(e, blk):
        lo = cnt_ref[e * (nb + 1) + blk]
        return pl.multiple_of(jnp.minimum((lo // BF16_SUBLANES) * BF16_SUBLANES, cap - rw), BF16_SUBLANES)

    def window_copy(e, start, slab):
        return pltpu.make_async_copy(y_hbm.at[e, pl.ds(start, rw), :],
                                     buf.at[slab, pl.ds(e * rw, rw), :], sem.at[slab, e])

    def fetch(blk, slab):
        for e in range(N_EXPERTS):
            window_copy(e, window_start(e, blk), slab).start()

    cur = b % 2

    @pl.when(b == 0)
    def _():
        fetch(0, 0)

    @pl.when(b + 1 < nb)
    def _():
        fetch(b + 1, 1 - cur)

    starts = [window_start(e, b) for e in range(N_EXPERTS)]
    for e in range(N_EXPERTS):
        window_copy(e, starts[e], cur).wait()

    width = N_EXPERTS * rw
    ex = (lax.broadcasted_iota(I32, (N_EXPERTS, width), 0)
          == lax.broadcasted_iota(I32, (N_EXPERTS, width), 1) // rw).astype(BF16)
    lane16 = lax.broadcasted_iota(I32, (1, N_EXPERTS), 1)
    st_vec = jnp.zeros((1, N_EXPERTS), I32)
    for e in range(N_EXPERTS):
        st_vec = jnp.where(lane16 == e, starts[e], st_vec)
    rel = jnp.clip(slot_ref[...] - st_vec, -1, rw).astype(F32).astype(BF16)
    row = (lax.broadcasted_iota(I32, (tb, width), 1) % rw).astype(F32)
    hit = jnp.dot(rel, ex, preferred_element_type=F32) == row
    gate = gate_ref[...]
    g_hi = gate.astype(BF16)
    g_lo = (gate - g_hi.astype(F32)).astype(BF16)
    p_hi = jnp.where(hit, jnp.dot(g_hi, ex, preferred_element_type=F32), 0.0).astype(BF16)
    p_lo = jnp.where(hit, jnp.dot(g_lo, ex, preferred_element_type=F32), 0.0).astype(BF16)
    both = jnp.dot(jnp.concatenate([p_hi, p_lo], axis=0), buf[cur], preferred_element_type=F32)
    acc_ref[...] = x1_ref[...] + both[:tb] + both[tb:]

    xlane = lax.broadcasted_iota(I32, (tb, xr), 1)
    for e in range(N_EXPERTS):
        done = starts[e] + rw
        n_extra = (jnp.maximum(cnt_ref[e * (nb + 1) + b + 1] - done, 0) + xr - 1) // xr

        def extra(w, c, e=e, done=done):
            first = done + w * xr
            cst = pl.multiple_of(jnp.minimum(first, cap - xr), BF16_SUBLANES)
            cp = pltpu.make_async_copy(y_hbm.at[e, pl.ds(cst, xr), :], xbuf, xsem)
            cp.start()
            cp.wait()
            slot = slot_ref[:, e:e + 1]
            hit = jnp.logical_and(slot - cst == xlane, slot >= first)
            onehot = jnp.where(hit, 1.0, 0.0).astype(BF16)
            acc_ref[...] += gate_ref[:, e:e + 1] * jnp.dot(onehot, xbuf[...], preferred_element_type=F32)
            return c

        lax.fori_loop(0, n_extra, extra, 0)

    o_ref[...] = _rms(acc_ref[...], fn_ref[...], RMS_EPS)


def _combine(cnt, x1, slot_t, gate_t, fn, y, tb):
    m = x1.shape[0]
    nb = m // tb
    cap = y.shape[1]
    rw = LANES // 2
    assert cap % BF16_SUBLANES == 0 and cap >= LANES
    return pl.pallas_call(
        functools.partial(_combine_kernel, nb=nb, cap=cap, rw=rw),
        out_shape=jax.ShapeDtypeStruct((m, D_MODEL), F32),
        grid_spec=pltpu.PrefetchScalarGridSpec(
            num_scalar_prefetch=1,
            grid=(nb,),
            in_specs=[pl.BlockSpec((tb, D_MODEL), lambda b, c: (b, 0)),
                      pl.BlockSpec((tb, N_EXPERTS), lambda b, c: (b, 0)),
                      pl.BlockSpec((tb, N_EXPERTS), lambda b, c: (b, 0)),
                      pl.BlockSpec((1, D_MODEL), lambda b, c: (0, 0)),
                      pl.BlockSpec(memory_space=pl.ANY)],
            out_specs=pl.BlockSpec((tb, D_MODEL), lambda b, c: (b, 0)),
            scratch_shapes=[pltpu.VMEM((2, N_EXPERTS * rw, D_MODEL), BF16),
                            pltpu.SemaphoreType.DMA((2, N_EXPERTS)),
                            pltpu.VMEM((LANES, D_MODEL), BF16),
                            pltpu.SemaphoreType.DMA(()),
                            pltpu.VMEM((tb, D_MODEL), F32)]),
        compiler_params=_params(("arbitrary",)),
        name="moe_combine",
    )(cnt, x1, slot_t, gate_t, fn, y)


def _pick(n, prefs):
    for p in prefs:
        if n % p == 0:
            return p
    raise ValueError(f"no tile in {prefs} divides {n}")


def _trunk(x, tbl, sink, lam, subln_g, norm1, w_in, w_o1, w_o2, norm2, wr, wg, wu, wd, fn):
    bsz, s, d = x.shape
    m = bsz * s
    cap = CAPACITY_FACTOR * m // N_EXPERTS
    x2 = x.reshape(m, d)
    proj = _inproj(x2, norm1, w_in, _pick(m, (1024, 512, 256)))
    proj3 = proj.reshape(bsz, s, IN_COLS)
    out_a = _diff_attention(proj3, tbl, *lam, subln_g, _pick(s, (512, 256)))
    out_b = _windowed_gqa(proj3, tbl, sink, _pick(s, (256,)))
    x1, h2, aff = _oproj(out_a.reshape(m, DA_V), out_b.reshape(m, WA_Q), x2, w_o1, w_o2,
                         norm2, wr, _pick(m, (1024, 512, 256)))
    slot, excl = _select(aff.T, cap)
    tb = 256
    nb = m // tb
    cnt = jnp.concatenate([excl[:, ::tb], jnp.full((N_EXPERTS, 1), cap, I32)], axis=1).reshape(-1)
    xg = _gather(cnt, h2, slot, cap, tb)
    y = _ffn(xg, wg, wu, wd, cap, _pick(cap, (1024, 512, 256, 128)))
    out = _combine(cnt, x1, slot.T, aff, fn, y, tb)
    return out.reshape(bsz, s, d)


def kernel(x_prompt, x_sample, rel_bias, norm1, w_in, lam_q1, lam_k1, lam_q2, lam_k2, subln_g, sink,
           w_o, norm2, w_router, w_gate, w_up, w_down, final_norm):
    col = jnp.arange(IN_COLS)
    is_q = (col < DA_HEADS * 2 * HEAD_DIM) | ((col >= 3 * DA_V) & (col < 3 * DA_V + WA_Q))
    col_scale = jnp.where(is_q, HEAD_DIM ** -0.5 * LOG2E, 1.0).astype(F32)
    w_in_s = (w_in[0] * col_scale).astype(BF16)
    args = dict(
        tbl=rel_bias, sink=sink,
        lam=(lam_q1, lam_k1, lam_q2, lam_k2), subln_g=subln_g,
        norm1=norm1, w_in=w_in_s,
        w_o1=w_o[0, :DA_V].astype(BF16), w_o2=w_o[0, DA_V:].astype(BF16),
        norm2=norm2, wr=w_router[0],
        wg=w_gate[0], wu=w_up[0], wd=w_down[0],
        fn=final_norm.reshape(1, D_MODEL))
    return (_trunk(x_prompt, **args), _trunk(x_sample, **args))
```

```python
import functools
import math

import jax
import jax.numpy as jnp
from jax import lax
from jax.experimental import pallas as pl
from jax.experimental.pallas import tpu as pltpu

F32 = jnp.float32
BF16 = jnp.bfloat16
I32 = jnp.int32

D_MODEL = 1024
HEAD_DIM = 64
DA_HEADS = 4
WA_Q_HEADS = 8
WA_REP = 4
WINDOW = 128
IN_COLS = 2304
DA_V = 512
WA_Q = 512
N_BUCKETS = 32
MAX_DISTANCE = 128
N_EXPERTS = 16
EXPERT_FF = 1024
CAPACITY_FACTOR = 2
RMS_EPS = 1e-6
SUBLN_EPS = 1e-5
LAMBDA_INIT = 0.8 - 0.6 * math.exp(-0.3 * 0)
LOG2E = math.log2(math.e)

LANES = 128
BF16_SUBLANES = 16
VMEM_LIMIT = 56 * 1024 * 1024
NT_DIMS = (((1,), (1,)), ((), ()))


def _params(sem, vmem=VMEM_LIMIT):
    return pltpu.CompilerParams(dimension_semantics=sem, vmem_limit_bytes=vmem)


def _rms(x, g, eps):
    return x * lax.rsqrt(jnp.mean(x * x, axis=-1, keepdims=True) + eps) * g


def _inproj_kernel(x_ref, g_ref, w_ref, o_ref):
    h = _rms(x_ref[...], g_ref[...], RMS_EPS).astype(BF16)
    o_ref[...] = jnp.dot(h, w_ref[...], preferred_element_type=F32).astype(o_ref.dtype)


def _inproj(x2, g, w, tm):
    m = x2.shape[0]
    return pl.pallas_call(
        _inproj_kernel,
        out_shape=jax.ShapeDtypeStruct((m, IN_COLS), BF16),
        grid=(m // tm,),
        in_specs=[pl.BlockSpec((tm, D_MODEL), lambda i: (i, 0)),
                  pl.BlockSpec((1, D_MODEL), lambda i: (0, 0)),
                  pl.BlockSpec((D_MODEL, IN_COLS), lambda i: (0, 0))],
        out_specs=pl.BlockSpec((tm, IN_COLS), lambda i: (i, 0)),
        compiler_params=_params(("parallel",)),
        name="inproj",
    )(x2, g, w)


def _t5_bucket(rel):
    nb = N_BUCKETS // 2
    max_exact = nb // 2
    ret = jnp.where(rel > 0, nb, 0)
    n = jnp.abs(rel)
    nf = jnp.maximum(n, 1).astype(F32)
    frac = jnp.log2(nf / max_exact) * ((nb - max_exact) / math.log2(MAX_DISTANCE / max_exact))
    large = max_exact + jnp.where(frac >= 0, jnp.floor(frac), jnp.ceil(frac)).astype(I32)
    large = jnp.minimum(large, nb - 1)
    return ret + jnp.where(n < max_exact, n, large)


def _bias_lookup(bucket, tbl_ref, col):
    out = jnp.zeros(bucket.shape, F32)
    for b in range(N_BUCKETS):
        out = jnp.where(bucket == b, tbl_ref[b, col], out)
    return out


def _toeplitz(row, tq, tk):
    width = tq + tk
    rb = jnp.broadcast_to(row, (tq, width))
    rolled = pltpu.roll(rb, width - tq + 1, 1, stride=1, stride_axis=0)
    return rolled[:, :tk]


def _da_kernel(tbl_ref, lq1_ref, lk1_ref, lq2_ref, lk2_ref, q_ref, k_ref, v_ref, g_ref,
               o_ref, qm_ref, m_ref, l_ref, acc_ref, bias_ref, *, t, seq):
    h = pl.program_id(1)
    qi = pl.program_id(2)
    nk = seq // t

    @pl.when(qi == 0)
    def _():
        for d in range(3):
            jj = lax.broadcasted_iota(I32, (1, 2 * t), 1)
            rel = jnp.clip((d - 1) * t + jj - (t - 1), -MAX_DISTANCE, MAX_DISTANCE)
            bias_ref[d] = _toeplitz(_bias_lookup(_t5_bucket(rel), tbl_ref, h) * LOG2E, t, t)

    c_left = tbl_ref[N_BUCKETS // 2 - 1, h] * LOG2E
    c_right = tbl_ref[N_BUCKETS - 1, h] * LOG2E

    q = q_ref[0]
    lane = lax.broadcasted_iota(I32, q.shape, 1)
    qm_ref[0] = jnp.where(lane < HEAD_DIM, q, jnp.zeros_like(q))
    qm_ref[1] = jnp.where(lane >= HEAD_DIM, q, jnp.zeros_like(q))
    m_ref[...] = jnp.full(m_ref.shape, -jnp.inf, F32)
    l_ref[...] = jnp.zeros(l_ref.shape, F32)
    acc_ref[...] = jnp.zeros(acc_ref.shape, F32)

    def step(tile, width, bias_idx):
        start = pl.multiple_of(tile * t, t)
        k = k_ref[0, pl.ds(start, width), :]
        v = v_ref[0, pl.ds(start, width), :]
        reps = width // LANES
        logits = [lax.dot_general(qm_ref[mp], k, NT_DIMS, preferred_element_type=F32) for mp in range(2)]
        for mp in range(2):
            s = logits[mp]
            if bias_idx is not None:
                s = s + jnp.concatenate([bias_ref[bias_idx + i] for i in range(width // t)], axis=1)
            m_prev = m_ref[mp]
            m_new = jnp.maximum(m_prev, jnp.max(s, axis=1, keepdims=True))
            alpha = jnp.exp2(m_prev - m_new)
            p = jnp.exp2(s - jnp.tile(m_new, (1, reps)))
            psum = p[:, :LANES]
            for r in range(1, reps):
                psum = psum + p[:, r * LANES:(r + 1) * LANES]
            l_ref[mp] = alpha * l_ref[mp] + psum
            acc_ref[mp] = alpha * acc_ref[mp] + jnp.dot(p.astype(BF16), v, preferred_element_type=F32)
            m_ref[mp] = m_new

    def far(first, count):
        def quad(j, c):
            step(first + 4 * j, 4 * t, None)
            return c

        def pair(j, c):
            step(first + (count // 4) * 4, 2 * t, None)
            return c

        def single(j, c):
            step(first + count - 1, t, None)
            return c

        lax.fori_loop(0, count // 4, quad, 0)
        lax.fori_loop(0, (count % 4) // 2, pair, 0)
        lax.fori_loop(0, count % 2, single, 0)

    def near_pair(j, c):
        step(first_near, 2 * t, first_near - qi + 1)
        return c

    def near_single(j, c):
        step(first_right - 1, t, first_right - qi)
        return c

    first_near = jnp.maximum(qi - 1, 0)
    first_right = jnp.minimum(qi + 2, nk)
    far(0, first_near)
    m_ref[...] += c_left
    n_near = first_right - first_near
    lax.fori_loop(0, n_near // 2, near_pair, 0)
    lax.fori_loop(0, n_near % 2, near_single, 0)
    m_ref[...] -= c_right
    far(first_right, nk - first_right)

    outs = [acc_ref[mp] / jnp.sum(l_ref[mp], axis=1, keepdims=True) for mp in range(2)]
    lam = (jnp.exp(jnp.sum(lq1_ref[...] * lk1_ref[...], axis=1, keepdims=True))
           - jnp.exp(jnp.sum(lq2_ref[...] * lk2_ref[...], axis=1, keepdims=True))
           + LAMBDA_INIT)
    o = outs[0] - lam * outs[1]
    o = _rms(o, g_ref[...], SUBLN_EPS) * (1.0 - LAMBDA_INIT)
    o_ref[0] = o.astype(o_ref.dtype)


def _diff_attention(proj3, tbl, lq1, lk1, lq2, lk2, subln_g, t):
    b, s, _ = proj3.shape
    assert t >= MAX_DISTANCE and s % t == 0
    vec = lambda n: pl.BlockSpec((1, n), lambda bi, hi, qi: (0, 0))
    kern = functools.partial(_da_kernel, t=t, seq=s)
    return pl.pallas_call(
        kern,
        out_shape=jax.ShapeDtypeStruct((b, s, DA_V), BF16),
        grid=(b, DA_HEADS, s // t),
        in_specs=[pl.BlockSpec(memory_space=pltpu.SMEM),
                  vec(HEAD_DIM), vec(HEAD_DIM), vec(HEAD_DIM), vec(HEAD_DIM),
                  pl.BlockSpec((1, t, LANES), lambda bi, hi, qi: (bi, qi, hi)),
                  pl.BlockSpec((1, s, LANES), lambda bi, hi, qi: (bi, 0, DA_HEADS + hi)),
                  pl.BlockSpec((1, s, LANES), lambda bi, hi, qi: (bi, 0, 2 * DA_HEADS + hi)),
                  vec(2 * HEAD_DIM)],
        out_specs=pl.BlockSpec((1, t, LANES), lambda bi, hi, qi: (bi, qi, hi)),
        scratch_shapes=[pltpu.VMEM((2, t, LANES), BF16), pltpu.VMEM((2, t, LANES), F32),
                        pltpu.VMEM((2, t, LANES), F32), pltpu.VMEM((2, t, 2 * HEAD_DIM), F32),
                        pltpu.VMEM((3, t, t), F32)],
        compiler_params=_params(("parallel", "parallel", "arbitrary")),
        name="diff_attention",
    )(tbl, lq1, lk1, lq2, lk2, proj3, proj3, proj3, subln_g)


def _wa_kernel(tbl_ref, sink_ref, q_ref, k_ref, v_ref, o_ref, bias_ref, *, tq, seq):
    q0 = pl.program_id(1) * tq
    win = tq + 2 * WINDOW
    ws = pl.multiple_of(jnp.clip(q0 - WINDOW, 0, seq - win), LANES)

    @pl.when(jnp.logical_and(pl.program_id(0) == 0, pl.program_id(1) == 0))
    def _():
        jj = lax.broadcasted_iota(I32, (1, win + tq), 1)
        for d in range(3):
            rel = jj - d * WINDOW - (tq - 1)
            valid = jnp.abs(rel) <= WINDOW
            bucket = _t5_bucket(jnp.clip(rel, -MAX_DISTANCE, MAX_DISTANCE))
            for hq in range(WA_Q_HEADS):
                row = jnp.where(valid, _bias_lookup(bucket, tbl_ref, DA_HEADS + hq) * LOG2E, -jnp.inf)
                bias_ref[d, hq] = _toeplitz(row, tq, win)

    variant = (q0 - ws) // WINDOW
    kw = k_ref[0, pl.ds(ws, win), :]
    vw = v_ref[0, pl.ds(ws, win), :]
    q = q_ref[0]
    low = lax.broadcasted_iota(I32, (tq, LANES), 1) < HEAD_DIM
    kdup = [jnp.concatenate([kw[:, g * HEAD_DIM:(g + 1) * HEAD_DIM]] * 2, axis=1) for g in range(2)]
    ones = jnp.ones((win, LANES), BF16)
    vext = [jnp.concatenate([vw[:, g * HEAD_DIM:(g + 1) * HEAD_DIM]] * 2 + [ones], axis=1) for g in range(2)]
    logits = []
    for hq in range(WA_Q_HEADS):
        qp = q[:, (hq // 2) * LANES:(hq // 2 + 1) * LANES]
        qm = jnp.where(low if hq % 2 == 0 else jnp.logical_not(low), qp, jnp.zeros_like(qp))
        logits.append(lax.dot_general(qm, kdup[hq // WA_REP], NT_DIMS, preferred_element_type=F32))
    outs = []
    for pair in range(WA_Q_HEADS // 2):
        g = (2 * pair) // WA_REP
        halves = []
        for half in range(2):
            hq = 2 * pair + half
            s = logits[hq] + bias_ref[variant, hq]
            sk = sink_ref[0, hq] * LOG2E
            m = jnp.maximum(jnp.full((tq, LANES), sk, F32), jnp.max(s, axis=1, keepdims=True))
            e = jnp.exp2(s - jnp.tile(m, (1, win // LANES))).astype(BF16)
            pv = jnp.dot(e, vext[g], preferred_element_type=F32)
            halves.append(pv[:, :LANES] / (pv[:, LANES:] + jnp.exp2(sk - m)))
        outs.append(jnp.where(low, halves[0], halves[1]))
    o_ref[0] = jnp.concatenate(outs, axis=1).astype(o_ref.dtype)


def _windowed_gqa(proj3, tbl, sink, tq):
    b, s, _ = proj3.shape
    kern = functools.partial(_wa_kernel, tq=tq, seq=s)
    kv_blk = (IN_COLS - 2 * LANES) // LANES
    return pl.pallas_call(
        kern,
        out_shape=jax.ShapeDtypeStruct((b, s, WA_Q), BF16),
        grid=(b, s // tq),
        in_specs=[pl.BlockSpec(memory_space=pltpu.SMEM),
                  pl.BlockSpec(memory_space=pltpu.SMEM),
                  pl.BlockSpec((1, tq, WA_Q), lambda bi, qi: (bi, qi, 3)),
                  pl.BlockSpec((1, s, LANES), lambda bi, qi: (bi, 0, kv_blk)),
                  pl.BlockSpec((1, s, LANES), lambda bi, qi: (bi, 0, kv_blk + 1))],
        out_specs=pl.BlockSpec((1, tq, WA_Q), lambda bi, qi: (bi, qi, 0)),
        scratch_shapes=[pltpu.VMEM((3, WA_Q_HEADS, tq, tq + 2 * WINDOW), F32)],
        compiler_params=_params(("arbitrary", "arbitrary")),
        name="windowed_gqa",
    )(tbl, sink, proj3, proj3, proj3)


def _oproj_kernel(a_ref, b_ref, x_ref, w1_ref, w2_ref, g_ref, wr_ref, x1_ref, h_ref, aff_ref):
    x1 = (x_ref[...] + jnp.dot(a_ref[...], w1_ref[...], preferred_element_type=F32)
          + jnp.dot(b_ref[...], w2_ref[...], preferred_element_type=F32))
    x1_ref[...] = x1
    h = _rms(x1, g_ref[...], RMS_EPS)
    h_hi = h.astype(BF16)
    h_ref[...] = h_hi
    h_lo = (h - h_hi.astype(F32)).astype(BF16)
    wr = wr_ref[...]
    w_hi = wr.astype(BF16)
    w_lo = (wr - w_hi.astype(F32)).astype(BF16)
    hh = jnp.dot(h_hi, jnp.concatenate([w_hi, w_lo], axis=1), preferred_element_type=F32)
    logits = (hh[:, :N_EXPERTS] + hh[:, N_EXPERTS:]
              + jnp.dot(h_lo, w_hi, preferred_element_type=F32))
    e = jnp.exp(logits - jnp.max(logits, axis=1, keepdims=True))
    aff_ref[...] = e / jnp.sum(e, axis=1, keepdims=True)


def _oproj(a, b, x2, w1, w2, g, wr, tm):
    m = x2.shape[0]
    row = lambda n: pl.BlockSpec((tm, n), lambda i: (i, 0))
    full = lambda r, c: pl.BlockSpec((r, c), lambda i: (0, 0))
    return pl.pallas_call(
        _oproj_kernel,
        out_shape=(jax.ShapeDtypeStruct((m, D_MODEL), F32),
                   jax.ShapeDtypeStruct((m, D_MODEL), BF16),
                   jax.ShapeDtypeStruct((m, N_EXPERTS), F32)),
        grid=(m // tm,),
        in_specs=[row(DA_V), row(WA_Q), row(D_MODEL), full(DA_V, D_MODEL), full(WA_Q, D_MODEL),
                  full(1, D_MODEL), full(D_MODEL, N_EXPERTS)],
        out_specs=(row(D_MODEL), row(D_MODEL), row(N_EXPERTS)),
        compiler_params=_params(("parallel",)),
        name="oproj_router",
    )(a, b, x2, w1, w2, g, wr)


def _cumsum_lanes(x):
    n = x.shape[1]
    lane = lax.broadcasted_iota(I32, x.shape, 1)
    shift = 1
    while shift < n:
        x = x + jnp.where(lane >= shift, pltpu.roll(x, shift, 1), 0)
        shift *= 2
    return x


def _select_kernel(aff_ref, slot_ref, excl_ref, *, cap):
    aff = aff_ref[...]

    def count(mask):
        return jnp.sum(mask.astype(F32), axis=1, keepdims=True)

    def body(i, prefix):
        cand = prefix | jnp.left_shift(jnp.int32(1), 30 - i)
        return jnp.where(count(aff >= pltpu.bitcast(cand, F32)) >= cap, cand, prefix)

    thr = lax.fori_loop(0, 31, body, jnp.zeros((aff.shape[0], 1), I32))
    above = aff >= pltpu.bitcast(thr + 1, F32)
    edge = jnp.logical_and(aff >= pltpu.bitcast(thr, F32), jnp.logical_not(above))
    need = cap - count(above)
    edge_i = edge.astype(I32)
    edge_excl = (_cumsum_lanes(edge_i) - edge_i).astype(F32)
    sel = above | (edge & (edge_excl < need))
    sel_i = sel.astype(I32)
    excl = _cumsum_lanes(sel_i) - sel_i
    slot_ref[...] = jnp.where(sel, excl, -1)
    excl_ref[...] = excl


def _select(aff_t, cap):
    shp = jax.ShapeDtypeStruct(aff_t.shape, I32)
    return pl.pallas_call(
        functools.partial(_select_kernel, cap=cap),
        out_shape=(shp, shp),
        compiler_params=_params(None),
        name="expert_select",
    )(aff_t)


def _gather_kernel(cnt_ref, x_ref, slot_ref, xg_hbm, obuf, sem, carry_ref, xo, xsem, *, nb, rw):
    b = pl.program_id(0)
    tb = x_ref.shape[0]
    cur = b % 2
    grp = BF16_SUBLANES
    groups = rw // grp

    def lo_of(e, blk):
        return cnt_ref[e * (nb + 1) + blk]

    def base_of(e, blk):
        return pl.multiple_of((lo_of(e, blk) // grp) * grp, grp)

    def out_copy(e, blk, slab):
        return pltpu.make_async_copy(obuf.at[slab, pl.ds(e * rw, rw), :],
                                     xg_hbm.at[e, pl.ds(base_of(e, blk), rw), :], sem.at[slab, e])

    def pack(rel):
        rows = lax.broadcasted_iota(I32, (rw, tb), 0)
        return jnp.where(rel == rows, 1.0, 0.0).astype(BF16)

    @pl.when(b == 0)
    def _():
        carry_ref[...] = jnp.zeros(carry_ref.shape, carry_ref.dtype)
        xo[...] = jnp.zeros(xo.shape, xo.dtype)
        for e in range(N_EXPERTS):
            cp = pltpu.make_async_copy(xo, xg_hbm.at[e, pl.ds(xg_hbm.shape[1] - rw, rw), :], xsem)
            cp.start()
            cp.wait()

    x = x_ref[...]
    onehot = jnp.concatenate([pack(slot_ref[e:e + 1, :] - base_of(e, b)) for e in range(N_EXPERTS)], axis=0)
    obuf[cur] = jnp.dot(onehot, x, preferred_element_type=F32).astype(BF16)
    for e in range(N_EXPERTS):
        obuf[cur, pl.ds(e * rw, grp), :] += carry_ref[e]
        g_hi = (lo_of(e, b + 1) - base_of(e, b)) // grp
        row0 = pl.multiple_of(e * rw + jnp.minimum(g_hi, groups - 1) * grp, grp)
        kept = obuf[cur, pl.ds(row0, grp), :]
        carry_ref[e] = jnp.where(g_hi < groups, kept, jnp.zeros_like(kept))

    @pl.when(b > 0)
    def _():
        for e in range(N_EXPERTS):
            out_copy(e, b - 1, 1 - cur).wait()

    for e in range(N_EXPERTS):
        out_copy(e, b, cur).start()

    for e in range(N_EXPERTS):
        base = base_of(e, b)
        hi = lo_of(e, b + 1)
        n_extra = (jnp.maximum(hi - base - rw, 0) + rw - 1) // rw

        def extra(w, c, e=e, base=base, hi=hi):
            first = pl.multiple_of(base + (w + 1) * rw, grp)
            xo[...] = jnp.dot(pack(slot_ref[e:e + 1, :] - first), x, preferred_element_type=F32).astype(BF16)
            cp = pltpu.make_async_copy(xo, xg_hbm.at[e, pl.ds(first, rw), :], xsem)
            cp.start()
            cp.wait()
            g_hi = (hi - first) // grp
            row0 = pl.multiple_of(jnp.clip(g_hi, 0, groups - 1) * grp, grp)
            carry_ref[e] = jnp.where(jnp.logical_and(g_hi >= 0, g_hi < groups),
                                     xo[pl.ds(row0, grp), :], carry_ref[e])
            return c

        lax.fori_loop(0, n_extra, extra, 0)

    @pl.when(b == nb - 1)
    def _():
        for e in range(N_EXPERTS):
            out_copy(e, b, cur).wait()


def _gather(cnt, h2, slot, cap, tb):
    m = h2.shape[0]
    nb = m // tb
    rw = (tb // 8 + max(tb // 16, 32) + BF16_SUBLANES) // BF16_SUBLANES * BF16_SUBLANES
    assert cap % BF16_SUBLANES == 0
    return pl.pallas_call(
        functools.partial(_gather_kernel, nb=nb, rw=rw),
        out_shape=jax.ShapeDtypeStruct((N_EXPERTS, cap + rw, D_MODEL), BF16),
        grid_spec=pltpu.PrefetchScalarGridSpec(
            num_scalar_prefetch=1,
            grid=(nb,),
            in_specs=[pl.BlockSpec((tb, D_MODEL), lambda b, c: (b, 0)),
                      pl.BlockSpec((N_EXPERTS, tb), lambda b, c: (0, b))],
            out_specs=pl.BlockSpec(memory_space=pl.ANY),
            scratch_shapes=[pltpu.VMEM((2, N_EXPERTS * rw, D_MODEL), BF16),
                            pltpu.SemaphoreType.DMA((2, N_EXPERTS)),
                            pltpu.VMEM((N_EXPERTS, BF16_SUBLANES, D_MODEL), BF16),
                            pltpu.VMEM((rw, D_MODEL), BF16),
                            pltpu.SemaphoreType.DMA(())]),
        compiler_params=_params(("arbitrary",)),
        name="moe_gather",
    )(cnt, h2, slot)


def _ffn_kernel(x_ref, wg_ref, wu_ref, wd_ref, y_ref, wg_s, wu_s, wd_s):
    @pl.when(pl.program_id(1) == 0)
    def _():
        wg_s[...] = wg_ref[0].astype(BF16)
        wu_s[...] = wu_ref[0].astype(BF16)
        wd_s[...] = wd_ref[0].astype(BF16)

    x = x_ref[0]
    g = jnp.dot(x, wg_s[...], preferred_element_type=F32)
    u = jnp.dot(x, wu_s[...], preferred_element_type=F32)
    h = (g * jax.nn.sigmoid(g) * u).astype(BF16)
    y_ref[0] = jnp.dot(h, wd_s[...], preferred_element_type=F32).astype(y_ref.dtype)


def _ffn(xg, wg, wu, wd, cap, tm):
    wspec = lambda r, c: pl.BlockSpec((1, r, c), lambda e, i: (e, 0, 0))
    return pl.pallas_call(
        _ffn_kernel,
        out_shape=jax.ShapeDtypeStruct((N_EXPERTS, cap, D_MODEL), BF16),
        grid=(N_EXPERTS, cap // tm),
        in_specs=[pl.BlockSpec((1, tm, D_MODEL), lambda e, i: (e, i, 0)),
                  wspec(D_MODEL, EXPERT_FF), wspec(D_MODEL, EXPERT_FF), wspec(EXPERT_FF, D_MODEL)],
        out_specs=pl.BlockSpec((1, tm, D_MODEL), lambda e, i: (e, i, 0)),
        scratch_shapes=[pltpu.VMEM((D_MODEL, EXPERT_FF), BF16), pltpu.VMEM((D_MODEL, EXPERT_FF), BF16),
                        pltpu.VMEM((EXPERT_FF, D_MODEL), BF16)],
        compiler_params=_params(("arbitrary", "arbitrary")),
        name="moe_ffn",
    )(xg, wg, wu, wd)


def _combine_kernel(cnt_ref, x1_ref, slot_ref, gate_ref, fn_ref, y_hbm, o_ref,
                    buf, sem, xbuf, xsem, acc_ref, *, nb, cap, rw):
    b = pl.program_id(0)
    tb = x1_ref.shape[0]
    xr = xbuf.shape[0]

    def window_start(e, blk):
        lo = cnt_ref[e * (nb + 1) + blk]
        return pl.multiple_of(jnp.minimum((lo // BF16_SUBLANES) * BF16_SUBLANES, cap - rw), BF16_SUBLANES)

    def window_copy(e, start, slab):
        return pltpu.make_async_copy(y_hbm.at[e, pl.ds(start, rw), :],
                                     buf.at[slab, pl.ds(e * rw, rw), :], sem.at[slab, e])

    def fetch(blk, slab):
        for e in range(N_EXPERTS):
            window_copy(e, window_start(e, blk), slab).start()

    cur = b % 2

    @pl.when(b == 0)
    def _():
        fetch(0, 0)

    @pl.when(b + 1 < nb)
    def _():
        fetch(b + 1, 1 - cur)

    starts = [window_start(e, b) for e in range(N_EXPERTS)]
    for e in range(N_EXPERTS):
        window_copy(e, starts[e], cur).wait()

    width = N_EXPERTS * rw
    ex = (lax.broadcasted_iota(I32, (N_EXPERTS, width), 0)
          == lax.broadcasted_iota(I32, (N_EXPERTS, width), 1) // rw).astype(BF16)
    lane16 = lax.broadcasted_iota(I32, (1, N_EXPERTS), 1)
    st_vec = jnp.zeros((1, N_EXPERTS), I32)
    for e in range(N_EXPERTS):
        st_vec = jnp.where(lane16 == e, starts[e], st_vec)
    rel = jnp.clip(slot_ref[...] - st_vec, -1, rw).astype(F32).astype(BF16)
    row = (lax.broadcasted_iota(I32, (tb, width), 1) % rw).astype(F32)
    hit = jnp.dot(rel, ex, preferred_element_type=F32) == row
    gate = gate_ref[...]
    g_hi = gate.astype(BF16)
    g_lo = (gate - g_hi.astype(F32)).astype(BF16)
    p_hi = jnp.where(hit, jnp.dot(g_hi, ex, preferred_element_type=F32), 0.0).astype(BF16)
    p_lo = jnp.where(hit, jnp.dot(g_lo, ex, preferred_element_type=F32), 0.0).astype(BF16)
    both = jnp.dot(jnp.concatenate([p_hi, p_lo], axis=0), buf[cur], preferred_element_type=F32)
    acc_ref[...] = x1_ref[...] + both[:tb] + both[tb:]

    xlane = lax.broadcasted_iota(I32, (tb, xr), 1)
    for e in range(N_EXPERTS):
        done = starts[e] + rw
        n_extra = (jnp.maximum(cnt_ref[e * (nb + 1) + b + 1] - done, 0) + xr - 1) // xr

        def extra(w, c, e=e, done=done):
            first = done + w * xr
            cst = pl.multiple_of(jnp.minimum(first, cap - xr), BF16_SUBLANES)
            cp = pltpu.make_async_copy(y_hbm.at[e, pl.ds(cst, xr), :], xbuf, xsem)
            cp.start()
            cp.wait()
            slot = slot_ref[:, e:e + 1]
            hit = jnp.logical_and(slot - cst == xlane, slot >= first)
            onehot = jnp.where(hit, 1.0, 0.0).astype(BF16)
            acc_ref[...] += gate_ref[:, e:e + 1] * jnp.dot(onehot, xbuf[...], preferred_element_type=F32)
            return c

        lax.fori_loop(0, n_extra, extra, 0)

    o_ref[...] = _rms(acc_ref[...], fn_ref[...], RMS_EPS)


def _combine(cnt, x1, slot_t, gate_t, fn, y, tb):
    m = x1.shape[0]
    nb = m // tb
    cap = y.shape[1]
    rw = LANES // 2
    assert cap % BF16_SUBLANES == 0 and cap >= LANES
    return pl.pallas_call(
        functools.partial(_combine_kernel, nb=nb, cap=cap, rw=rw),
        out_shape=jax.ShapeDtypeStruct((m, D_MODEL), F32),
        grid_spec=pltpu.PrefetchScalarGridSpec(
            num_scalar_prefetch=1,
            grid=(nb,),
            in_specs=[pl.BlockSpec((tb, D_MODEL), lambda b, c: (b, 0)),
                      pl.BlockSpec((tb, N_EXPERTS), lambda b, c: (b, 0)),
                      pl.BlockSpec((tb, N_EXPERTS), lambda b, c: (b, 0)),
                      pl.BlockSpec((1, D_MODEL), lambda b, c: (0, 0)),
                      pl.BlockSpec(memory_space=pl.ANY)],
            out_specs=pl.BlockSpec((tb, D_MODEL), lambda b, c: (b, 0)),
            scratch_shapes=[pltpu.VMEM((2, N_EXPERTS * rw, D_MODEL), BF16),
                            pltpu.SemaphoreType.DMA((2, N_EXPERTS)),
                            pltpu.VMEM((LANES, D_MODEL), BF16),
                            pltpu.SemaphoreType.DMA(()),
                            pltpu.VMEM((tb, D_MODEL), F32)]),
        compiler_params=_params(("arbitrary",)),
        name="moe_combine",
    )(cnt, x1, slot_t, gate_t, fn, y)


def _pick(n, prefs):
    for p in prefs:
        if n % p == 0:
            return p
    raise ValueError(f"no tile in {prefs} divides {n}")


def _trunk(x, tbl, sink, lam, subln_g, norm1, w_in, w_o1, w_o2, norm2, wr, wg, wu, wd, fn):
    bsz, s, d = x.shape
    m = bsz * s
    cap = CAPACITY_FACTOR * m // N_EXPERTS
    x2 = x.reshape(m, d)
    proj = _inproj(x2, norm1, w_in, _pick(m, (1024, 512, 256)))
    proj3 = proj.reshape(bsz, s, IN_COLS)
    out_a = _diff_attention(proj3, tbl, *lam, subln_g, _pick(s, (512, 256)))
    out_b = _windowed_gqa(proj3, tbl, sink, _pick(s, (256,)))
    x1, h2, aff = _oproj(out_a.reshape(m, DA_V), out_b.reshape(m, WA_Q), x2, w_o1, w_o2,
                         norm2, wr, _pick(m, (1024, 512, 256)))
    slot, excl = _select(aff.T, cap)
    def block_counts(tb):
        return jnp.concatenate([excl[:, ::tb], jnp.full((N_EXPERTS, 1), cap, I32)], axis=1).reshape(-1)

    tb_gather = _pick(m, (512, 256))
    tb_combine = 256
    xg = _gather(block_counts(tb_gather), h2, slot, cap, tb_gather)
    y = _ffn(xg, wg, wu, wd, cap, _pick(cap, (1024, 512, 256, 128)))
    out = _combine(block_counts(tb_combine), x1, slot.T, aff, fn, y, tb_combine)
    return out.reshape(bsz, s, d)


def kernel(x_prompt, x_sample, rel_bias, norm1, w_in, lam_q1, lam_k1, lam_q2, lam_k2, subln_g, sink,
           w_o, norm2, w_router, w_gate, w_up, w_down, final_norm):
    col = jnp.arange(IN_COLS)
    is_q = (col < DA_HEADS * 2 * HEAD_DIM) | ((col >= 3 * DA_V) & (col < 3 * DA_V + WA_Q))
    col_scale = jnp.where(is_q, HEAD_DIM ** -0.5 * LOG2E, 1.0).astype(F32)
    w_in_s = (w_in[0] * col_scale).astype(BF16)
    args = dict(
        tbl=rel_bias, sink=sink,
        lam=(lam_q1, lam_k1, lam_q2, lam_k2), subln_g=subln_g,
        norm1=norm1, w_in=w_in_s,
        w_o1=w_o[0, :DA_V].astype(BF16), w_o2=w_o[0, DA_V:].astype(BF16),
        norm2=norm2, wr=w_router[0],
        wg=w_gate[0], wu=w_up[0], wd=w_down[0],
        fn=final_norm.reshape(1, D_MODEL))
    return (_trunk(x_prompt, **args), _trunk(x_sample, **args))
```

```python
import functools
import math

import jax
import jax.numpy as jnp
from jax import lax
from jax.experimental import pallas as pl
from jax.experimental.pallas import tpu as pltpu

F32 = jnp.float32
BF16 = jnp.bfloat16
I32 = jnp.int32

D_MODEL = 1024
HEAD_DIM = 64
DA_HEADS = 4
WA_Q_HEADS = 8
WA_REP = 4
WINDOW = 128
IN_COLS = 2304
DA_V = 512
WA_Q = 512
N_BUCKETS = 32
MAX_DISTANCE = 128
N_EXPERTS = 16
EXPERT_FF = 1024
CAPACITY_FACTOR = 2
RMS_EPS = 1e-6
SUBLN_EPS = 1e-5
LAMBDA_INIT = 0.8 - 0.6 * math.exp(-0.3 * 0)
LOG2E = math.log2(math.e)

LANES = 128
D_ROUTED = D_MODEL + LANES
BF16_SUBLANES = 16
VMEM_LIMIT = 56 * 1024 * 1024
NT_DIMS = (((1,), (1,)), ((), ()))


def _params(sem, vmem=VMEM_LIMIT):
    return pltpu.CompilerParams(dimension_semantics=sem, vmem_limit_bytes=vmem)


def _rms(x, g, eps):
    return x * lax.rsqrt(jnp.mean(x * x, axis=-1, keepdims=True) + eps) * g


def _inproj_kernel(x_ref, g_ref, w_ref, o_ref):
    h = _rms(x_ref[...], g_ref[...], RMS_EPS).astype(BF16)
    o_ref[...] = jnp.dot(h, w_ref[...], preferred_element_type=F32).astype(o_ref.dtype)


def _inproj(x2, g, w, tm):
    m = x2.shape[0]
    return pl.pallas_call(
        _inproj_kernel,
        out_shape=jax.ShapeDtypeStruct((m, IN_COLS), BF16),
        grid=(m // tm,),
        in_specs=[pl.BlockSpec((tm, D_MODEL), lambda i: (i, 0)),
                  pl.BlockSpec((1, D_MODEL), lambda i: (0, 0)),
                  pl.BlockSpec((D_MODEL, IN_COLS), lambda i: (0, 0))],
        out_specs=pl.BlockSpec((tm, IN_COLS), lambda i: (i, 0)),
        compiler_params=_params(("parallel",)),
        name="inproj",
    )(x2, g, w)


def _t5_bucket(rel):
    nb = N_BUCKETS // 2
    max_exact = nb // 2
    ret = jnp.where(rel > 0, nb, 0)
    n = jnp.abs(rel)
    nf = jnp.maximum(n, 1).astype(F32)
    frac = jnp.log2(nf / max_exact) * ((nb - max_exact) / math.log2(MAX_DISTANCE / max_exact))
    large = max_exact + jnp.where(frac >= 0, jnp.floor(frac), jnp.ceil(frac)).astype(I32)
    large = jnp.minimum(large, nb - 1)
    return ret + jnp.where(n < max_exact, n, large)


def _bias_lookup(bucket, tbl_ref, col):
    out = jnp.zeros(bucket.shape, F32)
    for b in range(N_BUCKETS):
        out = jnp.where(bucket == b, tbl_ref[b, col], out)
    return out


def _toeplitz(row, tq, tk):
    width = tq + tk
    rb = jnp.broadcast_to(row, (tq, width))
    rolled = pltpu.roll(rb, width - tq + 1, 1, stride=1, stride_axis=0)
    return rolled[:, :tk]


def _da_kernel(tbl_ref, lq1_ref, lk1_ref, lq2_ref, lk2_ref, q_ref, k_ref, v_ref, g_ref,
               o_ref, qm_ref, m_ref, l_ref, acc_ref, bias_ref, *, t, seq):
    h = pl.program_id(1)
    qi = pl.program_id(2)
    nk = seq // t

    @pl.when(qi == 0)
    def _():
        for d in range(3):
            jj = lax.broadcasted_iota(I32, (1, 2 * t), 1)
            rel = jnp.clip((d - 1) * t + jj - (t - 1), -MAX_DISTANCE, MAX_DISTANCE)
            bias_ref[d] = _toeplitz(_bias_lookup(_t5_bucket(rel), tbl_ref, h) * LOG2E, t, t)

    c_left = tbl_ref[N_BUCKETS // 2 - 1, h] * LOG2E
    c_right = tbl_ref[N_BUCKETS - 1, h] * LOG2E

    q = q_ref[0]
    lane = lax.broadcasted_iota(I32, q.shape, 1)
    qm_ref[0] = jnp.where(lane < HEAD_DIM, q, jnp.zeros_like(q))
    qm_ref[1] = jnp.where(lane >= HEAD_DIM, q, jnp.zeros_like(q))
    m_ref[...] = jnp.full(m_ref.shape, -jnp.inf, F32)
    l_ref[...] = jnp.zeros(l_ref.shape, F32)
    acc_ref[...] = jnp.zeros(acc_ref.shape, F32)

    def step(tile, width, bias_idx):
        start = pl.multiple_of(tile * t, t)
        k = k_ref[0, pl.ds(start, width), :]
        v = v_ref[0, pl.ds(start, width), :]
        reps = width // LANES
        logits = [lax.dot_general(qm_ref[mp], k, NT_DIMS, preferred_element_type=F32) for mp in range(2)]
        for mp in range(2):
            s = logits[mp]
            if bias_idx is not None:
                s = s + jnp.concatenate([bias_ref[bias_idx + i] for i in range(width // t)], axis=1)
            m_prev = m_ref[mp]
            m_new = jnp.maximum(m_prev, jnp.max(s, axis=1, keepdims=True))
            alpha = jnp.exp2(m_prev - m_new)
            p = jnp.exp2(s - jnp.tile(m_new, (1, reps)))
            psum = p[:, :LANES]
            for r in range(1, reps):
                psum = psum + p[:, r * LANES:(r + 1) * LANES]
            l_ref[mp] = alpha * l_ref[mp] + psum
            acc_ref[mp] = alpha * acc_ref[mp] + jnp.dot(p.astype(BF16), v, preferred_element_type=F32)
            m_ref[mp] = m_new

    def far(first, count):
        def quad(j, c):
            step(first + 4 * j, 4 * t, None)
            return c

        def pair(j, c):
            step(first + (count // 4) * 4, 2 * t, None)
            return c

        def single(j, c):
            step(first + count - 1, t, None)
            return c

        lax.fori_loop(0, count // 4, quad, 0)
        lax.fori_loop(0, (count % 4) // 2, pair, 0)
        lax.fori_loop(0, count % 2, single, 0)

    def near_pair(j, c):
        step(first_near, 2 * t, first_near - qi + 1)
        return c

    def near_single(j, c):
        step(first_right - 1, t, first_right - qi)
        return c

    first_near = jnp.maximum(qi - 1, 0)
    first_right = jnp.minimum(qi + 2, nk)
    far(0, first_near)
    m_ref[...] += c_left
    n_near = first_right - first_near
    lax.fori_loop(0, n_near // 2, near_pair, 0)
    lax.fori_loop(0, n_near % 2, near_single, 0)
    m_ref[...] -= c_right
    far(first_right, nk - first_right)

    outs = [acc_ref[mp] / jnp.sum(l_ref[mp], axis=1, keepdims=True) for mp in range(2)]
    lam = (jnp.exp(jnp.sum(lq1_ref[...] * lk1_ref[...], axis=1, keepdims=True))
           - jnp.exp(jnp.sum(lq2_ref[...] * lk2_ref[...], axis=1, keepdims=True))
           + LAMBDA_INIT)
    o = outs[0] - lam * outs[1]
    o = _rms(o, g_ref[...], SUBLN_EPS) * (1.0 - LAMBDA_INIT)
    o_ref[0] = o.astype(o_ref.dtype)


def _diff_attention(proj3, tbl, lq1, lk1, lq2, lk2, subln_g, t):
    b, s, _ = proj3.shape
    assert t >= MAX_DISTANCE and s % t == 0
    vec = lambda n: pl.BlockSpec((1, n), lambda bi, hi, qi: (0, 0))
    kern = functools.partial(_da_kernel, t=t, seq=s)
    return pl.pallas_call(
        kern,
        out_shape=jax.ShapeDtypeStruct((b, s, DA_V), BF16),
        grid=(b, DA_HEADS, s // t),
        in_specs=[pl.BlockSpec(memory_space=pltpu.SMEM),
                  vec(HEAD_DIM), vec(HEAD_DIM), vec(HEAD_DIM), vec(HEAD_DIM),
                  pl.BlockSpec((1, t, LANES), lambda bi, hi, qi: (bi, qi, hi)),
                  pl.BlockSpec((1, s, LANES), lambda bi, hi, qi: (bi, 0, DA_HEADS + hi)),
                  pl.BlockSpec((1, s, LANES), lambda bi, hi, qi: (bi, 0, 2 * DA_HEADS + hi)),
                  vec(2 * HEAD_DIM)],
        out_specs=pl.BlockSpec((1, t, LANES), lambda bi, hi, qi: (bi, qi, hi)),
        scratch_shapes=[pltpu.VMEM((2, t, LANES), BF16), pltpu.VMEM((2, t, LANES), F32),
                        pltpu.VMEM((2, t, LANES), F32), pltpu.VMEM((2, t, 2 * HEAD_DIM), F32),
                        pltpu.VMEM((3, t, t), F32)],
        compiler_params=_params(("parallel", "parallel", "arbitrary")),
        name="diff_attention",
    )(tbl, lq1, lk1, lq2, lk2, proj3, proj3, proj3, subln_g)


def _wa_kernel(tbl_ref, sink_ref, q_ref, k_ref, v_ref, o_ref, bias_ref, *, tq, seq):
    q0 = pl.program_id(1) * tq
    win = tq + 2 * WINDOW
    ws = pl.multiple_of(jnp.clip(q0 - WINDOW, 0, seq - win), LANES)

    @pl.when(jnp.logical_and(pl.program_id(0) == 0, pl.program_id(1) == 0))
    def _():
        jj = lax.broadcasted_iota(I32, (1, win + tq), 1)
        for d in range(3):
            rel = jj - d * WINDOW - (tq - 1)
            valid = jnp.abs(rel) <= WINDOW
            bucket = _t5_bucket(jnp.clip(rel, -MAX_DISTANCE, MAX_DISTANCE))
            for hq in range(WA_Q_HEADS):
                row = jnp.where(valid, _bias_lookup(bucket, tbl_ref, DA_HEADS + hq) * LOG2E, -jnp.inf)
                bias_ref[d, hq] = _toeplitz(row, tq, win)

    variant = (q0 - ws) // WINDOW
    kw = k_ref[0, pl.ds(ws, win), :]
    vw = v_ref[0, pl.ds(ws, win), :]
    q = q_ref[0]
    low = lax.broadcasted_iota(I32, (tq, LANES), 1) < HEAD_DIM
    kdup = [jnp.concatenate([kw[:, g * HEAD_DIM:(g + 1) * HEAD_DIM]] * 2, axis=1) for g in range(2)]
    ones = jnp.ones((win, LANES), BF16)
    vext = [jnp.concatenate([vw[:, g * HEAD_DIM:(g + 1) * HEAD_DIM]] * 2 + [ones], axis=1) for g in range(2)]
    logits = []
    for hq in range(WA_Q_HEADS):
        qp = q[:, (hq // 2) * LANES:(hq // 2 + 1) * LANES]
        qm = jnp.where(low if hq % 2 == 0 else jnp.logical_not(low), qp, jnp.zeros_like(qp))
        logits.append(lax.dot_general(qm, kdup[hq // WA_REP], NT_DIMS, preferred_element_type=F32))
    outs = []
    for pair in range(WA_Q_HEADS // 2):
        g = (2 * pair) // WA_REP
        halves = []
        for half in range(2):
            hq = 2 * pair + half
            s = logits[hq] + bias_ref[variant, hq]
            sk = sink_ref[0, hq] * LOG2E
            m = jnp.maximum(jnp.full((tq, LANES), sk, F32), jnp.max(s, axis=1, keepdims=True))
            e = jnp.exp2(s - jnp.tile(m, (1, win // LANES))).astype(BF16)
            pv = jnp.dot(e, vext[g], preferred_element_type=F32)
            halves.append(pv[:, :LANES] / (pv[:, LANES:] + jnp.exp2(sk - m)))
        outs.append(jnp.where(low, halves[0], halves[1]))
    o_ref[0] = jnp.concatenate(outs, axis=1).astype(o_ref.dtype)


def _windowed_gqa(proj3, tbl, sink, tq):
    b, s, _ = proj3.shape
    kern = functools.partial(_wa_kernel, tq=tq, seq=s)
    kv_blk = (IN_COLS - 2 * LANES) // LANES
    return pl.pallas_call(
        kern,
        out_shape=jax.ShapeDtypeStruct((b, s, WA_Q), BF16),
        grid=(b, s // tq),
        in_specs=[pl.BlockSpec(memory_space=pltpu.SMEM),
                  pl.BlockSpec(memory_space=pltpu.SMEM),
                  pl.BlockSpec((1, tq, WA_Q), lambda bi, qi: (bi, qi, 3)),
                  pl.BlockSpec((1, s, LANES), lambda bi, qi: (bi, 0, kv_blk)),
                  pl.BlockSpec((1, s, LANES), lambda bi, qi: (bi, 0, kv_blk + 1))],
        out_specs=pl.BlockSpec((1, tq, WA_Q), lambda bi, qi: (bi, qi, 0)),
        scratch_shapes=[pltpu.VMEM((3, WA_Q_HEADS, tq, tq + 2 * WINDOW), F32)],
        compiler_params=_params(("arbitrary", "arbitrary")),
        name="windowed_gqa",
    )(tbl, sink, proj3, proj3, proj3)


def _oproj_kernel(a_ref, b_ref, x_ref, w1_ref, w2_ref, g_ref, wr_ref, x1_ref, h_ref, aff_ref):
    x1 = (x_ref[...] + jnp.dot(a_ref[...], w1_ref[...], preferred_element_type=F32)
          + jnp.dot(b_ref[...], w2_ref[...], preferred_element_type=F32))
    x1_ref[...] = x1
    h = _rms(x1, g_ref[...], RMS_EPS)
    h_hi = h.astype(BF16)
    h_ref[:, :D_MODEL] = h_hi
    h_lo = (h - h_hi.astype(F32)).astype(BF16)
    wr = wr_ref[...]
    w_hi = wr.astype(BF16)
    w_lo = (wr - w_hi.astype(F32)).astype(BF16)
    hh = jnp.dot(h_hi, jnp.concatenate([w_hi, w_lo], axis=1), preferred_element_type=F32)
    logits = (hh[:, :N_EXPERTS] + hh[:, N_EXPERTS:]
              + jnp.dot(h_lo, w_hi, preferred_element_type=F32))
    e = jnp.exp(logits - jnp.max(logits, axis=1, keepdims=True))
    aff = e / jnp.sum(e, axis=1, keepdims=True)
    aff_ref[...] = aff
    a_hi = aff.astype(BF16)
    a_lo = (aff - a_hi.astype(F32)).astype(BF16)
    pad = jnp.zeros((aff.shape[0], LANES - 2 * N_EXPERTS), BF16)
    h_ref[:, D_MODEL:] = jnp.concatenate([a_hi, a_lo, pad], axis=1)


def _oproj(a, b, x2, w1, w2, g, wr, tm):
    m = x2.shape[0]
    row = lambda n: pl.BlockSpec((tm, n), lambda i: (i, 0))
    full = lambda r, c: pl.BlockSpec((r, c), lambda i: (0, 0))
    return pl.pallas_call(
        _oproj_kernel,
        out_shape=(jax.ShapeDtypeStruct((m, D_MODEL), F32),
                   jax.ShapeDtypeStruct((m, D_ROUTED), BF16),
                   jax.ShapeDtypeStruct((m, N_EXPERTS), F32)),
        grid=(m // tm,),
        in_specs=[row(DA_V), row(WA_Q), row(D_MODEL), full(DA_V, D_MODEL), full(WA_Q, D_MODEL),
                  full(1, D_MODEL), full(D_MODEL, N_EXPERTS)],
        out_specs=(row(D_MODEL), row(D_ROUTED), row(N_EXPERTS)),
        compiler_params=_params(("parallel",)),
        name="oproj_router",
    )(a, b, x2, w1, w2, g, wr)


def _cumsum_lanes(x):
    n = x.shape[1]
    lane = lax.broadcasted_iota(I32, x.shape, 1)
    shift = 1
    while shift < n:
        x = x + jnp.where(lane >= shift, pltpu.roll(x, shift, 1), 0)
        shift *= 2
    return x


def _select_kernel(aff_ref, slot_ref, excl_ref, *, cap):
    aff = aff_ref[...]

    def count(mask):
        return jnp.sum(mask.astype(F32), axis=1, keepdims=True)

    def body(i, prefix):
        cand = prefix | jnp.left_shift(jnp.int32(1), 30 - i)
        return jnp.where(count(aff >= pltpu.bitcast(cand, F32)) >= cap, cand, prefix)

    thr = lax.fori_loop(0, 31, body, jnp.zeros((aff.shape[0], 1), I32))
    above = aff >= pltpu.bitcast(thr + 1, F32)
    edge = jnp.logical_and(aff >= pltpu.bitcast(thr, F32), jnp.logical_not(above))
    need = cap - count(above)
    edge_i = edge.astype(I32)
    edge_excl = (_cumsum_lanes(edge_i) - edge_i).astype(F32)
    sel = above | (edge & (edge_excl < need))
    sel_i = sel.astype(I32)
    excl = _cumsum_lanes(sel_i) - sel_i
    slot_ref[...] = jnp.where(sel, excl, -1)
    excl_ref[...] = excl


def _select(aff_t, cap):
    shp = jax.ShapeDtypeStruct(aff_t.shape, I32)
    return pl.pallas_call(
        functools.partial(_select_kernel, cap=cap),
        out_shape=(shp, shp),
        compiler_params=_params(None),
        name="expert_select",
    )(aff_t)


def _gather_kernel(cnt_ref, x_ref, slot_ref, xg_hbm, obuf, sem, carry_ref, xo, xsem, *, nb, rw):
    b = pl.program_id(0)
    tb = x_ref.shape[0]
    cur = b % 2
    grp = BF16_SUBLANES
    groups = rw // grp

    def lo_of(e, blk):
        return cnt_ref[e * (nb + 1) + blk]

    def base_of(e, blk):
        return pl.multiple_of((lo_of(e, blk) // grp) * grp, grp)

    def out_copy(e, blk, slab):
        return pltpu.make_async_copy(obuf.at[slab, pl.ds(e * rw, rw), :],
                                     xg_hbm.at[e, pl.ds(base_of(e, blk), rw), :], sem.at[slab, e])

    def pack(rel):
        rows = lax.broadcasted_iota(I32, (rw, tb), 0)
        return jnp.where(rel == rows, 1.0, 0.0).astype(BF16)

    @pl.when(b == 0)
    def _():
        carry_ref[...] = jnp.zeros(carry_ref.shape, carry_ref.dtype)
        xo[...] = jnp.zeros(xo.shape, xo.dtype)
        for e in range(N_EXPERTS):
            cp = pltpu.make_async_copy(xo, xg_hbm.at[e, pl.ds(xg_hbm.shape[1] - rw, rw), :], xsem)
            cp.start()
            cp.wait()

    x = x_ref[...]
    onehot = jnp.concatenate([pack(slot_ref[e:e + 1, :] - base_of(e, b)) for e in range(N_EXPERTS)], axis=0)
    obuf[cur] = jnp.dot(onehot, x, preferred_element_type=F32).astype(BF16)
    for e in range(N_EXPERTS):
        obuf[cur, pl.ds(e * rw, grp), :] += carry_ref[e]
        g_hi = (lo_of(e, b + 1) - base_of(e, b)) // grp
        row0 = pl.multiple_of(e * rw + jnp.minimum(g_hi, groups - 1) * grp, grp)
        kept = obuf[cur, pl.ds(row0, grp), :]
        carry_ref[e] = jnp.where(g_hi < groups, kept, jnp.zeros_like(kept))

    @pl.when(b > 0)
    def _():
        for e in range(N_EXPERTS):
            out_copy(e, b - 1, 1 - cur).wait()

    for e in range(N_EXPERTS):
        out_copy(e, b, cur).start()

    for e in range(N_EXPERTS):
        base = base_of(e, b)
        hi = lo_of(e, b + 1)
        n_extra = (jnp.maximum(hi - base - rw, 0) + rw - 1) // rw

        def extra(w, c, e=e, base=base, hi=hi):
            first = pl.multiple_of(base + (w + 1) * rw, grp)
            xo[...] = jnp.dot(pack(slot_ref[e:e + 1, :] - first), x, preferred_element_type=F32).astype(BF16)
            cp = pltpu.make_async_copy(xo, xg_hbm.at[e, pl.ds(first, rw), :], xsem)
            cp.start()
            cp.wait()
            g_hi = (hi - first) // grp
            row0 = pl.multiple_of(jnp.clip(g_hi, 0, groups - 1) * grp, grp)
            carry_ref[e] = jnp.where(jnp.logical_and(g_hi >= 0, g_hi < groups),
                                     xo[pl.ds(row0, grp), :], carry_ref[e])
            return c

        lax.fori_loop(0, n_extra, extra, 0)

    @pl.when(b == nb - 1)
    def _():
        for e in range(N_EXPERTS):
            out_copy(e, b, cur).wait()


def _gather(cnt, h2, slot, cap, tb):
    m, width = h2.shape
    nb = m // tb
    rw = (tb // 8 + max(tb // 16, 32) + BF16_SUBLANES) // BF16_SUBLANES * BF16_SUBLANES
    assert cap % BF16_SUBLANES == 0
    return pl.pallas_call(
        functools.partial(_gather_kernel, nb=nb, rw=rw),
        out_shape=jax.ShapeDtypeStruct((N_EXPERTS, cap + rw, width), BF16),
        grid_spec=pltpu.PrefetchScalarGridSpec(
            num_scalar_prefetch=1,
            grid=(nb,),
            in_specs=[pl.BlockSpec((tb, width), lambda b, c: (b, 0)),
                      pl.BlockSpec((N_EXPERTS, tb), lambda b, c: (0, b))],
            out_specs=pl.BlockSpec(memory_space=pl.ANY),
            scratch_shapes=[pltpu.VMEM((2, N_EXPERTS * rw, width), BF16),
                            pltpu.SemaphoreType.DMA((2, N_EXPERTS)),
                            pltpu.VMEM((N_EXPERTS, BF16_SUBLANES, width), BF16),
                            pltpu.VMEM((rw, width), BF16),
                            pltpu.SemaphoreType.DMA(())]),
        compiler_params=_params(("arbitrary",)),
        name="moe_gather",
    )(cnt, h2, slot)


def _ffn_kernel(x_ref, wg_ref, wu_ref, wd_ref, y_ref, wg_s, wu_s, wd_s):
    @pl.when(pl.program_id(1) == 0)
    def _():
        wg_s[...] = wg_ref[0].astype(BF16)
        wu_s[...] = wu_ref[0].astype(BF16)
        wd_s[...] = wd_ref[0].astype(BF16)

    x = x_ref[0, :, :D_MODEL]
    tail = x_ref[0, :, D_MODEL:].astype(F32)
    lane = lax.broadcasted_iota(I32, tail.shape, 1)
    e = pl.program_id(0)
    mine = jnp.logical_or(lane == e, lane == e + N_EXPERTS)
    gate = jnp.sum(jnp.where(mine, tail, 0.0), axis=1, keepdims=True)
    g = jnp.dot(x, wg_s[...], preferred_element_type=F32)
    u = jnp.dot(x, wu_s[...], preferred_element_type=F32)
    h = (g * jax.nn.sigmoid(g) * u).astype(BF16)
    y_ref[0] = (jnp.dot(h, wd_s[...], preferred_element_type=F32) * gate).astype(y_ref.dtype)


def _ffn(xg, wg, wu, wd, cap, tm):
    wspec = lambda r, c: pl.BlockSpec((1, r, c), lambda e, i: (e, 0, 0))
    return pl.pallas_call(
        _ffn_kernel,
        out_shape=jax.ShapeDtypeStruct((N_EXPERTS, cap, D_MODEL), BF16),
        grid=(N_EXPERTS, cap // tm),
        in_specs=[pl.BlockSpec((1, tm, D_ROUTED), lambda e, i: (e, i, 0)),
                  wspec(D_MODEL, EXPERT_FF), wspec(D_MODEL, EXPERT_FF), wspec(EXPERT_FF, D_MODEL)],
        out_specs=pl.BlockSpec((1, tm, D_MODEL), lambda e, i: (e, i, 0)),
        scratch_shapes=[pltpu.VMEM((D_MODEL, EXPERT_FF), BF16), pltpu.VMEM((D_MODEL, EXPERT_FF), BF16),
                        pltpu.VMEM((EXPERT_FF, D_MODEL), BF16)],
        compiler_params=_params(("arbitrary", "arbitrary")),
        name="moe_ffn",
    )(xg, wg, wu, wd)


def _combine_kernel(cnt_ref, x1_ref, slot_ref, fn_ref, y_hbm, o_ref,
                    buf, sem, xbuf, xsem, acc_ref, *, nb, cap, rw):
    b = pl.program_id(0)
    tb = x1_ref.shape[0]
    xr = xbuf.shape[0]

    def window_start(e, blk):
        lo = cnt_ref[e * (nb + 1) + blk]
        return pl.multiple_of(jnp.minimum((lo // BF16_SUBLANES) * BF16_SUBLANES, cap - rw), BF16_SUBLANES)

    def window_copy(e, start, slab):
        return pltpu.make_async_copy(y_hbm.at[e, pl.ds(start, rw), :],
                                     buf.at[slab, pl.ds(e * rw, rw), :], sem.at[slab, e])

    def fetch(blk, slab):
        for e in range(N_EXPERTS):
            window_copy(e, window_start(e, blk), slab).start()

    cur = b % 2

    @pl.when(b == 0)
    def _():
        fetch(0, 0)

    @pl.when(b + 1 < nb)
    def _():
        fetch(b + 1, 1 - cur)

    starts = [window_start(e, b) for e in range(N_EXPERTS)]
    for e in range(N_EXPERTS):
        window_copy(e, starts[e], cur).wait()

    width = N_EXPERTS * rw
    ex = (lax.broadcasted_iota(I32, (N_EXPERTS, width), 0)
          == lax.broadcasted_iota(I32, (N_EXPERTS, width), 1) // rw).astype(BF16)
    lane16 = lax.broadcasted_iota(I32, (1, N_EXPERTS), 1)
    st_vec = jnp.zeros((1, N_EXPERTS), I32)
    for e in range(N_EXPERTS):
        st_vec = jnp.where(lane16 == e, starts[e], st_vec)
    rel = jnp.clip(slot_ref[...] - st_vec, -1, rw).astype(F32).astype(BF16)
    row = (lax.broadcasted_iota(I32, (tb, width), 1) % rw).astype(F32)
    hit = jnp.dot(rel, ex, preferred_element_type=F32) == row
    onehot = jnp.where(hit, 1.0, 0.0).astype(BF16)
    acc_ref[...] = x1_ref[...] + jnp.dot(onehot, buf[cur], preferred_element_type=F32)

    xlane = lax.broadcasted_iota(I32, (tb, xr), 1)
    for e in range(N_EXPERTS):
        done = starts[e] + rw
        n_extra = (jnp.maximum(cnt_ref[e * (nb + 1) + b + 1] - done, 0) + xr - 1) // xr

        def extra(w, c, e=e, done=done):
            first = done + w * xr
            cst = pl.multiple_of(jnp.minimum(first, cap - xr), BF16_SUBLANES)
            cp = pltpu.make_async_copy(y_hbm.at[e, pl.ds(cst, xr), :], xbuf, xsem)
            cp.start()
            cp.wait()
            slot = slot_ref[:, e:e + 1]
            hit = jnp.logical_and(slot - cst == xlane, slot >= first)
            onehot = jnp.where(hit, 1.0, 0.0).astype(BF16)
            acc_ref[...] += jnp.dot(onehot, xbuf[...], preferred_element_type=F32)
            return c

        lax.fori_loop(0, n_extra, extra, 0)

    o_ref[...] = _rms(acc_ref[...], fn_ref[...], RMS_EPS)


def _combine(cnt, x1, slot_t, fn, y, tb):
    m = x1.shape[0]
    nb = m // tb
    cap = y.shape[1]
    rw = LANES // 2
    assert cap % BF16_SUBLANES == 0 and cap >= LANES
    return pl.pallas_call(
        functools.partial(_combine_kernel, nb=nb, cap=cap, rw=rw),
        out_shape=jax.ShapeDtypeStruct((m, D_MODEL), F32),
        grid_spec=pltpu.PrefetchScalarGridSpec(
            num_scalar_prefetch=1,
            grid=(nb,),
            in_specs=[pl.BlockSpec((tb, D_MODEL), lambda b, c: (b, 0)),
                      pl.BlockSpec((tb, N_EXPERTS), lambda b, c: (b, 0)),
                      pl.BlockSpec((1, D_MODEL), lambda b, c: (0, 0)),
                      pl.BlockSpec(memory_space=pl.ANY)],
            out_specs=pl.BlockSpec((tb, D_MODEL), lambda b, c: (b, 0)),
            scratch_shapes=[pltpu.VMEM((2, N_EXPERTS * rw, D_MODEL), BF16),
                            pltpu.SemaphoreType.DMA((2, N_EXPERTS)),
                            pltpu.VMEM((LANES, D_MODEL), BF16),
                            pltpu.SemaphoreType.DMA(()),
                            pltpu.VMEM((tb, D_MODEL), F32)]),
        compiler_params=_params(("arbitrary",)),
        name="moe_combine",
    )(cnt, x1, slot_t, fn, y)


def _pick(n, prefs):
    for p in prefs:
        if n % p == 0:
            return p
    raise ValueError(f"no tile in {prefs} divides {n}")


def _trunk(x, tbl, sink, lam, subln_g, norm1, w_in, w_o1, w_o2, norm2, wr, wg, wu, wd, fn):
    bsz, s, d = x.shape
    m = bsz * s
    cap = CAPACITY_FACTOR * m // N_EXPERTS
    x2 = x.reshape(m, d)
    proj = _inproj(x2, norm1, w_in, _pick(m, (1024, 512, 256)))
    proj3 = proj.reshape(bsz, s, IN_COLS)
    out_a = _diff_attention(proj3, tbl, *lam, subln_g, _pick(s, (512, 256)))
    out_b = _windowed_gqa(proj3, tbl, sink, _pick(s, (256,)))
    x1, h2, aff = _oproj(out_a.reshape(m, DA_V), out_b.reshape(m, WA_Q), x2, w_o1, w_o2,
                         norm2, wr, _pick(m, (1024, 512, 256)))
    slot, excl = _select(aff.T, cap)
    def block_counts(tb):
        return jnp.concatenate([excl[:, ::tb], jnp.full((N_EXPERTS, 1), cap, I32)], axis=1).reshape(-1)

    tb_gather = _pick(m, (512, 256))
    tb_combine = 256
    xg = _gather(block_counts(tb_gather), h2, slot, cap, tb_gather)
    y = _ffn(xg, wg, wu, wd, cap, _pick(cap, (1024, 512, 256, 128)))
    out = _combine(block_counts(tb_combine), x1, slot.T, fn, y, tb_combine)
    return out.reshape(bsz, s, d)


def kernel(x_prompt, x_sample, rel_bias, norm1, w_in, lam_q1, lam_k1, lam_q2, lam_k2, subln_g, sink,
           w_o, norm2, w_router, w_gate, w_up, w_down, final_norm):
    col = jnp.arange(IN_COLS)
    is_q = (col < DA_HEADS * 2 * HEAD_DIM) | ((col >= 3 * DA_V) & (col < 3 * DA_V + WA_Q))
    col_scale = jnp.where(is_q, HEAD_DIM ** -0.5 * LOG2E, 1.0).astype(F32)
    w_in_s = (w_in[0] * col_scale).astype(BF16)
    args = dict(
        tbl=rel_bias, sink=sink,
        lam=(lam_q1, lam_k1, lam_q2, lam_k2), subln_g=subln_g,
        norm1=norm1, w_in=w_in_s,
        w_o1=w_o[0, :DA_V].astype(BF16), w_o2=w_o[0, DA_V:].astype(BF16),
        norm2=norm2, wr=w_router[0],
        wg=w_gate[0], wu=w_up[0], wd=w_down[0],
        fn=final_norm.reshape(1, D_MODEL))
    return (_trunk(x_prompt, **args), _trunk(x_sample, **args))
```

```python
import functools
import math

import jax
import jax.numpy as jnp
from jax import lax
from jax.experimental import pallas as pl
from jax.experimental.pallas import tpu as pltpu

F32 = jnp.float32
BF16 = jnp.bfloat16
I32 = jnp.int32

D_MODEL = 1024
HEAD_DIM = 64
DA_HEADS = 4
WA_Q_HEADS = 8
WA_REP = 4
WINDOW = 128
IN_COLS = 2304
DA_V = 512
WA_Q = 512
N_BUCKETS = 32
MAX_DISTANCE = 128
N_EXPERTS = 16
EXPERT_FF = 1024
CAPACITY_FACTOR = 2
RMS_EPS = 1e-6
SUBLN_EPS = 1e-5
LAMBDA_INIT = 0.8 - 0.6 * math.exp(-0.3 * 0)
LOG2E = math.log2(math.e)

LANES = 128
D_ROUTED = D_MODEL + LANES
BF16_SUBLANES = 16
VMEM_LIMIT = 56 * 1024 * 1024
NT_DIMS = (((1,), (1,)), ((), ()))


def _params(sem, vmem=VMEM_LIMIT):
    return pltpu.CompilerParams(dimension_semantics=sem, vmem_limit_bytes=vmem)


def _rms(x, g, eps):
    return x * lax.rsqrt(jnp.mean(x * x, axis=-1, keepdims=True) + eps) * g


def _inproj_kernel(x_ref, g_ref, w_ref, o_ref):
    h = _rms(x_ref[...], g_ref[...], RMS_EPS).astype(BF16)
    o_ref[...] = jnp.dot(h, w_ref[...], preferred_element_type=F32).astype(o_ref.dtype)


def _inproj(x2, g, w, tm):
    m = x2.shape[0]
    return pl.pallas_call(
        _inproj_kernel,
        out_shape=jax.ShapeDtypeStruct((m, IN_COLS), BF16),
        grid=(m // tm,),
        in_specs=[pl.BlockSpec((tm, D_MODEL), lambda i: (i, 0)),
                  pl.BlockSpec((1, D_MODEL), lambda i: (0, 0)),
                  pl.BlockSpec((D_MODEL, IN_COLS), lambda i: (0, 0))],
        out_specs=pl.BlockSpec((tm, IN_COLS), lambda i: (i, 0)),
        compiler_params=_params(("parallel",)),
        name="inproj",
    )(x2, g, w)


def _t5_bucket(rel):
    nb = N_BUCKETS // 2
    max_exact = nb // 2
    ret = jnp.where(rel > 0, nb, 0)
    n = jnp.abs(rel)
    nf = jnp.maximum(n, 1).astype(F32)
    frac = jnp.log2(nf / max_exact) * ((nb - max_exact) / math.log2(MAX_DISTANCE / max_exact))
    large = max_exact + jnp.where(frac >= 0, jnp.floor(frac), jnp.ceil(frac)).astype(I32)
    large = jnp.minimum(large, nb - 1)
    return ret + jnp.where(n < max_exact, n, large)


def _bias_lookup(bucket, tbl_ref, col):
    out = jnp.zeros(bucket.shape, F32)
    for b in range(N_BUCKETS):
        out = jnp.where(bucket == b, tbl_ref[b, col], out)
    return out


def _toeplitz(row, tq, tk):
    width = tq + tk
    rb = jnp.broadcast_to(row, (tq, width))
    rolled = pltpu.roll(rb, width - tq + 1, 1, stride=1, stride_axis=0)
    return rolled[:, :tk]


def _da_kernel(tbl_ref, lq1_ref, lk1_ref, lq2_ref, lk2_ref, q_ref, k_ref, v_ref, g_ref,
               o_ref, qm_ref, m_ref, l_ref, acc_ref, bias_ref, vext_ref, *, t, seq):
    h = pl.program_id(1)
    qi = pl.program_id(2)
    nk = seq // t

    @pl.when(qi == 0)
    def _():
        vext_ref[:, :LANES] = v_ref[0]
        vext_ref[:, LANES:] = jnp.ones((seq, LANES), BF16)
        for d in range(3):
            jj = lax.broadcasted_iota(I32, (1, 2 * t), 1)
            rel = jnp.clip((d - 1) * t + jj - (t - 1), -MAX_DISTANCE, MAX_DISTANCE)
            bias_ref[d] = _toeplitz(_bias_lookup(_t5_bucket(rel), tbl_ref, h) * LOG2E, t, t)

    c_left = tbl_ref[N_BUCKETS // 2 - 1, h] * LOG2E
    c_right = tbl_ref[N_BUCKETS - 1, h] * LOG2E

    q = q_ref[0]
    lane = lax.broadcasted_iota(I32, q.shape, 1)
    qm_ref[0] = jnp.where(lane < HEAD_DIM, q, jnp.zeros_like(q))
    qm_ref[1] = jnp.where(lane >= HEAD_DIM, q, jnp.zeros_like(q))
    m_ref[...] = jnp.full(m_ref.shape, -jnp.inf, F32)
    l_ref[...] = jnp.zeros(l_ref.shape, F32)
    acc_ref[...] = jnp.zeros(acc_ref.shape, F32)

    def step(tile, width, bias_idx, mxu_sums=False):
        start = pl.multiple_of(tile * t, t)
        k = k_ref[0, pl.ds(start, width), :]
        v = vext_ref[pl.ds(start, width), :] if mxu_sums else v_ref[0, pl.ds(start, width), :]
        reps = width // LANES
        logits = [lax.dot_general(qm_ref[mp], k, NT_DIMS, preferred_element_type=F32) for mp in range(2)]
        for mp in range(2):
            s = logits[mp]
            if bias_idx is not None:
                s = s + jnp.concatenate([bias_ref[bias_idx + i] for i in range(width // t)], axis=1)
            m_prev = m_ref[mp]
            m_new = jnp.maximum(m_prev, jnp.max(s, axis=1, keepdims=True))
            alpha = jnp.exp2(m_prev - m_new)
            p = jnp.exp2(s - jnp.tile(m_new, (1, reps)))
            pv = jnp.dot(p.astype(BF16), v, preferred_element_type=F32)
            if mxu_sums:
                psum = pv[:, LANES:] * (1.0 / LANES)
                pv = pv[:, :LANES]
            else:
                psum = p[:, :LANES]
                for r in range(1, reps):
                    psum = psum + p[:, r * LANES:(r + 1) * LANES]
            l_ref[mp] = alpha * l_ref[mp] + psum
            acc_ref[mp] = alpha * acc_ref[mp] + pv
            m_ref[mp] = m_new

    def far(first, count):
        def quad(j, c):
            step(first + 4 * j, 4 * t, None, mxu_sums=True)
            return c

        def pair(j, c):
            step(first + (count // 4) * 4, 2 * t, None, mxu_sums=True)
            return c

        def single(j, c):
            step(first + count - 1, t, None)
            return c

        lax.fori_loop(0, count // 4, quad, 0)
        lax.fori_loop(0, (count % 4) // 2, pair, 0)
        lax.fori_loop(0, count % 2, single, 0)

    def near_pair(j, c):
        step(first_near, 2 * t, first_near - qi + 1)
        return c

    def near_single(j, c):
        step(first_right - 1, t, first_right - qi)
        return c

    first_near = jnp.maximum(qi - 1, 0)
    first_right = jnp.minimum(qi + 2, nk)
    far(0, first_near)
    m_ref[...] += c_left
    n_near = first_right - first_near
    lax.fori_loop(0, n_near // 2, near_pair, 0)
    lax.fori_loop(0, n_near % 2, near_single, 0)
    m_ref[...] -= c_right
    far(first_right, nk - first_right)

    outs = [acc_ref[mp] / jnp.sum(l_ref[mp], axis=1, keepdims=True) for mp in range(2)]
    lam = (jnp.exp(jnp.sum(lq1_ref[...] * lk1_ref[...], axis=1, keepdims=True))
           - jnp.exp(jnp.sum(lq2_ref[...] * lk2_ref[...], axis=1, keepdims=True))
           + LAMBDA_INIT)
    o = outs[0] - lam * outs[1]
    o = _rms(o, g_ref[...], SUBLN_EPS) * (1.0 - LAMBDA_INIT)
    o_ref[0] = o.astype(o_ref.dtype)


def _diff_attention(proj3, tbl, lq1, lk1, lq2, lk2, subln_g, t):
    b, s, _ = proj3.shape
    assert t >= MAX_DISTANCE and s % t == 0
    vec = lambda n: pl.BlockSpec((1, n), lambda bi, hi, qi: (0, 0))
    kern = functools.partial(_da_kernel, t=t, seq=s)
    return pl.pallas_call(
        kern,
        out_shape=jax.ShapeDtypeStruct((b, s, DA_V), BF16),
        grid=(b, DA_HEADS, s // t),
        in_specs=[pl.BlockSpec(memory_space=pltpu.SMEM),
                  vec(HEAD_DIM), vec(HEAD_DIM), vec(HEAD_DIM), vec(HEAD_DIM),
                  pl.BlockSpec((1, t, LANES), lambda bi, hi, qi: (bi, qi, hi)),
                  pl.BlockSpec((1, s, LANES), lambda bi, hi, qi: (bi, 0, DA_HEADS + hi)),
                  pl.BlockSpec((1, s, LANES), lambda bi, hi, qi: (bi, 0, 2 * DA_HEADS + hi)),
                  vec(2 * HEAD_DIM)],
        out_specs=pl.BlockSpec((1, t, LANES), lambda bi, hi, qi: (bi, qi, hi)),
        scratch_shapes=[pltpu.VMEM((2, t, LANES), BF16), pltpu.VMEM((2, t, LANES), F32),
                        pltpu.VMEM((2, t, LANES), F32), pltpu.VMEM((2, t, 2 * HEAD_DIM), F32),
                        pltpu.VMEM((3, t, t), F32), pltpu.VMEM((s, 2 * LANES), BF16)],
        compiler_params=_params(("parallel", "parallel", "arbitrary")),
        name="diff_attention",
    )(tbl, lq1, lk1, lq2, lk2, proj3, proj3, proj3, subln_g)


def _wa_kernel(tbl_ref, sink_ref, q_ref, k_ref, v_ref, o_ref, bias_ref, *, tq, seq):
    q0 = pl.program_id(1) * tq
    win = tq + 2 * WINDOW
    ws = pl.multiple_of(jnp.clip(q0 - WINDOW, 0, seq - win), LANES)

    @pl.when(jnp.logical_and(pl.program_id(0) == 0, pl.program_id(1) == 0))
    def _():
        jj = lax.broadcasted_iota(I32, (1, win + tq), 1)
        for d in range(3):
            rel = jj - d * WINDOW - (tq - 1)
            valid = jnp.abs(rel) <= WINDOW
            bucket = _t5_bucket(jnp.clip(rel, -MAX_DISTANCE, MAX_DISTANCE))
            for hq in range(WA_Q_HEADS):
                row = jnp.where(valid, _bias_lookup(bucket, tbl_ref, DA_HEADS + hq) * LOG2E, -jnp.inf)
                bias_ref[d, hq] = _toeplitz(row, tq, win)

    variant = (q0 - ws) // WINDOW
    kw = k_ref[0, pl.ds(ws, win), :]
    vw = v_ref[0, pl.ds(ws, win), :]
    q = q_ref[0]
    low = lax.broadcasted_iota(I32, (tq, LANES), 1) < HEAD_DIM
    kdup = [jnp.concatenate([kw[:, g * HEAD_DIM:(g + 1) * HEAD_DIM]] * 2, axis=1) for g in range(2)]
    ones = jnp.ones((win, LANES), BF16)
    vext = [jnp.concatenate([vw[:, g * HEAD_DIM:(g + 1) * HEAD_DIM]] * 2 + [ones], axis=1) for g in range(2)]
    logits = []
    for hq in range(WA_Q_HEADS):
        qp = q[:, (hq // 2) * LANES:(hq // 2 + 1) * LANES]
        qm = jnp.where(low if hq % 2 == 0 else jnp.logical_not(low), qp, jnp.zeros_like(qp))
        logits.append(lax.dot_general(qm, kdup[hq // WA_REP], NT_DIMS, preferred_element_type=F32))
    outs = []
    for pair in range(WA_Q_HEADS // 2):
        g = (2 * pair) // WA_REP
        halves = []
        for half in range(2):
            hq = 2 * pair + half
            s = logits[hq] + bias_ref[variant, hq]
            sk = sink_ref[0, hq] * LOG2E
            m = jnp.maximum(jnp.full((tq, LANES), sk, F32), jnp.max(s, axis=1, keepdims=True))
            e = jnp.exp2(s - jnp.tile(m, (1, win // LANES))).astype(BF16)
            pv = jnp.dot(e, vext[g], preferred_element_type=F32)
            halves.append(pv[:, :LANES] / (pv[:, LANES:] + jnp.exp2(sk - m)))
        outs.append(jnp.where(low, halves[0], halves[1]))
    o_ref[0] = jnp.concatenate(outs, axis=1).astype(o_ref.dtype)


def _windowed_gqa(proj3, tbl, sink, tq):
    b, s, _ = proj3.shape
    kern = functools.partial(_wa_kernel, tq=tq, seq=s)
    kv_blk = (IN_COLS - 2 * LANES) // LANES
    return pl.pallas_call(
        kern,
        out_shape=jax.ShapeDtypeStruct((b, s, WA_Q), BF16),
        grid=(b, s // tq),
        in_specs=[pl.BlockSpec(memory_space=pltpu.SMEM),
                  pl.BlockSpec(memory_space=pltpu.SMEM),
                  pl.BlockSpec((1, tq, WA_Q), lambda bi, qi: (bi, qi, 3)),
                  pl.BlockSpec((1, s, LANES), lambda bi, qi: (bi, 0, kv_blk)),
                  pl.BlockSpec((1, s, LANES), lambda bi, qi: (bi, 0, kv_blk + 1))],
        out_specs=pl.BlockSpec((1, tq, WA_Q), lambda bi, qi: (bi, qi, 0)),
        scratch_shapes=[pltpu.VMEM((3, WA_Q_HEADS, tq, tq + 2 * WINDOW), F32)],
        compiler_params=_params(("arbitrary", "arbitrary")),
        name="windowed_gqa",
    )(tbl, sink, proj3, proj3, proj3)


def _oproj_kernel(a_ref, b_ref, x_ref, w1_ref, w2_ref, g_ref, wr_ref, x1_ref, h_ref, aff_ref):
    x1 = (x_ref[...] + jnp.dot(a_ref[...], w1_ref[...], preferred_element_type=F32)
          + jnp.dot(b_ref[...], w2_ref[...], preferred_element_type=F32))
    x1_ref[...] = x1
    h = _rms(x1, g_ref[...], RMS_EPS)
    h_hi = h.astype(BF16)
    h_ref[:, :D_MODEL] = h_hi
    h_lo = (h - h_hi.astype(F32)).astype(BF16)
    wr = wr_ref[...]
    w_hi = wr.astype(BF16)
    w_lo = (wr - w_hi.astype(F32)).astype(BF16)
    hh = jnp.dot(h_hi, jnp.concatenate([w_hi, w_lo], axis=1), preferred_element_type=F32)
    logits = (hh[:, :N_EXPERTS] + hh[:, N_EXPERTS:]
              + jnp.dot(h_lo, w_hi, preferred_element_type=F32))
    e = jnp.exp(logits - jnp.max(logits, axis=1, keepdims=True))
    aff = e / jnp.sum(e, axis=1, keepdims=True)
    aff_ref[...] = aff
    a_hi = aff.astype(BF16)
    a_lo = (aff - a_hi.astype(F32)).astype(BF16)
    pad = jnp.zeros((aff.shape[0], LANES - 2 * N_EXPERTS), BF16)
    h_ref[:, D_MODEL:] = jnp.concatenate([a_hi, a_lo, pad], axis=1)


def _oproj(a, b, x2, w1, w2, g, wr, tm):
    m = x2.shape[0]
    row = lambda n: pl.BlockSpec((tm, n), lambda i: (i, 0))
    full = lambda r, c: pl.BlockSpec((r, c), lambda i: (0, 0))
    return pl.pallas_call(
        _oproj_kernel,
        out_shape=(jax.ShapeDtypeStruct((m, D_MODEL), F32),
                   jax.ShapeDtypeStruct((m, D_ROUTED), BF16),
                   jax.ShapeDtypeStruct((m, N_EXPERTS), F32)),
        grid=(m // tm,),
        in_specs=[row(DA_V), row(WA_Q), row(D_MODEL), full(DA_V, D_MODEL), full(WA_Q, D_MODEL),
                  full(1, D_MODEL), full(D_MODEL, N_EXPERTS)],
        out_specs=(row(D_MODEL), row(D_ROUTED), row(N_EXPERTS)),
        compiler_params=_params(("parallel",)),
        name="oproj_router",
    )(a, b, x2, w1, w2, g, wr)


def _cumsum_lanes(x):
    n = x.shape[1]
    lane = lax.broadcasted_iota(I32, x.shape, 1)
    shift = 1
    while shift < n:
        x = x + jnp.where(lane >= shift, pltpu.roll(x, shift, 1), 0)
        shift *= 2
    return x


def _select_kernel(aff_ref, slot_ref, excl_ref, *, cap):
    aff = aff_ref[...]

    def count(mask):
        return jnp.sum(mask.astype(F32), axis=1, keepdims=True)

    def body(i, prefix):
        cand = prefix | jnp.left_shift(jnp.int32(1), 30 - i)
        return jnp.where(count(aff >= pltpu.bitcast(cand, F32)) >= cap, cand, prefix)

    thr = lax.fori_loop(0, 31, body, jnp.zeros((aff.shape[0], 1), I32))
    above = aff >= pltpu.bitcast(thr + 1, F32)
    edge = jnp.logical_and(aff >= pltpu.bitcast(thr, F32), jnp.logical_not(above))
    need = cap - count(above)
    edge_i = edge.astype(I32)
    edge_excl = (_cumsum_lanes(edge_i) - edge_i).astype(F32)
    sel = above | (edge & (edge_excl < need))
    sel_i = sel.astype(I32)
    excl = _cumsum_lanes(sel_i) - sel_i
    slot_ref[...] = jnp.where(sel, excl, -1)
    excl_ref[...] = excl


def _select(aff_t, cap):
    shp = jax.ShapeDtypeStruct(aff_t.shape, I32)
    return pl.pallas_call(
        functools.partial(_select_kernel, cap=cap),
        out_shape=(shp, shp),
        compiler_params=_params(None),
        name="expert_select",
    )(aff_t)


def _gather_kernel(cnt_ref, x_ref, slot_ref, xg_hbm, obuf, sem, carry_ref, xo, xsem, *, nb, rw):
    b = pl.program_id(0)
    tb = x_ref.shape[0]
    cur = b % 2
    grp = BF16_SUBLANES
    groups = rw // grp

    def lo_of(e, blk):
        return cnt_ref[e * (nb + 1) + blk]

    def base_of(e, blk):
        return pl.multiple_of((lo_of(e, blk) // grp) * grp, grp)

    def out_copy(e, blk, slab):
        return pltpu.make_async_copy(obuf.at[slab, pl.ds(e * rw, rw), :],
                                     xg_hbm.at[e, pl.ds(base_of(e, blk), rw), :], sem.at[slab, e])

    def pack(rel):
        rows = lax.broadcasted_iota(I32, (rw, tb), 0)
        return jnp.where(rel == rows, 1.0, 0.0).astype(BF16)

    @pl.when(b == 0)
    def _():
        carry_ref[...] = jnp.zeros(carry_ref.shape, carry_ref.dtype)
        xo[...] = jnp.zeros(xo.shape, xo.dtype)
        for e in range(N_EXPERTS):
            cp = pltpu.make_async_copy(xo, xg_hbm.at[e, pl.ds(xg_hbm.shape[1] - rw, rw), :], xsem)
            cp.start()
            cp.wait()

    x = x_ref[...]
    onehot = jnp.concatenate([pack(slot_ref[e:e + 1, :] - base_of(e, b)) for e in range(N_EXPERTS)], axis=0)
    obuf[cur] = jnp.dot(onehot, x, preferred_element_type=F32).astype(BF16)
    for e in range(N_EXPERTS):
        obuf[cur, pl.ds(e * rw, grp), :] += carry_ref[e]
        g_hi = (lo_of(e, b + 1) - base_of(e, b)) // grp
        row0 = pl.multiple_of(e * rw + jnp.minimum(g_hi, groups - 1) * grp, grp)
        kept = obuf[cur, pl.ds(row0, grp), :]
        carry_ref[e] = jnp.where(g_hi < groups, kept, jnp.zeros_like(kept))

    @pl.when(b > 0)
    def _():
        for e in range(N_EXPERTS):
            out_copy(e, b - 1, 1 - cur).wait()

    for e in range(N_EXPERTS):
        out_copy(e, b, cur).start()

    for e in range(N_EXPERTS):
        base = base_of(e, b)
        hi = lo_of(e, b + 1)
        n_extra = (jnp.maximum(hi - base - rw, 0) + rw - 1) // rw

        def extra(w, c, e=e, base=base, hi=hi):
            first = pl.multiple_of(base + (w + 1) * rw, grp)
            xo[...] = jnp.dot(pack(slot_ref[e:e + 1, :] - first), x, preferred_element_type=F32).astype(BF16)
            cp = pltpu.make_async_copy(xo, xg_hbm.at[e, pl.ds(first, rw), :], xsem)
            cp.start()
            cp.wait()
            g_hi = (hi - first) // grp
            row0 = pl.multiple_of(jnp.clip(g_hi, 0, groups - 1) * grp, grp)
            carry_ref[e] = jnp.where(jnp.logical_and(g_hi >= 0, g_hi < groups),
                                     xo[pl.ds(row0, grp), :], carry_ref[e])
            return c

        lax.fori_loop(0, n_extra, extra, 0)

    @pl.when(b == nb - 1)
    def _():
        for e in range(N_EXPERTS):
            out_copy(e, b, cur).wait()


def _gather(cnt, h2, slot, cap, tb):
    m, width = h2.shape
    nb = m // tb
    rw = (tb // 8 + max(tb // 16, 32) + BF16_SUBLANES) // BF16_SUBLANES * BF16_SUBLANES
    assert cap % BF16_SUBLANES == 0
    return pl.pallas_call(
        functools.partial(_gather_kernel, nb=nb, rw=rw),
        out_shape=jax.ShapeDtypeStruct((N_EXPERTS, cap + rw, width), BF16),
        grid_spec=pltpu.PrefetchScalarGridSpec(
            num_scalar_prefetch=1,
            grid=(nb,),
            in_specs=[pl.BlockSpec((tb, width), lambda b, c: (b, 0)),
                      pl.BlockSpec((N_EXPERTS, tb), lambda b, c: (0, b))],
            out_specs=pl.BlockSpec(memory_space=pl.ANY),
            scratch_shapes=[pltpu.VMEM((2, N_EXPERTS * rw, width), BF16),
                            pltpu.SemaphoreType.DMA((2, N_EXPERTS)),
                            pltpu.VMEM((N_EXPERTS, BF16_SUBLANES, width), BF16),
                            pltpu.VMEM((rw, width), BF16),
                            pltpu.SemaphoreType.DMA(())]),
        compiler_params=_params(("arbitrary",)),
        name="moe_gather",
    )(cnt, h2, slot)


def _ffn_kernel(x_ref, wg_ref, wu_ref, wd_ref, y_ref, wg_s, wu_s, wd_s):
    @pl.when(pl.program_id(1) == 0)
    def _():
        wg_s[...] = wg_ref[0].astype(BF16)
        wu_s[...] = wu_ref[0].astype(BF16)
        wd_s[...] = wd_ref[0].astype(BF16)

    x = x_ref[0, :, :D_MODEL]
    tail = x_ref[0, :, D_MODEL:].astype(F32)
    lane = lax.broadcasted_iota(I32, tail.shape, 1)
    e = pl.program_id(0)
    mine = jnp.logical_or(lane == e, lane == e + N_EXPERTS)
    gate = jnp.sum(jnp.where(mine, tail, 0.0), axis=1, keepdims=True)
    g = jnp.dot(x, wg_s[...], preferred_element_type=F32)
    u = jnp.dot(x, wu_s[...], preferred_element_type=F32)
    h = (g * jax.nn.sigmoid(g) * u).astype(BF16)
    y_ref[0] = (jnp.dot(h, wd_s[...], preferred_element_type=F32) * gate).astype(y_ref.dtype)


def _ffn(xg, wg, wu, wd, cap, tm):
    wspec = lambda r, c: pl.BlockSpec((1, r, c), lambda e, i: (e, 0, 0))
    return pl.pallas_call(
        _ffn_kernel,
        out_shape=jax.ShapeDtypeStruct((N_EXPERTS, cap, D_MODEL), BF16),
        grid=(N_EXPERTS, cap // tm),
        in_specs=[pl.BlockSpec((1, tm, D_ROUTED), lambda e, i: (e, i, 0)),
                  wspec(D_MODEL, EXPERT_FF), wspec(D_MODEL, EXPERT_FF), wspec(EXPERT_FF, D_MODEL)],
        out_specs=pl.BlockSpec((1, tm, D_MODEL), lambda e, i: (e, i, 0)),
        scratch_shapes=[pltpu.VMEM((D_MODEL, EXPERT_FF), BF16), pltpu.VMEM((D_MODEL, EXPERT_FF), BF16),
                        pltpu.VMEM((EXPERT_FF, D_MODEL), BF16)],
        compiler_params=_params(("arbitrary", "arbitrary")),
        name="moe_ffn",
    )(xg, wg, wu, wd)


def _combine_kernel(cnt_ref, x1_ref, slot_ref, fn_ref, y_hbm, o_ref,
                    buf, sem, xbuf, xsem, acc_ref, *, nb, cap, rw):
    b = pl.program_id(0)
    tb = x1_ref.shape[0]
    xr = xbuf.shape[0]

    def window_start(e, blk):
        lo = cnt_ref[e * (nb + 1) + blk]
        return pl.multiple_of(jnp.minimum((lo // BF16_SUBLANES) * BF16_SUBLANES, cap - rw), BF16_SUBLANES)

    def window_copy(e, start, slab):
        return pltpu.make_async_copy(y_hbm.at[e, pl.ds(start, rw), :],
                                     buf.at[slab, pl.ds(e * rw, rw), :], sem.at[slab, e])

    def fetch(blk, slab):
        for e in range(N_EXPERTS):
            window_copy(e, window_start(e, blk), slab).start()

    cur = b % 2

    @pl.when(b == 0)
    def _():
        fetch(0, 0)

    @pl.when(b + 1 < nb)
    def _():
        fetch(b + 1, 1 - cur)

    starts = [window_start(e, b) for e in range(N_EXPERTS)]
    for e in range(N_EXPERTS):
        window_copy(e, starts[e], cur).wait()

    width = N_EXPERTS * rw
    ex = (lax.broadcasted_iota(I32, (N_EXPERTS, width), 0)
          == lax.broadcasted_iota(I32, (N_EXPERTS, width), 1) // rw).astype(BF16)
    lane16 = lax.broadcasted_iota(I32, (1, N_EXPERTS), 1)
    st_vec = jnp.zeros((1, N_EXPERTS), I32)
    for e in range(N_EXPERTS):
        st_vec = jnp.where(lane16 == e, starts[e], st_vec)
    rel = jnp.clip(slot_ref[...] - st_vec, -1, rw).astype(F32).astype(BF16)
    row = (lax.broadcasted_iota(I32, (tb, width), 1) % rw).astype(F32)
    hit = jnp.dot(rel, ex, preferred_element_type=F32) == row
    onehot = jnp.where(hit, 1.0, 0.0).astype(BF16)
    acc_ref[...] = x1_ref[...] + jnp.dot(onehot, buf[cur], preferred_element_type=F32)

    xlane = lax.broadcasted_iota(I32, (tb, xr), 1)
    for e in range(N_EXPERTS):
        done = starts[e] + rw
        n_extra = (jnp.maximum(cnt_ref[e * (nb + 1) + b + 1] - done, 0) + xr - 1) // xr

        def extra(w, c, e=e, done=done):
            first = done + w * xr
            cst = pl.multiple_of(jnp.minimum(first, cap - xr), BF16_SUBLANES)
            cp = pltpu.make_async_copy(y_hbm.at[e, pl.ds(cst, xr), :], xbuf, xsem)
            cp.start()
            cp.wait()
            slot = slot_ref[:, e:e + 1]
            hit = jnp.logical_and(slot - cst == xlane, slot >= first)
            onehot = jnp.where(hit, 1.0, 0.0).astype(BF16)
            acc_ref[...] += jnp.dot(onehot, xbuf[...], preferred_element_type=F32)
            return c

        lax.fori_loop(0, n_extra, extra, 0)

    o_ref[...] = _rms(acc_ref[...], fn_ref[...], RMS_EPS)


def _combine(cnt, x1, slot_t, fn, y, tb):
    m = x1.shape[0]
    nb = m // tb
    cap = y.shape[1]
    rw = LANES // 2
    assert cap % BF16_SUBLANES == 0 and cap >= LANES
    return pl.pallas_call(
        functools.partial(_combine_kernel, nb=nb, cap=cap, rw=rw),
        out_shape=jax.ShapeDtypeStruct((m, D_MODEL), F32),
        grid_spec=pltpu.PrefetchScalarGridSpec(
            num_scalar_prefetch=1,
            grid=(nb,),
            in_specs=[pl.BlockSpec((tb, D_MODEL), lambda b, c: (b, 0)),
                      pl.BlockSpec((tb, N_EXPERTS), lambda b, c: (b, 0)),
                      pl.BlockSpec((1, D_MODEL), lambda b, c: (0, 0)),
                      pl.BlockSpec(memory_space=pl.ANY)],
            out_specs=pl.BlockSpec((tb, D_MODEL), lambda b, c: (b, 0)),
            scratch_shapes=[pltpu.VMEM((2, N_EXPERTS * rw, D_MODEL), BF16),
                            pltpu.SemaphoreType.DMA((2, N_EXPERTS)),
                            pltpu.VMEM((LANES, D_MODEL), BF16),
                            pltpu.SemaphoreType.DMA(()),
                            pltpu.VMEM((tb, D_MODEL), F32)]),
        compiler_params=_params(("arbitrary",)),
        name="moe_combine",
    )(cnt, x1, slot_t, fn, y)


def _pick(n, prefs):
    for p in prefs:
        if n % p == 0:
            return p
    raise ValueError(f"no tile in {prefs} divides {n}")


def _trunk(x, tbl, sink, lam, subln_g, norm1, w_in, w_o1, w_o2, norm2, wr, wg, wu, wd, fn):
    bsz, s, d = x.shape
    m = bsz * s
    cap = CAPACITY_FACTOR * m // N_EXPERTS
    x2 = x.reshape(m, d)
    proj = _inproj(x2, norm1, w_in, _pick(m, (1024, 512, 256)))
    proj3 = proj.reshape(bsz, s, IN_COLS)
    out_a = _diff_attention(proj3, tbl, *lam, subln_g, _pick(s, (512, 256)))
    out_b = _windowed_gqa(proj3, tbl, sink, _pick(s, (256,)))
    x1, h2, aff = _oproj(out_a.reshape(m, DA_V), out_b.reshape(m, WA_Q), x2, w_o1, w_o2,
                         norm2, wr, _pick(m, (1024, 512, 256)))
    slot, excl = _select(aff.T, cap)
    def block_counts(tb):
        return jnp.concatenate([excl[:, ::tb], jnp.full((N_EXPERTS, 1), cap, I32)], axis=1).reshape(-1)

    tb_gather = _pick(m, (512, 256))
    tb_combine = 256
    xg = _gather(block_counts(tb_gather), h2, slot, cap, tb_gather)
    y = _ffn(xg, wg, wu, wd, cap, _pick(cap, (1024, 512, 256, 128)))
    out = _combine(block_counts(tb_combine), x1, slot.T, fn, y, tb_combine)
    return out.reshape(bsz, s, d)


def kernel(x_prompt, x_sample, rel_bias, norm1, w_in, lam_q1, lam_k1, lam_q2, lam_k2, subln_g, sink,
           w_o, norm2, w_router, w_gate, w_up, w_down, final_norm):
    col = jnp.arange(IN_COLS)
    is_q = (col < DA_HEADS * 2 * HEAD_DIM) | ((col >= 3 * DA_V) & (col < 3 * DA_V + WA_Q))
    col_scale = jnp.where(is_q, HEAD_DIM ** -0.5 * LOG2E, 1.0).astype(F32)
    w_in_s = (w_in[0] * col_scale).astype(BF16)
    args = dict(
        tbl=rel_bias, sink=sink,
        lam=(lam_q1, lam_k1, lam_q2, lam_k2), subln_g=subln_g,
        norm1=norm1, w_in=w_in_s,
        w_o1=w_o[0, :DA_V].astype(BF16), w_o2=w_o[0, DA_V:].astype(BF16),
        norm2=norm2, wr=w_router[0],
        wg=w_gate[0], wu=w_up[0], wd=w_down[0],
        fn=final_norm.reshape(1, D_MODEL))
    return (_trunk(x_prompt, **args), _trunk(x_sample, **args))
```

```python
import functools
import math

import jax
import jax.numpy as jnp
from jax import lax
from jax.experimental import pallas as pl
from jax.experimental.pallas import tpu as pltpu

F32 = jnp.float32
BF16 = jnp.bfloat16
I32 = jnp.int32

D_MODEL = 1024
HEAD_DIM = 64
DA_HEADS = 4
WA_Q_HEADS = 8
WA_REP = 4
WINDOW = 128
IN_COLS = 2304
DA_V = 512
WA_Q = 512
N_BUCKETS = 32
MAX_DISTANCE = 128
N_EXPERTS = 16
EXPERT_FF = 1024
CAPACITY_FACTOR = 2
RMS_EPS = 1e-6
SUBLN_EPS = 1e-5
LAMBDA_INIT = 0.8 - 0.6 * math.exp(-0.3 * 0)
LOG2E = math.log2(math.e)
DA_NEAR = 4

LANES = 128
D_ROUTED = D_MODEL + LANES
BF16_SUBLANES = 16
VMEM_LIMIT = 56 * 1024 * 1024
NT_DIMS = (((1,), (1,)), ((), ()))


def _params(sem, vmem=VMEM_LIMIT):
    return pltpu.CompilerParams(dimension_semantics=sem, vmem_limit_bytes=vmem)


def _rms(x, g, eps):
    return x * lax.rsqrt(jnp.mean(x * x, axis=-1, keepdims=True) + eps) * g


def _inproj_kernel(x_ref, g_ref, w_ref, o_ref):
    h = _rms(x_ref[...], g_ref[...], RMS_EPS).astype(BF16)
    o_ref[...] = jnp.dot(h, w_ref[...], preferred_element_type=F32).astype(o_ref.dtype)


def _inproj(x2, g, w, tm):
    m = x2.shape[0]
    return pl.pallas_call(
        _inproj_kernel,
        out_shape=jax.ShapeDtypeStruct((m, IN_COLS), BF16),
        grid=(m // tm,),
        in_specs=[pl.BlockSpec((tm, D_MODEL), lambda i: (i, 0)),
                  pl.BlockSpec((1, D_MODEL), lambda i: (0, 0)),
                  pl.BlockSpec((D_MODEL, IN_COLS), lambda i: (0, 0))],
        out_specs=pl.BlockSpec((tm, IN_COLS), lambda i: (i, 0)),
        compiler_params=_params(("parallel",)),
        name="inproj",
    )(x2, g, w)


def _t5_bucket(rel):
    nb = N_BUCKETS // 2
    max_exact = nb // 2
    ret = jnp.where(rel > 0, nb, 0)
    n = jnp.abs(rel)
    nf = jnp.maximum(n, 1).astype(F32)
    frac = jnp.log2(nf / max_exact) * ((nb - max_exact) / math.log2(MAX_DISTANCE / max_exact))
    large = max_exact + jnp.where(frac >= 0, jnp.floor(frac), jnp.ceil(frac)).astype(I32)
    large = jnp.minimum(large, nb - 1)
    return ret + jnp.where(n < max_exact, n, large)


def _bias_lookup(bucket, tbl_ref, col):
    out = jnp.zeros(bucket.shape, F32)
    for b in range(N_BUCKETS):
        out = jnp.where(bucket == b, tbl_ref[b, col], out)
    return out


def _toeplitz(row, tq, tk):
    width = tq + tk
    rb = jnp.broadcast_to(row, (tq, width))
    rolled = pltpu.roll(rb, width - tq + 1, 1, stride=1, stride_axis=0)
    return rolled[:, :tk]


def _da_kernel(tbl_ref, lq1_ref, lk1_ref, lq2_ref, lk2_ref, q_ref, k_ref, v_ref, g_ref,
               o_ref, qm_ref, m_ref, l_ref, acc_ref, bias_ref, vext_ref, *, t, seq):
    h = pl.program_id(1)
    qi = pl.program_id(2)
    nk = seq // t

    @pl.when(qi == 0)
    def _():
        vext_ref[:, :LANES] = v_ref[0]
        vext_ref[:, LANES:] = jnp.ones((seq, LANES), BF16)
        for d in range(2 * DA_NEAR - 1):
            jj = lax.broadcasted_iota(I32, (1, 2 * t), 1)
            rel = jnp.clip((d - (DA_NEAR - 1)) * t + jj - (t - 1), -MAX_DISTANCE, MAX_DISTANCE)
            bias_ref[d] = _toeplitz(_bias_lookup(_t5_bucket(rel), tbl_ref, h) * LOG2E, t, t)

    c_left = tbl_ref[N_BUCKETS // 2 - 1, h] * LOG2E
    c_right = tbl_ref[N_BUCKETS - 1, h] * LOG2E

    q = q_ref[0]
    lane = lax.broadcasted_iota(I32, q.shape, 1)
    qm_ref[0] = jnp.where(lane < HEAD_DIM, q, jnp.zeros_like(q))
    qm_ref[1] = jnp.where(lane >= HEAD_DIM, q, jnp.zeros_like(q))
    m_ref[...] = jnp.full(m_ref.shape, -jnp.inf, F32)
    l_ref[...] = jnp.zeros(l_ref.shape, F32)
    acc_ref[...] = jnp.zeros(acc_ref.shape, F32)

    def step(tile, width, bias_idx, mxu_sums=False):
        start = pl.multiple_of(tile * t, t)
        k = k_ref[0, pl.ds(start, width), :]
        v = vext_ref[pl.ds(start, width), :] if mxu_sums else v_ref[0, pl.ds(start, width), :]
        reps = width // LANES
        logits = [lax.dot_general(qm_ref[mp], k, NT_DIMS, preferred_element_type=F32) for mp in range(2)]
        for mp in range(2):
            s = logits[mp]
            if bias_idx is not None:
                s = s + jnp.concatenate([bias_ref[bias_idx + i] for i in range(width // t)], axis=1)
            m_prev = m_ref[mp]
            m_new = jnp.maximum(m_prev, jnp.max(s, axis=1, keepdims=True))
            alpha = jnp.exp2(m_prev - m_new)
            p = jnp.exp2(s - jnp.tile(m_new, (1, reps)))
            pv = jnp.dot(p.astype(BF16), v, preferred_element_type=F32)
            if mxu_sums:
                psum = pv[:, LANES:] * (1.0 / LANES)
                pv = pv[:, :LANES]
            else:
                psum = p[:, :LANES]
                for r in range(1, reps):
                    psum = psum + p[:, r * LANES:(r + 1) * LANES]
            l_ref[mp] = alpha * l_ref[mp] + psum
            acc_ref[mp] = alpha * acc_ref[mp] + pv
            m_ref[mp] = m_new

    def far(first, count):
        def quad(j, c):
            step(first + 4 * j, 4 * t, None, mxu_sums=True)
            return c

        def pair(j, c):
            step(first + (count // 4) * 4, 2 * t, None, mxu_sums=True)
            return c

        def single(j, c):
            step(first + count - 1, t, None)
            return c

        lax.fori_loop(0, count // 4, quad, 0)
        lax.fori_loop(0, (count % 4) // 2, pair, 0)
        lax.fori_loop(0, count % 2, single, 0)

    first_near = jnp.clip(qi - 1, 0, nk - DA_NEAR)
    first_right = first_near + DA_NEAR
    far(0, first_near)
    m_ref[...] += c_left
    step(first_near, DA_NEAR * t, first_near - qi + DA_NEAR - 1)
    m_ref[...] -= c_right
    far(first_right, nk - first_right)

    outs = [acc_ref[mp] / jnp.sum(l_ref[mp], axis=1, keepdims=True) for mp in range(2)]
    lam = (jnp.exp(jnp.sum(lq1_ref[...] * lk1_ref[...], axis=1, keepdims=True))
           - jnp.exp(jnp.sum(lq2_ref[...] * lk2_ref[...], axis=1, keepdims=True))
           + LAMBDA_INIT)
    o = outs[0] - lam * outs[1]
    o = _rms(o, g_ref[...], SUBLN_EPS) * (1.0 - LAMBDA_INIT)
    o_ref[0] = o.astype(o_ref.dtype)


def _diff_attention(proj3, tbl, lq1, lk1, lq2, lk2, subln_g, t):
    b, s, _ = proj3.shape
    assert t >= MAX_DISTANCE and s % t == 0 and s // t >= DA_NEAR
    vec = lambda n: pl.BlockSpec((1, n), lambda bi, hi, qi: (0, 0))
    kern = functools.partial(_da_kernel, t=t, seq=s)
    return pl.pallas_call(
        kern,
        out_shape=jax.ShapeDtypeStruct((b, s, DA_V), BF16),
        grid=(b, DA_HEADS, s // t),
        in_specs=[pl.BlockSpec(memory_space=pltpu.SMEM),
                  vec(HEAD_DIM), vec(HEAD_DIM), vec(HEAD_DIM), vec(HEAD_DIM),
                  pl.BlockSpec((1, t, LANES), lambda bi, hi, qi: (bi, qi, hi)),
                  pl.BlockSpec((1, s, LANES), lambda bi, hi, qi: (bi, 0, DA_HEADS + hi)),
                  pl.BlockSpec((1, s, LANES), lambda bi, hi, qi: (bi, 0, 2 * DA_HEADS + hi)),
                  vec(2 * HEAD_DIM)],
        out_specs=pl.BlockSpec((1, t, LANES), lambda bi, hi, qi: (bi, qi, hi)),
        scratch_shapes=[pltpu.VMEM((2, t, LANES), BF16), pltpu.VMEM((2, t, LANES), F32),
                        pltpu.VMEM((2, t, LANES), F32), pltpu.VMEM((2, t, 2 * HEAD_DIM), F32),
                        pltpu.VMEM((2 * DA_NEAR - 1, t, t), F32), pltpu.VMEM((s, 2 * LANES), BF16)],
        compiler_params=_params(("parallel", "parallel", "arbitrary")),
        name="diff_attention",
    )(tbl, lq1, lk1, lq2, lk2, proj3, proj3, proj3, subln_g)


def _wa_kernel(tbl_ref, sink_ref, q_ref, k_ref, v_ref, o_ref, bias_ref, *, tq, seq):
    q0 = pl.program_id(1) * tq
    win = tq + 2 * WINDOW
    ws = pl.multiple_of(jnp.clip(q0 - WINDOW, 0, seq - win), LANES)

    @pl.when(jnp.logical_and(pl.program_id(0) == 0, pl.program_id(1) == 0))
    def _():
        jj = lax.broadcasted_iota(I32, (1, win + tq), 1)
        for d in range(3):
            rel = jj - d * WINDOW - (tq - 1)
            valid = jnp.abs(rel) <= WINDOW
            bucket = _t5_bucket(jnp.clip(rel, -MAX_DISTANCE, MAX_DISTANCE))
            for hq in range(WA_Q_HEADS):
                row = jnp.where(valid, _bias_lookup(bucket, tbl_ref, DA_HEADS + hq) * LOG2E, -jnp.inf)
                bias_ref[d, hq] = _toeplitz(row, tq, win)

    variant = (q0 - ws) // WINDOW
    kw = k_ref[0, pl.ds(ws, win), :]
    vw = v_ref[0, pl.ds(ws, win), :]
    q = q_ref[0]
    low = lax.broadcasted_iota(I32, (tq, LANES), 1) < HEAD_DIM
    kdup = [jnp.concatenate([kw[:, g * HEAD_DIM:(g + 1) * HEAD_DIM]] * 2, axis=1) for g in range(2)]
    ones = jnp.ones((win, LANES), BF16)
    vext = [jnp.concatenate([vw[:, g * HEAD_DIM:(g + 1) * HEAD_DIM]] * 2 + [ones], axis=1) for g in range(2)]
    logits = []
    for hq in range(WA_Q_HEADS):
        qp = q[:, (hq // 2) * LANES:(hq // 2 + 1) * LANES]
        qm = jnp.where(low if hq % 2 == 0 else jnp.logical_not(low), qp, jnp.zeros_like(qp))
        logits.append(lax.dot_general(qm, kdup[hq // WA_REP], NT_DIMS, preferred_element_type=F32))
    outs = []
    for pair in range(WA_Q_HEADS // 2):
        g = (2 * pair) // WA_REP
        halves = []
        for half in range(2):
            hq = 2 * pair + half
            s = logits[hq] + bias_ref[variant, hq]
            sk = sink_ref[0, hq] * LOG2E
            m = jnp.maximum(jnp.full((tq, LANES), sk, F32), jnp.max(s, axis=1, keepdims=True))
            e = jnp.exp2(s - jnp.tile(m, (1, win // LANES))).astype(BF16)
            pv = jnp.dot(e, vext[g], preferred_element_type=F32)
            halves.append(pv[:, :LANES] / (pv[:, LANES:] + jnp.exp2(sk - m)))
        outs.append(jnp.where(low, halves[0], halves[1]))
    o_ref[0] = jnp.concatenate(outs, axis=1).astype(o_ref.dtype)


def _windowed_gqa(proj3, tbl, sink, tq):
    b, s, _ = proj3.shape
    kern = functools.partial(_wa_kernel, tq=tq, seq=s)
    kv_blk = (IN_COLS - 2 * LANES) // LANES
    return pl.pallas_call(
        kern,
        out_shape=jax.ShapeDtypeStruct((b, s, WA_Q), BF16),
        grid=(b, s // tq),
        in_specs=[pl.BlockSpec(memory_space=pltpu.SMEM),
                  pl.BlockSpec(memory_space=pltpu.SMEM),
                  pl.BlockSpec((1, tq, WA_Q), lambda bi, qi: (bi, qi, 3)),
                  pl.BlockSpec((1, s, LANES), lambda bi, qi: (bi, 0, kv_blk)),
                  pl.BlockSpec((1, s, LANES), lambda bi, qi: (bi, 0, kv_blk + 1))],
        out_specs=pl.BlockSpec((1, tq, WA_Q), lambda bi, qi: (bi, qi, 0)),
        scratch_shapes=[pltpu.VMEM((3, WA_Q_HEADS, tq, tq + 2 * WINDOW), F32)],
        compiler_params=_params(("arbitrary", "arbitrary")),
        name="windowed_gqa",
    )(tbl, sink, proj3, proj3, proj3)


def _oproj_kernel(a_ref, b_ref, x_ref, w1_ref, w2_ref, g_ref, wr_ref, x1_ref, h_ref, aff_ref):
    x1 = (x_ref[...] + jnp.dot(a_ref[...], w1_ref[...], preferred_element_type=F32)
          + jnp.dot(b_ref[...], w2_ref[...], preferred_element_type=F32))
    x1_ref[...] = x1
    h = _rms(x1, g_ref[...], RMS_EPS)
    h_hi = h.astype(BF16)
    h_ref[:, :D_MODEL] = h_hi
    h_lo = (h - h_hi.astype(F32)).astype(BF16)
    wr = wr_ref[...]
    w_hi = wr.astype(BF16)
    w_lo = (wr - w_hi.astype(F32)).astype(BF16)
    hh = jnp.dot(h_hi, jnp.concatenate([w_hi, w_lo], axis=1), preferred_element_type=F32)
    logits = (hh[:, :N_EXPERTS] + hh[:, N_EXPERTS:]
              + jnp.dot(h_lo, w_hi, preferred_element_type=F32))
    e = jnp.exp(logits - jnp.max(logits, axis=1, keepdims=True))
    aff = e / jnp.sum(e, axis=1, keepdims=True)
    aff_ref[...] = aff
    a_hi = aff.astype(BF16)
    a_lo = (aff - a_hi.astype(F32)).astype(BF16)
    pad = jnp.zeros((aff.shape[0], LANES - 2 * N_EXPERTS), BF16)
    h_ref[:, D_MODEL:] = jnp.concatenate([a_hi, a_lo, pad], axis=1)


def _oproj(a, b, x2, w1, w2, g, wr, tm):
    m = x2.shape[0]
    row = lambda n: pl.BlockSpec((tm, n), lambda i: (i, 0))
    full = lambda r, c: pl.BlockSpec((r, c), lambda i: (0, 0))
    return pl.pallas_call(
        _oproj_kernel,
        out_shape=(jax.ShapeDtypeStruct((m, D_MODEL), F32),
                   jax.ShapeDtypeStruct((m, D_ROUTED), BF16),
                   jax.ShapeDtypeStruct((m, N_EXPERTS), F32)),
        grid=(m // tm,),
        in_specs=[row(DA_V), row(WA_Q), row(D_MODEL), full(DA_V, D_MODEL), full(WA_Q, D_MODEL),
                  full(1, D_MODEL), full(D_MODEL, N_EXPERTS)],
        out_specs=(row(D_MODEL), row(D_ROUTED), row(N_EXPERTS)),
        compiler_params=_params(("parallel",)),
        name="oproj_router",
    )(a, b, x2, w1, w2, g, wr)


def _cumsum_lanes(x):
    n = x.shape[1]
    lane = lax.broadcasted_iota(I32, x.shape, 1)
    shift = 1
    while shift < n:
        x = x + jnp.where(lane >= shift, pltpu.roll(x, shift, 1), 0)
        shift *= 2
    return x


def _select_kernel(aff_ref, slot_ref, excl_ref, *, cap):
    aff = aff_ref[...]

    def count(mask):
        return jnp.sum(mask.astype(F32), axis=1, keepdims=True)

    def body(i, prefix):
        cand = prefix | jnp.left_shift(jnp.int32(1), 30 - i)
        return jnp.where(count(aff >= pltpu.bitcast(cand, F32)) >= cap, cand, prefix)

    thr = lax.fori_loop(0, 31, body, jnp.zeros((aff.shape[0], 1), I32))
    above = aff >= pltpu.bitcast(thr + 1, F32)
    edge = jnp.logical_and(aff >= pltpu.bitcast(thr, F32), jnp.logical_not(above))
    need = cap - count(above)
    edge_i = edge.astype(I32)
    edge_excl = (_cumsum_lanes(edge_i) - edge_i).astype(F32)
    sel = above | (edge & (edge_excl < need))
    sel_i = sel.astype(I32)
    excl = _cumsum_lanes(sel_i) - sel_i
    slot_ref[...] = jnp.where(sel, excl, -1)
    excl_ref[...] = excl


def _select(aff_t, cap):
    shp = jax.ShapeDtypeStruct(aff_t.shape, I32)
    return pl.pallas_call(
        functools.partial(_select_kernel, cap=cap),
        out_shape=(shp, shp),
        compiler_params=_params(None),
        name="expert_select",
    )(aff_t)


def _gather_kernel(cnt_ref, x_ref, slot_ref, xg_hbm, obuf, sem, carry_ref, xo, xsem, *, nb, rw):
    b = pl.program_id(0)
    tb = x_ref.shape[0]
    cur = b % 2
    grp = BF16_SUBLANES
    groups = rw // grp

    def lo_of(e, blk):
        return cnt_ref[e * (nb + 1) + blk]

    def base_of(e, blk):
        return pl.multiple_of((lo_of(e, blk) // grp) * grp, grp)

    def out_copy(e, blk, slab):
        return pltpu.make_async_copy(obuf.at[slab, pl.ds(e * rw, rw), :],
                                     xg_hbm.at[e, pl.ds(base_of(e, blk), rw), :], sem.at[slab, e])

    def pack(rel):
        rows = lax.broadcasted_iota(I32, (rw, tb), 0)
        return jnp.where(rel == rows, 1.0, 0.0).astype(BF16)

    @pl.when(b == 0)
    def _():
        carry_ref[...] = jnp.zeros(carry_ref.shape, carry_ref.dtype)
        xo[...] = jnp.zeros(xo.shape, xo.dtype)
        for e in range(N_EXPERTS):
            cp = pltpu.make_async_copy(xo, xg_hbm.at[e, pl.ds(xg_hbm.shape[1] - rw, rw), :], xsem)
            cp.start()
            cp.wait()

    x = x_ref[...]
    onehot = jnp.concatenate([pack(slot_ref[e:e + 1, :] - base_of(e, b)) for e in range(N_EXPERTS)], axis=0)
    obuf[cur] = jnp.dot(onehot, x, preferred_element_type=F32).astype(BF16)
    for e in range(N_EXPERTS):
        obuf[cur, pl.ds(e * rw, grp), :] += carry_ref[e]
        g_hi = (lo_of(e, b + 1) - base_of(e, b)) // grp
        row0 = pl.multiple_of(e * rw + jnp.minimum(g_hi, groups - 1) * grp, grp)
        kept = obuf[cur, pl.ds(row0, grp), :]
        carry_ref[e] = jnp.where(g_hi < groups, kept, jnp.zeros_like(kept))

    @pl.when(b > 0)
    def _():
        for e in range(N_EXPERTS):
            out_copy(e, b - 1, 1 - cur).wait()

    for e in range(N_EXPERTS):
        out_copy(e, b, cur).start()

    for e in range(N_EXPERTS):
        base = base_of(e, b)
        hi = lo_of(e, b + 1)
        n_extra = (jnp.maximum(hi - base - rw, 0) + rw - 1) // rw

        def extra(w, c, e=e, base=base, hi=hi):
            first = pl.multiple_of(base + (w + 1) * rw, grp)
            xo[...] = jnp.dot(pack(slot_ref[e:e + 1, :] - first), x, preferred_element_type=F32).astype(BF16)
            cp = pltpu.make_async_copy(xo, xg_hbm.at[e, pl.ds(first, rw), :], xsem)
            cp.start()
            cp.wait()
            g_hi = (hi - first) // grp
            row0 = pl.multiple_of(jnp.clip(g_hi, 0, groups - 1) * grp, grp)
            carry_ref[e] = jnp.where(jnp.logical_and(g_hi >= 0, g_hi < groups),
                                     xo[pl.ds(row0, grp), :], carry_ref[e])
            return c

        lax.fori_loop(0, n_extra, extra, 0)

    @pl.when(b == nb - 1)
    def _():
        for e in range(N_EXPERTS):
            out_copy(e, b, cur).wait()


def _gather(cnt, h2, slot, cap, tb):
    m, width = h2.shape
    nb = m // tb
    rw = (tb // 8 + max(tb // 16, 32) + BF16_SUBLANES) // BF16_SUBLANES * BF16_SUBLANES
    assert cap % BF16_SUBLANES == 0
    return pl.pallas_call(
        functools.partial(_gather_kernel, nb=nb, rw=rw),
        out_shape=jax.ShapeDtypeStruct((N_EXPERTS, cap + rw, width), BF16),
        grid_spec=pltpu.PrefetchScalarGridSpec(
            num_scalar_prefetch=1,
            grid=(nb,),
            in_specs=[pl.BlockSpec((tb, width), lambda b, c: (b, 0)),
                      pl.BlockSpec((N_EXPERTS, tb), lambda b, c: (0, b))],
            out_specs=pl.BlockSpec(memory_space=pl.ANY),
            scratch_shapes=[pltpu.VMEM((2, N_EXPERTS * rw, width), BF16),
                            pltpu.SemaphoreType.DMA((2, N_EXPERTS)),
                            pltpu.VMEM((N_EXPERTS, BF16_SUBLANES, width), BF16),
                            pltpu.VMEM((rw, width), BF16),
                            pltpu.SemaphoreType.DMA(())]),
        compiler_params=_params(("arbitrary",)),
        name="moe_gather",
    )(cnt, h2, slot)


def _ffn_kernel(x_ref, wg_ref, wu_ref, wd_ref, y_ref, wg_s, wu_s, wd_s):
    @pl.when(pl.program_id(1) == 0)
    def _():
        wg_s[...] = wg_ref[0].astype(BF16)
        wu_s[...] = wu_ref[0].astype(BF16)
        wd_s[...] = wd_ref[0].astype(BF16)

    x = x_ref[0, :, :D_MODEL]
    tail = x_ref[0, :, D_MODEL:].astype(F32)
    lane = lax.broadcasted_iota(I32, tail.shape, 1)
    e = pl.program_id(0)
    mine = jnp.logical_or(lane == e, lane == e + N_EXPERTS)
    gate = jnp.sum(jnp.where(mine, tail, 0.0), axis=1, keepdims=True)
    g = jnp.dot(x, wg_s[...], preferred_element_type=F32)
    u = jnp.dot(x, wu_s[...], preferred_element_type=F32)
    h = (g * jax.nn.sigmoid(g) * u).astype(BF16)
    y_ref[0] = (jnp.dot(h, wd_s[...], preferred_element_type=F32) * gate).astype(y_ref.dtype)


def _ffn(xg, wg, wu, wd, cap, tm):
    wspec = lambda r, c: pl.BlockSpec((1, r, c), lambda e, i: (e, 0, 0))
    return pl.pallas_call(
        _ffn_kernel,
        out_shape=jax.ShapeDtypeStruct((N_EXPERTS, cap, D_MODEL), BF16),
        grid=(N_EXPERTS, cap // tm),
        in_specs=[pl.BlockSpec((1, tm, D_ROUTED), lambda e, i: (e, i, 0)),
                  wspec(D_MODEL, EXPERT_FF), wspec(D_MODEL, EXPERT_FF), wspec(EXPERT_FF, D_MODEL)],
        out_specs=pl.BlockSpec((1, tm, D_MODEL), lambda e, i: (e, i, 0)),
        scratch_shapes=[pltpu.VMEM((D_MODEL, EXPERT_FF), BF16), pltpu.VMEM((D_MODEL, EXPERT_FF), BF16),
                        pltpu.VMEM((EXPERT_FF, D_MODEL), BF16)],
        compiler_params=_params(("arbitrary", "arbitrary")),
        name="moe_ffn",
    )(xg, wg, wu, wd)


def _combine_kernel(cnt_ref, x1_ref, slot_ref, fn_ref, y_hbm, o_ref,
                    buf, sem, xbuf, xsem, acc_ref, *, nb, cap, rw):
    b = pl.program_id(0)
    tb = x1_ref.shape[0]
    xr = xbuf.shape[0]

    def window_start(e, blk):
        lo = cnt_ref[e * (nb + 1) + blk]
        return pl.multiple_of(jnp.minimum((lo // BF16_SUBLANES) * BF16_SUBLANES, cap - rw), BF16_SUBLANES)

    def window_copy(e, start, slab):
        return pltpu.make_async_copy(y_hbm.at[e, pl.ds(start, rw), :],
                                     buf.at[slab, pl.ds(e * rw, rw), :], sem.at[slab, e])

    def fetch(blk, slab):
        for e in range(N_EXPERTS):
            window_copy(e, window_start(e, blk), slab).start()

    cur = b % 2

    @pl.when(b == 0)
    def _():
        fetch(0, 0)

    @pl.when(b + 1 < nb)
    def _():
        fetch(b + 1, 1 - cur)

    starts = [window_start(e, b) for e in range(N_EXPERTS)]
    for e in range(N_EXPERTS):
        window_copy(e, starts[e], cur).wait()

    width = N_EXPERTS * rw
    ex = (lax.broadcasted_iota(I32, (N_EXPERTS, width), 0)
          == lax.broadcasted_iota(I32, (N_EXPERTS, width), 1) // rw).astype(BF16)
    lane16 = lax.broadcasted_iota(I32, (1, N_EXPERTS), 1)
    st_vec = jnp.zeros((1, N_EXPERTS), I32)
    for e in range(N_EXPERTS):
        st_vec = jnp.where(lane16 == e, starts[e], st_vec)
    rel = jnp.clip(slot_ref[...] - st_vec, -1, rw).astype(F32).astype(BF16)
    row = (lax.broadcasted_iota(I32, (tb, width), 1) % rw).astype(F32)
    hit = jnp.dot(rel, ex, preferred_element_type=F32) == row
    onehot = jnp.where(hit, 1.0, 0.0).astype(BF16)
    acc_ref[...] = x1_ref[...] + jnp.dot(onehot, buf[cur], preferred_element_type=F32)

    xlane = lax.broadcasted_iota(I32, (tb, xr), 1)
    for e in range(N_EXPERTS):
        done = starts[e] + rw
        n_extra = (jnp.maximum(cnt_ref[e * (nb + 1) + b + 1] - done, 0) + xr - 1) // xr

        def extra(w, c, e=e, done=done):
            first = done + w * xr
            cst = pl.multiple_of(jnp.minimum(first, cap - xr), BF16_SUBLANES)
            cp = pltpu.make_async_copy(y_hbm.at[e, pl.ds(cst, xr), :], xbuf, xsem)
            cp.start()
            cp.wait()
            slot = slot_ref[:, e:e + 1]
            hit = jnp.logical_and(slot - cst == xlane, slot >= first)
            onehot = jnp.where(hit, 1.0, 0.0).astype(BF16)
            acc_ref[...] += jnp.dot(onehot, xbuf[...], preferred_element_type=F32)
            return c

        lax.fori_loop(0, n_extra, extra, 0)

    o_ref[...] = _rms(acc_ref[...], fn_ref[...], RMS_EPS)


def _combine(cnt, x1, slot_t, fn, y, tb):
    m = x1.shape[0]
    nb = m // tb
    cap = y.shape[1]
    rw = LANES // 2
    assert cap % BF16_SUBLANES == 0 and cap >= LANES
    return pl.pallas_call(
        functools.partial(_combine_kernel, nb=nb, cap=cap, rw=rw),
        out_shape=jax.ShapeDtypeStruct((m, D_MODEL), F32),
        grid_spec=pltpu.PrefetchScalarGridSpec(
            num_scalar_prefetch=1,
            grid=(nb,),
            in_specs=[pl.BlockSpec((tb, D_MODEL), lambda b, c: (b, 0)),
                      pl.BlockSpec((tb, N_EXPERTS), lambda b, c: (b, 0)),
                      pl.BlockSpec((1, D_MODEL), lambda b, c: (0, 0)),
                      pl.BlockSpec(memory_space=pl.ANY)],
            out_specs=pl.BlockSpec((tb, D_MODEL), lambda b, c: (b, 0)),
            scratch_shapes=[pltpu.VMEM((2, N_EXPERTS * rw, D_MODEL), BF16),
                            pltpu.SemaphoreType.DMA((2, N_EXPERTS)),
                            pltpu.VMEM((LANES, D_MODEL), BF16),
                            pltpu.SemaphoreType.DMA(()),
                            pltpu.VMEM((tb, D_MODEL), F32)]),
        compiler_params=_params(("arbitrary",)),
        name="moe_combine",
    )(cnt, x1, slot_t, fn, y)


def _pick(n, prefs):
    for p in prefs:
        if n % p == 0:
            return p
    raise ValueError(f"no tile in {prefs} divides {n}")


def _trunk(x, tbl, sink, lam, subln_g, norm1, w_in, w_o1, w_o2, norm2, wr, wg, wu, wd, fn):
    bsz, s, d = x.shape
    m = bsz * s
    cap = CAPACITY_FACTOR * m // N_EXPERTS
    x2 = x.reshape(m, d)
    proj = _inproj(x2, norm1, w_in, _pick(m, (1024, 512, 256)))
    proj3 = proj.reshape(bsz, s, IN_COLS)
    out_a = _diff_attention(proj3, tbl, *lam, subln_g, _pick(s, (512, 256)))
    out_b = _windowed_gqa(proj3, tbl, sink, _pick(s, (256,)))
    x1, h2, aff = _oproj(out_a.reshape(m, DA_V), out_b.reshape(m, WA_Q), x2, w_o1, w_o2,
                         norm2, wr, _pick(m, (1024, 512, 256)))
    slot, excl = _select(aff.T, cap)
    def block_counts(tb):
        return jnp.concatenate([excl[:, ::tb], jnp.full((N_EXPERTS, 1), cap, I32)], axis=1).reshape(-1)

    tb_gather = _pick(m, (512, 256))
    tb_combine = 256
    xg = _gather(block_counts(tb_gather), h2, slot, cap, tb_gather)
    y = _ffn(xg, wg, wu, wd, cap, _pick(cap, (1024, 512, 256, 128)))
    out = _combine(block_counts(tb_combine), x1, slot.T, fn, y, tb_combine)
    return out.reshape(bsz, s, d)


def kernel(x_prompt, x_sample, rel_bias, norm1, w_in, lam_q1, lam_k1, lam_q2, lam_k2, subln_g, sink,
           w_o, norm2, w_router, w_gate, w_up, w_down, final_norm):
    col = jnp.arange(IN_COLS)
    is_q = (col < DA_HEADS * 2 * HEAD_DIM) | ((col >= 3 * DA_V) & (col < 3 * DA_V + WA_Q))
    col_scale = jnp.where(is_q, HEAD_DIM ** -0.5 * LOG2E, 1.0).astype(F32)
    w_in_s = (w_in[0] * col_scale).astype(BF16)
    args = dict(
        tbl=rel_bias, sink=sink,
        lam=(lam_q1, lam_k1, lam_q2, lam_k2), subln_g=subln_g,
        norm1=norm1, w_in=w_in_s,
        w_o1=w_o[0, :DA_V].astype(BF16), w_o2=w_o[0, DA_V:].astype(BF16),
        norm2=norm2, wr=w_router[0],
        wg=w_gate[0], wu=w_up[0], wd=w_down[0],
        fn=final_norm.reshape(1, D_MODEL))
    return (_trunk(x_prompt, **args), _trunk(x_sample, **args))
```

```python
import functools
import math

import jax
import jax.numpy as jnp
from jax import lax
from jax.experimental import pallas as pl
from jax.experimental.pallas import tpu as pltpu

F32 = jnp.float32
BF16 = jnp.bfloat16
I32 = jnp.int32

D_MODEL = 1024
HEAD_DIM = 64
DA_HEADS = 4
WA_Q_HEADS = 8
WA_REP = 4
WINDOW = 128
IN_COLS = 2304
DA_V = 512
WA_Q = 512
N_BUCKETS = 32
MAX_DISTANCE = 128
N_EXPERTS = 16
EXPERT_FF = 1024
CAPACITY_FACTOR = 2
RMS_EPS = 1e-6
SUBLN_EPS = 1e-5
LAMBDA_INIT = 0.8 - 0.6 * math.exp(-0.3 * 0)
LOG2E = math.log2(math.e)
DA_NEAR = 4

LANES = 128
D_ROUTED = D_MODEL + LANES
BF16_SUBLANES = 16
VMEM_LIMIT = 56 * 1024 * 1024
NT_DIMS = (((1,), (1,)), ((), ()))


def _params(sem, vmem=VMEM_LIMIT):
    return pltpu.CompilerParams(dimension_semantics=sem, vmem_limit_bytes=vmem)


def _rms(x, g, eps):
    return x * lax.rsqrt(jnp.mean(x * x, axis=-1, keepdims=True) + eps) * g


def _inproj_kernel(x_ref, g_ref, w_ref, o_ref):
    h = _rms(x_ref[...], g_ref[...], RMS_EPS).astype(BF16)
    o_ref[...] = jnp.dot(h, w_ref[...], preferred_element_type=F32).astype(o_ref.dtype)


def _inproj(x2, g, w, tm):
    m = x2.shape[0]
    return pl.pallas_call(
        _inproj_kernel,
        out_shape=jax.ShapeDtypeStruct((m, IN_COLS), BF16),
        grid=(m // tm,),
        in_specs=[pl.BlockSpec((tm, D_MODEL), lambda i: (i, 0)),
                  pl.BlockSpec((1, D_MODEL), lambda i: (0, 0)),
                  pl.BlockSpec((D_MODEL, IN_COLS), lambda i: (0, 0))],
        out_specs=pl.BlockSpec((tm, IN_COLS), lambda i: (i, 0)),
        compiler_params=_params(("parallel",)),
        name="inproj",
    )(x2, g, w)


def _t5_bucket(rel):
    nb = N_BUCKETS // 2
    max_exact = nb // 2
    ret = jnp.where(rel > 0, nb, 0)
    n = jnp.abs(rel)
    nf = jnp.maximum(n, 1).astype(F32)
    frac = jnp.log2(nf / max_exact) * ((nb - max_exact) / math.log2(MAX_DISTANCE / max_exact))
    large = max_exact + jnp.where(frac >= 0, jnp.floor(frac), jnp.ceil(frac)).astype(I32)
    large = jnp.minimum(large, nb - 1)
    return ret + jnp.where(n < max_exact, n, large)


def _bias_lookup(bucket, tbl_ref, col):
    out = jnp.zeros(bucket.shape, F32)
    for b in range(N_BUCKETS):
        out = jnp.where(bucket == b, tbl_ref[b, col], out)
    return out


def _toeplitz(row, tq, tk):
    width = tq + tk
    rb = jnp.broadcast_to(row, (tq, width))
    rolled = pltpu.roll(rb, width - tq + 1, 1, stride=1, stride_axis=0)
    return rolled[:, :tk]


def _da_kernel(tbl_ref, lq1_ref, lk1_ref, lq2_ref, lk2_ref, q_ref, k_ref, v_ref, g_ref,
               o_ref, qm_ref, m_ref, l_ref, acc_ref, bias_ref, vext_ref, *, t, seq):
    h = pl.program_id(1)
    qi = pl.program_id(2)
    nk = seq // t

    @pl.when(qi == 0)
    def _():
        vext_ref[:, :LANES] = v_ref[0]
        vext_ref[:, LANES:] = jnp.ones((seq, LANES), BF16)
        for d in range(2 * DA_NEAR - 1):
            jj = lax.broadcasted_iota(I32, (1, 2 * t), 1)
            rel = jnp.clip((d - (DA_NEAR - 1)) * t + jj - (t - 1), -MAX_DISTANCE, MAX_DISTANCE)
            row = (_bias_lookup(_t5_bucket(rel), tbl_ref, h) - tbl_ref[N_BUCKETS // 2 - 1, h]) * LOG2E
            bias_ref[d] = _toeplitz(row, t, t)

    c_left = tbl_ref[N_BUCKETS // 2 - 1, h] * LOG2E
    c_right = tbl_ref[N_BUCKETS - 1, h] * LOG2E

    q = q_ref[0]
    lane = lax.broadcasted_iota(I32, q.shape, 1)
    qm_ref[0] = jnp.where(lane < HEAD_DIM, q, jnp.zeros_like(q))
    qm_ref[1] = jnp.where(lane >= HEAD_DIM, q, jnp.zeros_like(q))
    m_ref[...] = jnp.full(m_ref.shape, -jnp.inf, F32)
    l_ref[...] = jnp.zeros(l_ref.shape, F32)
    acc_ref[...] = jnp.zeros(acc_ref.shape, F32)

    def step(tile, width, bias_idx, mxu_sums=False):
        start = pl.multiple_of(tile * t, t)
        k = k_ref[0, pl.ds(start, width), :]
        v = vext_ref[pl.ds(start, width), :] if mxu_sums else v_ref[0, pl.ds(start, width), :]
        reps = width // LANES
        logits = [lax.dot_general(qm_ref[mp], k, NT_DIMS, preferred_element_type=F32) for mp in range(2)]
        for mp in range(2):
            s = logits[mp]
            if bias_idx is not None:
                s = s + jnp.concatenate([bias_ref[bias_idx + i] for i in range(width // t)], axis=1)
            m_prev = m_ref[mp]
            m_new = jnp.maximum(m_prev, jnp.max(s, axis=1, keepdims=True))
            alpha = jnp.exp2(m_prev - m_new)
            p = jnp.exp2(s - jnp.tile(m_new, (1, reps)))
            pv = jnp.dot(p.astype(BF16), v, preferred_element_type=F32)
            if mxu_sums:
                psum = pv[:, LANES:] * (1.0 / LANES)
                pv = pv[:, :LANES]
            else:
                psum = p[:, :LANES]
                for r in range(1, reps):
                    psum = psum + p[:, r * LANES:(r + 1) * LANES]
            l_ref[mp] = alpha * l_ref[mp] + psum
            acc_ref[mp] = alpha * acc_ref[mp] + pv
            m_ref[mp] = m_new

    def far(first, count):
        def quad(j, c):
            step(first + 4 * j, 4 * t, None, mxu_sums=True)
            return c

        rest = first + (count // 4) * 4
        left = count % 4

        def triple(j, c):
            step(rest, 3 * t, None, mxu_sums=True)
            return c

        def pair(j, c):
            step(rest, 2 * t, None, mxu_sums=True)
            return c

        def single(j, c):
            step(rest, t, None)
            return c

        lax.fori_loop(0, count // 4, quad, 0)
        lax.fori_loop(0, (left == 3).astype(I32), triple, 0)
        lax.fori_loop(0, (left == 2).astype(I32), pair, 0)
        lax.fori_loop(0, (left == 1).astype(I32), single, 0)

    first_near = jnp.clip(qi - 1, 0, nk - DA_NEAR)
    first_right = first_near + DA_NEAR
    far(0, first_near)
    step(first_near, DA_NEAR * t, first_near - qi + DA_NEAR - 1)
    m_ref[...] += c_left - c_right
    far(first_right, nk - first_right)

    outs = [acc_ref[mp] / jnp.sum(l_ref[mp], axis=1, keepdims=True) for mp in range(2)]
    lam = (jnp.exp(jnp.sum(lq1_ref[...] * lk1_ref[...], axis=1, keepdims=True))
           - jnp.exp(jnp.sum(lq2_ref[...] * lk2_ref[...], axis=1, keepdims=True))
           + LAMBDA_INIT)
    o = outs[0] - lam * outs[1]
    o = _rms(o, g_ref[...], SUBLN_EPS) * (1.0 - LAMBDA_INIT)
    o_ref[0] = o.astype(o_ref.dtype)


def _diff_attention(proj3, tbl, lq1, lk1, lq2, lk2, subln_g, t):
    b, s, _ = proj3.shape
    assert t >= MAX_DISTANCE and s % t == 0 and s // t >= DA_NEAR
    vec = lambda n: pl.BlockSpec((1, n), lambda bi, hi, qi: (0, 0))
    kern = functools.partial(_da_kernel, t=t, seq=s)
    return pl.pallas_call(
        kern,
        out_shape=jax.ShapeDtypeStruct((b, s, DA_V), BF16),
        grid=(b, DA_HEADS, s // t),
        in_specs=[pl.BlockSpec(memory_space=pltpu.SMEM),
                  vec(HEAD_DIM), vec(HEAD_DIM), vec(HEAD_DIM), vec(HEAD_DIM),
                  pl.BlockSpec((1, t, LANES), lambda bi, hi, qi: (bi, qi, hi)),
                  pl.BlockSpec((1, s, LANES), lambda bi, hi, qi: (bi, 0, DA_HEADS + hi)),
                  pl.BlockSpec((1, s, LANES), lambda bi, hi, qi: (bi, 0, 2 * DA_HEADS + hi)),
                  vec(2 * HEAD_DIM)],
        out_specs=pl.BlockSpec((1, t, LANES), lambda bi, hi, qi: (bi, qi, hi)),
        scratch_shapes=[pltpu.VMEM((2, t, LANES), BF16), pltpu.VMEM((2, t, LANES), F32),
                        pltpu.VMEM((2, t, LANES), F32), pltpu.VMEM((2, t, 2 * HEAD_DIM), F32),
                        pltpu.VMEM((2 * DA_NEAR - 1, t, t), F32), pltpu.VMEM((s, 2 * LANES), BF16)],
        compiler_params=_params(("parallel", "parallel", "arbitrary")),
        name="diff_attention",
    )(tbl, lq1, lk1, lq2, lk2, proj3, proj3, proj3, subln_g)


def _wa_kernel(tbl_ref, sink_ref, q_ref, k_ref, v_ref, o_ref, bias_ref, *, tq, seq):
    q0 = pl.program_id(1) * tq
    win = tq + 2 * WINDOW
    ws = pl.multiple_of(jnp.clip(q0 - WINDOW, 0, seq - win), LANES)

    @pl.when(jnp.logical_and(pl.program_id(0) == 0, pl.program_id(1) == 0))
    def _():
        jj = lax.broadcasted_iota(I32, (1, win + tq), 1)
        for d in range(3):
            rel = jj - d * WINDOW - (tq - 1)
            valid = jnp.abs(rel) <= WINDOW
            bucket = _t5_bucket(jnp.clip(rel, -MAX_DISTANCE, MAX_DISTANCE))
            for hq in range(WA_Q_HEADS):
                row = jnp.where(valid, _bias_lookup(bucket, tbl_ref, DA_HEADS + hq) * LOG2E, -jnp.inf)
                bias_ref[d, hq] = _toeplitz(row, tq, win)

    variant = (q0 - ws) // WINDOW
    kw = k_ref[0, pl.ds(ws, win), :]
    vw = v_ref[0, pl.ds(ws, win), :]
    q = q_ref[0]
    low = lax.broadcasted_iota(I32, (tq, LANES), 1) < HEAD_DIM
    kdup = [jnp.concatenate([kw[:, g * HEAD_DIM:(g + 1) * HEAD_DIM]] * 2, axis=1) for g in range(2)]
    ones = jnp.ones((win, LANES), BF16)
    vext = [jnp.concatenate([vw[:, g * HEAD_DIM:(g + 1) * HEAD_DIM]] * 2 + [ones], axis=1) for g in range(2)]
    logits = []
    for hq in range(WA_Q_HEADS):
        qp = q[:, (hq // 2) * LANES:(hq // 2 + 1) * LANES]
        qm = jnp.where(low if hq % 2 == 0 else jnp.logical_not(low), qp, jnp.zeros_like(qp))
        logits.append(lax.dot_general(qm, kdup[hq // WA_REP], NT_DIMS, preferred_element_type=F32))
    outs = []
    for pair in range(WA_Q_HEADS // 2):
        g = (2 * pair) // WA_REP
        halves = []
        for half in range(2):
            hq = 2 * pair + half
            s = logits[hq] + bias_ref[variant, hq]
            sk = sink_ref[0, hq] * LOG2E
            m = jnp.maximum(jnp.full((tq, LANES), sk, F32), jnp.max(s, axis=1, keepdims=True))
            e = jnp.exp2(s - jnp.tile(m, (1, win // LANES))).astype(BF16)
            pv = jnp.dot(e, vext[g], preferred_element_type=F32)
            halves.append(pv[:, :LANES] / (pv[:, LANES:] + jnp.exp2(sk - m)))
        outs.append(jnp.where(low, halves[0], halves[1]))
    o_ref[0] = jnp.concatenate(outs, axis=1).astype(o_ref.dtype)


def _windowed_gqa(proj3, tbl, sink, tq):
    b, s, _ = proj3.shape
    kern = functools.partial(_wa_kernel, tq=tq, seq=s)
    kv_blk = (IN_COLS - 2 * LANES) // LANES
    return pl.pallas_call(
        kern,
        out_shape=jax.ShapeDtypeStruct((b, s, WA_Q), BF16),
        grid=(b, s // tq),
        in_specs=[pl.BlockSpec(memory_space=pltpu.SMEM),
                  pl.BlockSpec(memory_space=pltpu.SMEM),
                  pl.BlockSpec((1, tq, WA_Q), lambda bi, qi: (bi, qi, 3)),
                  pl.BlockSpec((1, s, LANES), lambda bi, qi: (bi, 0, kv_blk)),
                  pl.BlockSpec((1, s, LANES), lambda bi, qi: (bi, 0, kv_blk + 1))],
        out_specs=pl.BlockSpec((1, tq, WA_Q), lambda bi, qi: (bi, qi, 0)),
        scratch_shapes=[pltpu.VMEM((3, WA_Q_HEADS, tq, tq + 2 * WINDOW), F32)],
        compiler_params=_params(("arbitrary", "arbitrary")),
        name="windowed_gqa",
    )(tbl, sink, proj3, proj3, proj3)


def _oproj_kernel(a_ref, b_ref, x_ref, w1_ref, w2_ref, g_ref, wr_ref, x1_ref, h_ref, aff_ref):
    x1 = (x_ref[...] + jnp.dot(a_ref[...], w1_ref[...], preferred_element_type=F32)
          + jnp.dot(b_ref[...], w2_ref[...], preferred_element_type=F32))
    x1_ref[...] = x1
    h = _rms(x1, g_ref[...], RMS_EPS)
    h_hi = h.astype(BF16)
    h_ref[:, :D_MODEL] = h_hi
    h_lo = (h - h_hi.astype(F32)).astype(BF16)
    wr = wr_ref[...]
    w_hi = wr.astype(BF16)
    w_lo = (wr - w_hi.astype(F32)).astype(BF16)
    hh = jnp.dot(h_hi, jnp.concatenate([w_hi, w_lo], axis=1), preferred_element_type=F32)
    logits = (hh[:, :N_EXPERTS] + hh[:, N_EXPERTS:]
              + jnp.dot(h_lo, w_hi, preferred_element_type=F32))
    e = jnp.exp(logits - jnp.max(logits, axis=1, keepdims=True))
    aff = e / jnp.sum(e, axis=1, keepdims=True)
    aff_ref[...] = aff
    a_hi = aff.astype(BF16)
    a_lo = (aff - a_hi.astype(F32)).astype(BF16)
    pad = jnp.zeros((aff.shape[0], LANES - 2 * N_EXPERTS), BF16)
    h_ref[:, D_MODEL:] = jnp.concatenate([a_hi, a_lo, pad], axis=1)


def _oproj(a, b, x2, w1, w2, g, wr, tm):
    m = x2.shape[0]
    row = lambda n: pl.BlockSpec((tm, n), lambda i: (i, 0))
    full = lambda r, c: pl.BlockSpec((r, c), lambda i: (0, 0))
    return pl.pallas_call(
        _oproj_kernel,
        out_shape=(jax.ShapeDtypeStruct((m, D_MODEL), F32),
                   jax.ShapeDtypeStruct((m, D_ROUTED), BF16),
                   jax.ShapeDtypeStruct((m, N_EXPERTS), F32)),
        grid=(m // tm,),
        in_specs=[row(DA_V), row(WA_Q), row(D_MODEL), full(DA_V, D_MODEL), full(WA_Q, D_MODEL),
                  full(1, D_MODEL), full(D_MODEL, N_EXPERTS)],
        out_specs=(row(D_MODEL), row(D_ROUTED), row(N_EXPERTS)),
        compiler_params=_params(("parallel",)),
        name="oproj_router",
    )(a, b, x2, w1, w2, g, wr)


def _cumsum_lanes(x):
    n = x.shape[1]
    lane = lax.broadcasted_iota(I32, x.shape, 1)
    shift = 1
    while shift < n:
        x = x + jnp.where(lane >= shift, pltpu.roll(x, shift, 1), 0)
        shift *= 2
    return x


def _select_kernel(aff_ref, slot_ref, excl_ref, *, cap):
    aff = aff_ref[...]

    def count(mask):
        return jnp.sum(mask.astype(F32), axis=1, keepdims=True)

    def body(i, prefix):
        cand = prefix | jnp.left_shift(jnp.int32(1), 30 - i)
        return jnp.where(count(aff >= pltpu.bitcast(cand, F32)) >= cap, cand, prefix)

    thr = lax.fori_loop(0, 31, body, jnp.zeros((aff.shape[0], 1), I32))
    above = aff >= pltpu.bitcast(thr + 1, F32)
    edge = jnp.logical_and(aff >= pltpu.bitcast(thr, F32), jnp.logical_not(above))
    need = cap - count(above)
    edge_i = edge.astype(I32)
    edge_excl = (_cumsum_lanes(edge_i) - edge_i).astype(F32)
    sel = above | (edge & (edge_excl < need))
    sel_i = sel.astype(I32)
    excl = _cumsum_lanes(sel_i) - sel_i
    slot_ref[...] = jnp.where(sel, excl, -1)
    excl_ref[...] = excl


def _select(aff_t, cap):
    shp = jax.ShapeDtypeStruct(aff_t.shape, I32)
    return pl.pallas_call(
        functools.partial(_select_kernel, cap=cap),
        out_shape=(shp, shp),
        compiler_params=_params(None),
        name="expert_select",
    )(aff_t)


def _gather_kernel(cnt_ref, x_ref, slot_ref, xg_hbm, obuf, sem, carry_ref, xo, xsem, *, nb, rw):
    b = pl.program_id(0)
    tb = x_ref.shape[0]
    cur = b % 2
    grp = BF16_SUBLANES
    groups = rw // grp

    def lo_of(e, blk):
        return cnt_ref[e * (nb + 1) + blk]

    def base_of(e, blk):
        return pl.multiple_of((lo_of(e, blk) // grp) * grp, grp)

    def out_copy(e, blk, slab):
        return pltpu.make_async_copy(obuf.at[slab, pl.ds(e * rw, rw), :],
                                     xg_hbm.at[e, pl.ds(base_of(e, blk), rw), :], sem.at[slab, e])

    def pack(rel):
        rows = lax.broadcasted_iota(I32, (rw, tb), 0)
        return jnp.where(rel == rows, 1.0, 0.0).astype(BF16)

    @pl.when(b == 0)
    def _():
        carry_ref[...] = jnp.zeros(carry_ref.shape, carry_ref.dtype)
        xo[...] = jnp.zeros(xo.shape, xo.dtype)
        for e in range(N_EXPERTS):
            cp = pltpu.make_async_copy(xo, xg_hbm.at[e, pl.ds(xg_hbm.shape[1] - rw, rw), :], xsem)
            cp.start()
            cp.wait()

    x = x_ref[...]
    onehot = jnp.concatenate([pack(slot_ref[e:e + 1, :] - base_of(e, b)) for e in range(N_EXPERTS)], axis=0)
    obuf[cur] = jnp.dot(onehot, x, preferred_element_type=F32).astype(BF16)
    for e in range(N_EXPERTS):
        obuf[cur, pl.ds(e * rw, grp), :] += carry_ref[e]
        g_hi = (lo_of(e, b + 1) - base_of(e, b)) // grp
        row0 = pl.multiple_of(e * rw + jnp.minimum(g_hi, groups - 1) * grp, grp)
        kept = obuf[cur, pl.ds(row0, grp), :]
        carry_ref[e] = jnp.where(g_hi < groups, kept, jnp.zeros_like(kept))

    @pl.when(b > 0)
    def _():
        for e in range(N_EXPERTS):
            out_copy(e, b - 1, 1 - cur).wait()

    for e in range(N_EXPERTS):
        out_copy(e, b, cur).start()

    for e in range(N_EXPERTS):
        base = base_of(e, b)
        hi = lo_of(e, b + 1)
        n_extra = (jnp.maximum(hi - base - rw, 0) + rw - 1) // rw

        def extra(w, c, e=e, base=base, hi=hi):
            first = pl.multiple_of(base + (w + 1) * rw, grp)
            xo[...] = jnp.dot(pack(slot_ref[e:e + 1, :] - first), x, preferred_element_type=F32).astype(BF16)
            cp = pltpu.make_async_copy(xo, xg_hbm.at[e, pl.ds(first, rw), :], xsem)
            cp.start()
            cp.wait()
            g_hi = (hi - first) // grp
            row0 = pl.multiple_of(jnp.clip(g_hi, 0, groups - 1) * grp, grp)
            carry_ref[e] = jnp.where(jnp.logical_and(g_hi >= 0, g_hi < groups),
                                     xo[pl.ds(row0, grp), :], carry_ref[e])
            return c

        lax.fori_loop(0, n_extra, extra, 0)

    @pl.when(b == nb - 1)
    def _():
        for e in range(N_EXPERTS):
            out_copy(e, b, cur).wait()


def _gather(cnt, h2, slot, cap, tb):
    m, width = h2.shape
    nb = m // tb
    rw = (tb // 8 + max(tb // 16, 32) + BF16_SUBLANES) // BF16_SUBLANES * BF16_SUBLANES
    assert cap % BF16_SUBLANES == 0
    return pl.pallas_call(
        functools.partial(_gather_kernel, nb=nb, rw=rw),
        out_shape=jax.ShapeDtypeStruct((N_EXPERTS, cap + rw, width), BF16),
        grid_spec=pltpu.PrefetchScalarGridSpec(
            num_scalar_prefetch=1,
            grid=(nb,),
            in_specs=[pl.BlockSpec((tb, width), lambda b, c: (b, 0)),
                      pl.BlockSpec((N_EXPERTS, tb), lambda b, c: (0, b))],
            out_specs=pl.BlockSpec(memory_space=pl.ANY),
            scratch_shapes=[pltpu.VMEM((2, N_EXPERTS * rw, width), BF16),
                            pltpu.SemaphoreType.DMA((2, N_EXPERTS)),
                            pltpu.VMEM((N_EXPERTS, BF16_SUBLANES, width), BF16),
                            pltpu.VMEM((rw, width), BF16),
                            pltpu.SemaphoreType.DMA(())]),
        compiler_params=_params(("arbitrary",)),
        name="moe_gather",
    )(cnt, h2, slot)


def _ffn_kernel(x_ref, wg_ref, wu_ref, wd_ref, y_ref, wg_s, wu_s, wd_s):
    @pl.when(pl.program_id(1) == 0)
    def _():
        wg_s[...] = wg_ref[0].astype(BF16)
        wu_s[...] = wu_ref[0].astype(BF16)
        wd_s[...] = wd_ref[0].astype(BF16)

    x = x_ref[0, :, :D_MODEL]
    tail = x_ref[0, :, D_MODEL:].astype(F32)
    lane = lax.broadcasted_iota(I32, tail.shape, 1)
    e = pl.program_id(0)
    mine = jnp.logical_or(lane == e, lane == e + N_EXPERTS)
    gate = jnp.sum(jnp.where(mine, tail, 0.0), axis=1, keepdims=True)
    g = jnp.dot(x, wg_s[...], preferred_element_type=F32)
    u = jnp.dot(x, wu_s[...], preferred_element_type=F32)
    h = (g * jax.nn.sigmoid(g) * u).astype(BF16)
    y_ref[0] = (jnp.dot(h, wd_s[...], preferred_element_type=F32) * gate).astype(y_ref.dtype)


def _ffn(xg, wg, wu, wd, cap, tm):
    wspec = lambda r, c: pl.BlockSpec((1, r, c), lambda e, i: (e, 0, 0))
    return pl.pallas_call(
        _ffn_kernel,
        out_shape=jax.ShapeDtypeStruct((N_EXPERTS, cap, D_MODEL), BF16),
        grid=(N_EXPERTS, cap // tm),
        in_specs=[pl.BlockSpec((1, tm, D_ROUTED), lambda e, i: (e, i, 0)),
                  wspec(D_MODEL, EXPERT_FF), wspec(D_MODEL, EXPERT_FF), wspec(EXPERT_FF, D_MODEL)],
        out_specs=pl.BlockSpec((1, tm, D_MODEL), lambda e, i: (e, i, 0)),
        scratch_shapes=[pltpu.VMEM((D_MODEL, EXPERT_FF), BF16), pltpu.VMEM((D_MODEL, EXPERT_FF), BF16),
                        pltpu.VMEM((EXPERT_FF, D_MODEL), BF16)],
        compiler_params=_params(("arbitrary", "arbitrary")),
        name="moe_ffn",
    )(xg, wg, wu, wd)


def _combine_kernel(cnt_ref, x1_ref, slot_ref, fn_ref, y_hbm, o_ref,
                    buf, sem, xbuf, xsem, acc_ref, *, nb, cap, rw):
    b = pl.program_id(0)
    tb = x1_ref.shape[0]
    xr = xbuf.shape[0]

    def window_start(e, blk):
        lo = cnt_ref[e * (nb + 1) + blk]
        return pl.multiple_of(jnp.minimum((lo // BF16_SUBLANES) * BF16_SUBLANES, cap - rw), BF16_SUBLANES)

    def window_copy(e, start, slab):
        return pltpu.make_async_copy(y_hbm.at[e, pl.ds(start, rw), :],
                                     buf.at[slab, pl.ds(e * rw, rw), :], sem.at[slab, e])

    def fetch(blk, slab):
        for e in range(N_EXPERTS):
            window_copy(e, window_start(e, blk), slab).start()

    cur = b % 2

    @pl.when(b == 0)
    def _():
        fetch(0, 0)

    @pl.when(b + 1 < nb)
    def _():
        fetch(b + 1, 1 - cur)

    starts = [window_start(e, b) for e in range(N_EXPERTS)]
    for e in range(N_EXPERTS):
        window_copy(e, starts[e], cur).wait()

    width = N_EXPERTS * rw
    ex = (lax.broadcasted_iota(I32, (N_EXPERTS, width), 0)
          == lax.broadcasted_iota(I32, (N_EXPERTS, width), 1) // rw).astype(BF16)
    lane16 = lax.broadcasted_iota(I32, (1, N_EXPERTS), 1)
    st_vec = jnp.zeros((1, N_EXPERTS), I32)
    for e in range(N_EXPERTS):
        st_vec = jnp.where(lane16 == e, starts[e], st_vec)
    rel = jnp.clip(slot_ref[...] - st_vec, -1, rw).astype(F32).astype(BF16)
    row = (lax.broadcasted_iota(I32, (tb, width), 1) % rw).astype(F32)
    hit = jnp.dot(rel, ex, preferred_element_type=F32) == row
    onehot = jnp.where(hit, 1.0, 0.0).astype(BF16)
    acc_ref[...] = x1_ref[...] + jnp.dot(onehot, buf[cur], preferred_element_type=F32)

    xlane = lax.broadcasted_iota(I32, (tb, xr), 1)
    for e in range(N_EXPERTS):
        done = starts[e] + rw
        n_extra = (jnp.maximum(cnt_ref[e * (nb + 1) + b + 1] - done, 0) + xr - 1) // xr

        def extra(w, c, e=e, done=done):
            first = done + w * xr
            cst = pl.multiple_of(jnp.minimum(first, cap - xr), BF16_SUBLANES)
            cp = pltpu.make_async_copy(y_hbm.at[e, pl.ds(cst, xr), :], xbuf, xsem)
            cp.start()
            cp.wait()
            slot = slot_ref[:, e:e + 1]
            hit = jnp.logical_and(slot - cst == xlane, slot >= first)
            onehot = jnp.where(hit, 1.0, 0.0).astype(BF16)
            acc_ref[...] += jnp.dot(onehot, xbuf[...], preferred_element_type=F32)
            return c

        lax.fori_loop(0, n_extra, extra, 0)

    o_ref[...] = _rms(acc_ref[...], fn_ref[...], RMS_EPS)


def _combine(cnt, x1, slot_t, fn, y, tb):
    m = x1.shape[0]
    nb = m // tb
    cap = y.shape[1]
    rw = LANES // 2
    assert cap % BF16_SUBLANES == 0 and cap >= LANES
    return pl.pallas_call(
        functools.partial(_combine_kernel, nb=nb, cap=cap, rw=rw),
        out_shape=jax.ShapeDtypeStruct((m, D_MODEL), F32),
        grid_spec=pltpu.PrefetchScalarGridSpec(
            num_scalar_prefetch=1,
            grid=(nb,),
            in_specs=[pl.BlockSpec((tb, D_MODEL), lambda b, c: (b, 0)),
                      pl.BlockSpec((tb, N_EXPERTS), lambda b, c: (b, 0)),
                      pl.BlockSpec((1, D_MODEL), lambda b, c: (0, 0)),
                      pl.BlockSpec(memory_space=pl.ANY)],
            out_specs=pl.BlockSpec((tb, D_MODEL), lambda b, c: (b, 0)),
            scratch_shapes=[pltpu.VMEM((2, N_EXPERTS * rw, D_MODEL), BF16),
                            pltpu.SemaphoreType.DMA((2, N_EXPERTS)),
                            pltpu.VMEM((LANES, D_MODEL), BF16),
                            pltpu.SemaphoreType.DMA(()),
                            pltpu.VMEM((tb, D_MODEL), F32)]),
        compiler_params=_params(("arbitrary",)),
        name="moe_combine",
    )(cnt, x1, slot_t, fn, y)


def _pick(n, prefs):
    for p in prefs:
        if n % p == 0:
            return p
    raise ValueError(f"no tile in {prefs} divides {n}")


def _trunk(x, tbl, sink, lam, subln_g, norm1, w_in, w_o1, w_o2, norm2, wr, wg, wu, wd, fn):
    bsz, s, d = x.shape
    m = bsz * s
    cap = CAPACITY_FACTOR * m // N_EXPERTS
    x2 = x.reshape(m, d)
    proj = _inproj(x2, norm1, w_in, _pick(m, (1024, 512, 256)))
    proj3 = proj.reshape(bsz, s, IN_COLS)
    out_a = _diff_attention(proj3, tbl, *lam, subln_g, _pick(s, (512, 256)))
    out_b = _windowed_gqa(proj3, tbl, sink, _pick(s, (256,)))
    x1, h2, aff = _oproj(out_a.reshape(m, DA_V), out_b.reshape(m, WA_Q), x2, w_o1, w_o2,
                         norm2, wr, _pick(m, (1024, 512, 256)))
    slot, excl = _select(aff.T, cap)
    def block_counts(tb):
        return jnp.concatenate([excl[:, ::tb], jnp.full((N_EXPERTS, 1), cap, I32)], axis=1).reshape(-1)

    tb_gather = _pick(m, (512, 256))
    tb_combine = 256
    xg = _gather(block_counts(tb_gather), h2, slot, cap, tb_gather)
    y = _ffn(xg, wg, wu, wd, cap, _pick(cap, (1024, 512, 256, 128)))
    out = _combine(block_counts(tb_combine), x1, slot.T, fn, y, tb_combine)
    return out.reshape(bsz, s, d)


def kernel(x_prompt, x_sample, rel_bias, norm1, w_in, lam_q1, lam_k1, lam_q2, lam_k2, subln_g, sink,
           w_o, norm2, w_router, w_gate, w_up, w_down, final_norm):
    col = jnp.arange(IN_COLS)
    is_q = (col < DA_HEADS * 2 * HEAD_DIM) | ((col >= 3 * DA_V) & (col < 3 * DA_V + WA_Q))
    col_scale = jnp.where(is_q, HEAD_DIM ** -0.5 * LOG2E, 1.0).astype(F32)
    w_in_s = (w_in[0] * col_scale).astype(BF16)
    args = dict(
        tbl=rel_bias, sink=sink,
        lam=(lam_q1, lam_k1, lam_q2, lam_k2), subln_g=subln_g,
        norm1=norm1, w_in=w_in_s,
        w_o1=w_o[0, :DA_V].astype(BF16), w_o2=w_o[0, DA_V:].astype(BF16),
        norm2=norm2, wr=w_router[0],
        wg=w_gate[0], wu=w_up[0], wd=w_down[0],
        fn=final_norm.reshape(1, D_MODEL))
    return (_trunk(x_prompt, **args), _trunk(x_sample, **args))
```

```python
import functools
import math

import jax
import jax.numpy as jnp
from jax import lax
from jax.experimental import pallas as pl
from jax.experimental.pallas import tpu as pltpu

F32 = jnp.float32
BF16 = jnp.bfloat16
I32 = jnp.int32

D_MODEL = 1024
HEAD_DIM = 64
DA_HEADS = 4
WA_Q_HEADS = 8
WA_REP = 4
WINDOW = 128
IN_COLS = 2304
DA_V = 512
WA_Q = 512
N_BUCKETS = 32
MAX_DISTANCE = 128
N_EXPERTS = 16
EXPERT_FF = 1024
CAPACITY_FACTOR = 2
RMS_EPS = 1e-6
SUBLN_EPS = 1e-5
LAMBDA_INIT = 0.8 - 0.6 * math.exp(-0.3 * 0)
LOG2E = math.log2(math.e)
DA_NEAR = 4

LANES = 128
D_ROUTED = D_MODEL + LANES
BF16_SUBLANES = 16
VMEM_LIMIT = 56 * 1024 * 1024
NT_DIMS = (((1,), (1,)), ((), ()))


def _params(sem, vmem=VMEM_LIMIT):
    return pltpu.CompilerParams(dimension_semantics=sem, vmem_limit_bytes=vmem)


def _rms(x, g, eps):
    return x * lax.rsqrt(jnp.mean(x * x, axis=-1, keepdims=True) + eps) * g


def _inproj_kernel(x_ref, g_ref, w_ref, o_ref):
    h = _rms(x_ref[...], g_ref[...], RMS_EPS).astype(BF16)
    o_ref[...] = jnp.dot(h, w_ref[...], preferred_element_type=F32).astype(o_ref.dtype)


def _inproj(x2, g, w, tm):
    m = x2.shape[0]
    return pl.pallas_call(
        _inproj_kernel,
        out_shape=jax.ShapeDtypeStruct((m, IN_COLS), BF16),
        grid=(m // tm,),
        in_specs=[pl.BlockSpec((tm, D_MODEL), lambda i: (i, 0)),
                  pl.BlockSpec((1, D_MODEL), lambda i: (0, 0)),
                  pl.BlockSpec((D_MODEL, IN_COLS), lambda i: (0, 0))],
        out_specs=pl.BlockSpec((tm, IN_COLS), lambda i: (i, 0)),
        compiler_params=_params(("parallel",)),
        name="inproj",
    )(x2, g, w)


def _t5_bucket(rel):
    nb = N_BUCKETS // 2
    max_exact = nb // 2
    ret = jnp.where(rel > 0, nb, 0)
    n = jnp.abs(rel)
    nf = jnp.maximum(n, 1).astype(F32)
    frac = jnp.log2(nf / max_exact) * ((nb - max_exact) / math.log2(MAX_DISTANCE / max_exact))
    large = max_exact + jnp.where(frac >= 0, jnp.floor(frac), jnp.ceil(frac)).astype(I32)
    large = jnp.minimum(large, nb - 1)
    return ret + jnp.where(n < max_exact, n, large)


def _bias_lookup(bucket, tbl_ref, col):
    out = jnp.zeros(bucket.shape, F32)
    for b in range(N_BUCKETS):
        out = jnp.where(bucket == b, tbl_ref[b, col], out)
    return out


def _toeplitz(row, tq, tk):
    width = tq + tk
    rb = jnp.broadcast_to(row, (tq, width))
    rolled = pltpu.roll(rb, width - tq + 1, 1, stride=1, stride_axis=0)
    return rolled[:, :tk]


def _da_kernel(tbl_ref, lq1_ref, lk1_ref, lq2_ref, lk2_ref, q_ref, k_ref, v_ref, g_ref,
               o_ref, qm_ref, m_ref, l_ref, acc_ref, bias_ref, vext_ref, *, t, seq):
    h = pl.program_id(1)
    qi = pl.program_id(2)
    nk = seq // t

    @pl.when(qi == 0)
    def _():
        vext_ref[:, :LANES] = v_ref[0]
        vext_ref[:, LANES:] = jnp.ones((seq, LANES), BF16)
        for d in range(2 * DA_NEAR - 1):
            jj = lax.broadcasted_iota(I32, (1, 2 * t), 1)
            rel = jnp.clip((d - (DA_NEAR - 1)) * t + jj - (t - 1), -MAX_DISTANCE, MAX_DISTANCE)
            row = (_bias_lookup(_t5_bucket(rel), tbl_ref, h) - tbl_ref[N_BUCKETS // 2 - 1, h]) * LOG2E
            bias_ref[d] = _toeplitz(row, t, t)

    c_left = tbl_ref[N_BUCKETS // 2 - 1, h] * LOG2E
    c_right = tbl_ref[N_BUCKETS - 1, h] * LOG2E

    q = q_ref[0]
    lane = lax.broadcasted_iota(I32, q.shape, 1)
    qm_ref[0] = jnp.where(lane < HEAD_DIM, q, jnp.zeros_like(q))
    qm_ref[1] = jnp.where(lane >= HEAD_DIM, q, jnp.zeros_like(q))
    m_ref[...] = jnp.full(m_ref.shape, -jnp.inf, F32)
    l_ref[...] = jnp.zeros(l_ref.shape, F32)
    acc_ref[...] = jnp.zeros(acc_ref.shape, F32)

    def step(tile, width, bias_idx, mxu_sums=False):
        start = pl.multiple_of(tile * t, t)
        k = k_ref[0, pl.ds(start, width), :]
        v = vext_ref[pl.ds(start, width), :] if mxu_sums else v_ref[0, pl.ds(start, width), :]
        reps = width // LANES
        logits = [lax.dot_general(qm_ref[mp], k, NT_DIMS, preferred_element_type=F32) for mp in range(2)]
        for mp in range(2):
            s = logits[mp]
            if bias_idx is not None:
                s = s + jnp.concatenate([bias_ref[bias_idx + i] for i in range(width // t)], axis=1)
            m_prev = m_ref[mp]
            m_new = jnp.maximum(m_prev, jnp.max(s, axis=1, keepdims=True))
            alpha = jnp.exp2(m_prev - m_new)
            p = jnp.exp2(s - jnp.tile(m_new, (1, reps)))
            pv = jnp.dot(p.astype(BF16), v, preferred_element_type=F32)
            if mxu_sums:
                psum = pv[:, LANES:] * (1.0 / LANES)
                pv = pv[:, :LANES]
            else:
                psum = p[:, :LANES]
                for r in range(1, reps):
                    psum = psum + p[:, r * LANES:(r + 1) * LANES]
            l_ref[mp] = alpha * l_ref[mp] + psum
            acc_ref[mp] = alpha * acc_ref[mp] + pv
            m_ref[mp] = m_new

    def far(first, count):
        def quad(j, c):
            step(first + 4 * j, 4 * t, None, mxu_sums=True)
            return c

        rest = first + (count // 4) * 4
        left = count % 4

        def triple(j, c):
            step(rest, 3 * t, None, mxu_sums=True)
            return c

        def pair(j, c):
            step(rest, 2 * t, None, mxu_sums=True)
            return c

        def single(j, c):
            step(rest, t, None)
            return c

        lax.fori_loop(0, count // 4, quad, 0)
        lax.fori_loop(0, (left == 3).astype(I32), triple, 0)
        lax.fori_loop(0, (left == 2).astype(I32), pair, 0)
        lax.fori_loop(0, (left == 1).astype(I32), single, 0)

    first_near = jnp.clip(qi - 1, 0, nk - DA_NEAR)
    first_right = first_near + DA_NEAR
    far(0, first_near)
    step(first_near, DA_NEAR * t, first_near - qi + DA_NEAR - 1)
    m_ref[...] += c_left - c_right
    far(first_right, nk - first_right)

    outs = [acc_ref[mp] / jnp.sum(l_ref[mp], axis=1, keepdims=True) for mp in range(2)]
    lam = (jnp.exp(jnp.sum(lq1_ref[...] * lk1_ref[...], axis=1, keepdims=True))
           - jnp.exp(jnp.sum(lq2_ref[...] * lk2_ref[...], axis=1, keepdims=True))
           + LAMBDA_INIT)
    o = outs[0] - lam * outs[1]
    o = _rms(o, g_ref[...], SUBLN_EPS) * (1.0 - LAMBDA_INIT)
    o_ref[0] = o.astype(o_ref.dtype)


def _diff_attention(proj3, tbl, lq1, lk1, lq2, lk2, subln_g, t):
    b, s, _ = proj3.shape
    assert t >= MAX_DISTANCE and s % t == 0 and s // t >= DA_NEAR
    vec = lambda n: pl.BlockSpec((1, n), lambda bi, hi, qi: (0, 0))
    kern = functools.partial(_da_kernel, t=t, seq=s)
    return pl.pallas_call(
        kern,
        out_shape=jax.ShapeDtypeStruct((b, s, DA_V), BF16),
        grid=(b, DA_HEADS, s // t),
        in_specs=[pl.BlockSpec(memory_space=pltpu.SMEM),
                  vec(HEAD_DIM), vec(HEAD_DIM), vec(HEAD_DIM), vec(HEAD_DIM),
                  pl.BlockSpec((1, t, LANES), lambda bi, hi, qi: (bi, qi, hi)),
                  pl.BlockSpec((1, s, LANES), lambda bi, hi, qi: (bi, 0, DA_HEADS + hi)),
                  pl.BlockSpec((1, s, LANES), lambda bi, hi, qi: (bi, 0, 2 * DA_HEADS + hi)),
                  vec(2 * HEAD_DIM)],
        out_specs=pl.BlockSpec((1, t, LANES), lambda bi, hi, qi: (bi, qi, hi)),
        scratch_shapes=[pltpu.VMEM((2, t, LANES), BF16), pltpu.VMEM((2, t, LANES), F32),
                        pltpu.VMEM((2, t, LANES), F32), pltpu.VMEM((2, t, 2 * HEAD_DIM), F32),
                        pltpu.VMEM((2 * DA_NEAR - 1, t, t), F32), pltpu.VMEM((s, 2 * LANES), BF16)],
        compiler_params=_params(("parallel", "parallel", "arbitrary")),
        name="diff_attention",
    )(tbl, lq1, lk1, lq2, lk2, proj3, proj3, proj3, subln_g)


def _wa_kernel(tbl_ref, sink_ref, q_ref, k_ref, v_ref, o_ref, bias_ref, *, tq, seq):
    q0 = pl.program_id(1) * tq
    win = tq + 2 * WINDOW
    ws = pl.multiple_of(jnp.clip(q0 - WINDOW, 0, seq - win), LANES)

    @pl.when(jnp.logical_and(pl.program_id(0) == 0, pl.program_id(1) == 0))
    def _():
        jj = lax.broadcasted_iota(I32, (1, win + tq), 1)
        for d in range(3):
            rel = jj - d * WINDOW - (tq - 1)
            valid = jnp.abs(rel) <= WINDOW
            bucket = _t5_bucket(jnp.clip(rel, -MAX_DISTANCE, MAX_DISTANCE))
            for hq in range(WA_Q_HEADS):
                row = jnp.where(valid, _bias_lookup(bucket, tbl_ref, DA_HEADS + hq) * LOG2E, -jnp.inf)
                bias_ref[d, hq] = _toeplitz(row, tq, win)

    variant = (q0 - ws) // WINDOW
    kw = k_ref[0, pl.ds(ws, win), :]
    vw = v_ref[0, pl.ds(ws, win), :]
    q = q_ref[0]
    low = lax.broadcasted_iota(I32, (tq, LANES), 1) < HEAD_DIM
    kdup = [jnp.concatenate([kw[:, g * HEAD_DIM:(g + 1) * HEAD_DIM]] * 2, axis=1) for g in range(2)]
    ones = jnp.ones((win, LANES), BF16)
    vext = [jnp.concatenate([vw[:, g * HEAD_DIM:(g + 1) * HEAD_DIM]] * 2 + [ones], axis=1) for g in range(2)]
    logits = []
    for hq in range(WA_Q_HEADS):
        qp = q[:, (hq // 2) * LANES:(hq // 2 + 1) * LANES]
        qm = jnp.where(low if hq % 2 == 0 else jnp.logical_not(low), qp, jnp.zeros_like(qp))
        logits.append(lax.dot_general(qm, kdup[hq // WA_REP], NT_DIMS, preferred_element_type=F32))
    outs = []
    for pair in range(WA_Q_HEADS // 2):
        g = (2 * pair) // WA_REP
        halves = []
        for half in range(2):
            hq = 2 * pair + half
            s = logits[hq] + bias_ref[variant, hq]
            sk = sink_ref[0, hq] * LOG2E
            m = jnp.maximum(jnp.full((tq, LANES), sk, F32), jnp.max(s, axis=1, keepdims=True))
            e = jnp.exp2(s - jnp.tile(m, (1, win // LANES))).astype(BF16)
            pv = jnp.dot(e, vext[g], preferred_element_type=F32)
            halves.append(pv[:, :LANES] / (pv[:, LANES:] + jnp.exp2(sk - m)))
        outs.append(jnp.where(low, halves[0], halves[1]))
    o_ref[0] = jnp.concatenate(outs, axis=1).astype(o_ref.dtype)


def _windowed_gqa(proj3, tbl, sink, tq):
    b, s, _ = proj3.shape
    kern = functools.partial(_wa_kernel, tq=tq, seq=s)
    kv_blk = (IN_COLS - 2 * LANES) // LANES
    return pl.pallas_call(
        kern,
        out_shape=jax.ShapeDtypeStruct((b, s, WA_Q), BF16),
        grid=(b, s // tq),
        in_specs=[pl.BlockSpec(memory_space=pltpu.SMEM),
                  pl.BlockSpec(memory_space=pltpu.SMEM),
                  pl.BlockSpec((1, tq, WA_Q), lambda bi, qi: (bi, qi, 3)),
                  pl.BlockSpec((1, s, LANES), lambda bi, qi: (bi, 0, kv_blk)),
                  pl.BlockSpec((1, s, LANES), lambda bi, qi: (bi, 0, kv_blk + 1))],
        out_specs=pl.BlockSpec((1, tq, WA_Q), lambda bi, qi: (bi, qi, 0)),
        scratch_shapes=[pltpu.VMEM((3, WA_Q_HEADS, tq, tq + 2 * WINDOW), F32)],
        compiler_params=_params(("arbitrary", "arbitrary")),
        name="windowed_gqa",
    )(tbl, sink, proj3, proj3, proj3)


def _oproj_kernel(a_ref, b_ref, x_ref, w1_ref, w2_ref, g_ref, wr_ref, x1_ref, h_ref, aff_ref):
    x1 = (x_ref[...] + jnp.dot(a_ref[...], w1_ref[...], preferred_element_type=F32)
          + jnp.dot(b_ref[...], w2_ref[...], preferred_element_type=F32))
    x1_ref[...] = x1
    h = _rms(x1, g_ref[...], RMS_EPS)
    h_hi = h.astype(BF16)
    h_ref[:, :D_MODEL] = h_hi
    h_lo = (h - h_hi.astype(F32)).astype(BF16)
    wr = wr_ref[...]
    w_hi = wr.astype(BF16)
    w_lo = (wr - w_hi.astype(F32)).astype(BF16)
    hh = jnp.dot(h_hi, jnp.concatenate([w_hi, w_lo], axis=1), preferred_element_type=F32)
    logits = (hh[:, :N_EXPERTS] + hh[:, N_EXPERTS:]
              + jnp.dot(h_lo, w_hi, preferred_element_type=F32))
    e = jnp.exp(logits - jnp.max(logits, axis=1, keepdims=True))
    aff = e / jnp.sum(e, axis=1, keepdims=True)
    aff_ref[...] = aff
    a_hi = aff.astype(BF16)
    a_lo = (aff - a_hi.astype(F32)).astype(BF16)
    pad = jnp.zeros((aff.shape[0], LANES - 2 * N_EXPERTS), BF16)
    h_ref[:, D_MODEL:] = jnp.concatenate([a_hi, a_lo, pad], axis=1)


def _oproj(a, b, x2, w1, w2, g, wr, tm):
    m = x2.shape[0]
    row = lambda n: pl.BlockSpec((tm, n), lambda i: (i, 0))
    full = lambda r, c: pl.BlockSpec((r, c), lambda i: (0, 0))
    return pl.pallas_call(
        _oproj_kernel,
        out_shape=(jax.ShapeDtypeStruct((m, D_MODEL), F32),
                   jax.ShapeDtypeStruct((m, D_ROUTED), BF16),
                   jax.ShapeDtypeStruct((m, N_EXPERTS), F32)),
        grid=(m // tm,),
        in_specs=[row(DA_V), row(WA_Q), row(D_MODEL), full(DA_V, D_MODEL), full(WA_Q, D_MODEL),
                  full(1, D_MODEL), full(D_MODEL, N_EXPERTS)],
        out_specs=(row(D_MODEL), row(D_ROUTED), row(N_EXPERTS)),
        compiler_params=_params(("parallel",)),
        name="oproj_router",
    )(a, b, x2, w1, w2, g, wr)


def _cumsum_lanes(x):
    n = x.shape[1]
    lane = lax.broadcasted_iota(I32, x.shape, 1)
    shift = 1
    while shift < n:
        x = x + jnp.where(lane >= shift, pltpu.roll(x, shift, 1), 0)
        shift *= 2
    return x


def _select_kernel(aff_ref, slot_ref, excl_ref, *, cap):
    aff = aff_ref[...]

    def count(mask):
        return jnp.sum(mask.astype(F32), axis=1, keepdims=True)

    def body(i, prefix):
        cand = prefix | jnp.left_shift(jnp.int32(1), 30 - i)
        return jnp.where(count(aff >= pltpu.bitcast(cand, F32)) >= cap, cand, prefix)

    thr = lax.fori_loop(0, 31, body, jnp.zeros((aff.shape[0], 1), I32))
    above = aff >= pltpu.bitcast(thr + 1, F32)
    edge = jnp.logical_and(aff >= pltpu.bitcast(thr, F32), jnp.logical_not(above))
    need = cap - count(above)
    edge_i = edge.astype(I32)
    edge_excl = (_cumsum_lanes(edge_i) - edge_i).astype(F32)
    sel = above | (edge & (edge_excl < need))
    sel_i = sel.astype(I32)
    excl = _cumsum_lanes(sel_i) - sel_i
    slot_ref[...] = jnp.where(sel, excl, -1)
    excl_ref[...] = excl


def _select(aff_t, cap):
    shp = jax.ShapeDtypeStruct(aff_t.shape, I32)
    return pl.pallas_call(
        functools.partial(_select_kernel, cap=cap),
        out_shape=(shp, shp),
        compiler_params=_params(None),
        name="expert_select",
    )(aff_t)


def _gather_kernel(cnt_ref, x_ref, slot_ref, xg_hbm, obuf, sem, carry_ref, xo, xsem, *, nb, rw):
    b = pl.program_id(0)
    tb = x_ref.shape[0]
    cur = b % 2
    grp = BF16_SUBLANES
    groups = rw // grp

    def lo_of(e, blk):
        return cnt_ref[e * (nb + 1) + blk]

    def base_of(e, blk):
        return pl.multiple_of((lo_of(e, blk) // grp) * grp, grp)

    def out_copy(e, blk, slab):
        return pltpu.make_async_copy(obuf.at[slab, pl.ds(e * rw, rw), :],
                                     xg_hbm.at[e, pl.ds(base_of(e, blk), rw), :], sem.at[slab, e])

    def pack(rel):
        rows = lax.broadcasted_iota(I32, (rw, tb), 0)
        return jnp.where(rel == rows, 1.0, 0.0).astype(BF16)

    @pl.when(b == 0)
    def _():
        carry_ref[...] = jnp.zeros(carry_ref.shape, carry_ref.dtype)
        xo[...] = jnp.zeros(xo.shape, xo.dtype)
        for e in range(N_EXPERTS):
            cp = pltpu.make_async_copy(xo, xg_hbm.at[e, pl.ds(xg_hbm.shape[1] - rw, rw), :], xsem)
            cp.start()
            cp.wait()

    x = x_ref[...]
    onehot = jnp.concatenate([pack(slot_ref[e:e + 1, :] - base_of(e, b)) for e in range(N_EXPERTS)], axis=0)
    obuf[cur] = jnp.dot(onehot, x, preferred_element_type=F32).astype(BF16)
    for e in range(N_EXPERTS):
        obuf[cur, pl.ds(e * rw, grp), :] += carry_ref[e]
        g_hi = (lo_of(e, b + 1) - base_of(e, b)) // grp
        row0 = pl.multiple_of(e * rw + jnp.minimum(g_hi, groups - 1) * grp, grp)
        kept = obuf[cur, pl.ds(row0, grp), :]
        carry_ref[e] = jnp.where(g_hi < groups, kept, jnp.zeros_like(kept))

    @pl.when(b > 0)
    def _():
        for e in range(N_EXPERTS):
            out_copy(e, b - 1, 1 - cur).wait()

    for e in range(N_EXPERTS):
        out_copy(e, b, cur).start(priority=e % 2)

    for e in range(N_EXPERTS):
        base = base_of(e, b)
        hi = lo_of(e, b + 1)
        n_extra = (jnp.maximum(hi - base - rw, 0) + rw - 1) // rw

        def extra(w, c, e=e, base=base, hi=hi):
            first = pl.multiple_of(base + (w + 1) * rw, grp)
            xo[...] = jnp.dot(pack(slot_ref[e:e + 1, :] - first), x, preferred_element_type=F32).astype(BF16)
            cp = pltpu.make_async_copy(xo, xg_hbm.at[e, pl.ds(first, rw), :], xsem)
            cp.start()
            cp.wait()
            g_hi = (hi - first) // grp
            row0 = pl.multiple_of(jnp.clip(g_hi, 0, groups - 1) * grp, grp)
            carry_ref[e] = jnp.where(jnp.logical_and(g_hi >= 0, g_hi < groups),
                                     xo[pl.ds(row0, grp), :], carry_ref[e])
            return c

        lax.fori_loop(0, n_extra, extra, 0)

    @pl.when(b == nb - 1)
    def _():
        for e in range(N_EXPERTS):
            out_copy(e, b, cur).wait()


def _gather(cnt, h2, slot, cap, tb):
    m, width = h2.shape
    nb = m // tb
    rw = (tb // 8 + max(tb // 16, 32) + BF16_SUBLANES) // BF16_SUBLANES * BF16_SUBLANES
    assert cap % BF16_SUBLANES == 0
    return pl.pallas_call(
        functools.partial(_gather_kernel, nb=nb, rw=rw),
        out_shape=jax.ShapeDtypeStruct((N_EXPERTS, cap + rw, width), BF16),
        grid_spec=pltpu.PrefetchScalarGridSpec(
            num_scalar_prefetch=1,
            grid=(nb,),
            in_specs=[pl.BlockSpec((tb, width), lambda b, c: (b, 0)),
                      pl.BlockSpec((N_EXPERTS, tb), lambda b, c: (0, b))],
            out_specs=pl.BlockSpec(memory_space=pl.ANY),
            scratch_shapes=[pltpu.VMEM((2, N_EXPERTS * rw, width), BF16),
                            pltpu.SemaphoreType.DMA((2, N_EXPERTS)),
                            pltpu.VMEM((N_EXPERTS, BF16_SUBLANES, width), BF16),
                            pltpu.VMEM((rw, width), BF16),
                            pltpu.SemaphoreType.DMA(())]),
        compiler_params=_params(("arbitrary",)),
        name="moe_gather",
    )(cnt, h2, slot)


def _ffn_kernel(x_ref, wg_ref, wu_ref, wd_ref, y_ref, wg_s, wu_s, wd_s):
    @pl.when(pl.program_id(1) == 0)
    def _():
        wg_s[...] = wg_ref[0].astype(BF16)
        wu_s[...] = wu_ref[0].astype(BF16)
        wd_s[...] = wd_ref[0].astype(BF16)

    x = x_ref[0, :, :D_MODEL]
    tail = x_ref[0, :, D_MODEL:].astype(F32)
    lane = lax.broadcasted_iota(I32, tail.shape, 1)
    e = pl.program_id(0)
    mine = jnp.logical_or(lane == e, lane == e + N_EXPERTS)
    gate = jnp.sum(jnp.where(mine, tail, 0.0), axis=1, keepdims=True)
    g = jnp.dot(x, wg_s[...], preferred_element_type=F32)
    u = jnp.dot(x, wu_s[...], preferred_element_type=F32)
    h = (g * jax.nn.sigmoid(g) * u).astype(BF16)
    y_ref[0] = (jnp.dot(h, wd_s[...], preferred_element_type=F32) * gate).astype(y_ref.dtype)


def _ffn(xg, wg, wu, wd, cap, tm):
    wspec = lambda r, c: pl.BlockSpec((1, r, c), lambda e, i: (e, 0, 0))
    return pl.pallas_call(
        _ffn_kernel,
        out_shape=jax.ShapeDtypeStruct((N_EXPERTS, cap, D_MODEL), BF16),
        grid=(N_EXPERTS, cap // tm),
        in_specs=[pl.BlockSpec((1, tm, D_ROUTED), lambda e, i: (e, i, 0)),
                  wspec(D_MODEL, EXPERT_FF), wspec(D_MODEL, EXPERT_FF), wspec(EXPERT_FF, D_MODEL)],
        out_specs=pl.BlockSpec((1, tm, D_MODEL), lambda e, i: (e, i, 0)),
        scratch_shapes=[pltpu.VMEM((D_MODEL, EXPERT_FF), BF16), pltpu.VMEM((D_MODEL, EXPERT_FF), BF16),
                        pltpu.VMEM((EXPERT_FF, D_MODEL), BF16)],
        compiler_params=_params(("arbitrary", "arbitrary")),
        name="moe_ffn",
    )(xg, wg, wu, wd)


def _combine_kernel(cnt_ref, x1_ref, slot_ref, fn_ref, y_hbm, o_ref,
                    buf, sem, xbuf, xsem, acc_ref, *, nb, cap, rw):
    b = pl.program_id(0)
    tb = x1_ref.shape[0]
    xr = xbuf.shape[0]

    def window_start(e, blk):
        lo = cnt_ref[e * (nb + 1) + blk]
        return pl.multiple_of(jnp.minimum((lo // BF16_SUBLANES) * BF16_SUBLANES, cap - rw), BF16_SUBLANES)

    def window_copy(e, start, slab):
        return pltpu.make_async_copy(y_hbm.at[e, pl.ds(start, rw), :],
                                     buf.at[slab, pl.ds(e * rw, rw), :], sem.at[slab, e])

    def fetch(blk, slab):
        for e in range(N_EXPERTS):
            window_copy(e, window_start(e, blk), slab).start(priority=e % 2)

    cur = b % 2

    @pl.when(b == 0)
    def _():
        fetch(0, 0)

    @pl.when(b + 1 < nb)
    def _():
        fetch(b + 1, 1 - cur)

    starts = [window_start(e, b) for e in range(N_EXPERTS)]
    for e in range(N_EXPERTS):
        window_copy(e, starts[e], cur).wait()

    width = N_EXPERTS * rw
    ex = (lax.broadcasted_iota(I32, (N_EXPERTS, width), 0)
          == lax.broadcasted_iota(I32, (N_EXPERTS, width), 1) // rw).astype(BF16)
    lane16 = lax.broadcasted_iota(I32, (1, N_EXPERTS), 1)
    st_vec = jnp.zeros((1, N_EXPERTS), I32)
    for e in range(N_EXPERTS):
        st_vec = jnp.where(lane16 == e, starts[e], st_vec)
    rel = jnp.clip(slot_ref[...] - st_vec, -1, rw).astype(F32).astype(BF16)
    row = (lax.broadcasted_iota(I32, (tb, width), 1) % rw).astype(F32)
    hit = jnp.dot(rel, ex, preferred_element_type=F32) == row
    onehot = jnp.where(hit, 1.0, 0.0).astype(BF16)
    acc_ref[...] = x1_ref[...] + jnp.dot(onehot, buf[cur], preferred_element_type=F32)

    xlane = lax.broadcasted_iota(I32, (tb, xr), 1)
    for e in range(N_EXPERTS):
        done = starts[e] + rw
        n_extra = (jnp.maximum(cnt_ref[e * (nb + 1) + b + 1] - done, 0) + xr - 1) // xr

        def extra(w, c, e=e, done=done):
            first = done + w * xr
            cst = pl.multiple_of(jnp.minimum(first, cap - xr), BF16_SUBLANES)
            cp = pltpu.make_async_copy(y_hbm.at[e, pl.ds(cst, xr), :], xbuf, xsem)
            cp.start()
            cp.wait()
            slot = slot_ref[:, e:e + 1]
            hit = jnp.logical_and(slot - cst == xlane, slot >= first)
            onehot = jnp.where(hit, 1.0, 0.0).astype(BF16)
            acc_ref[...] += jnp.dot(onehot, xbuf[...], preferred_element_type=F32)
            return c

        lax.fori_loop(0, n_extra, extra, 0)

    o_ref[...] = _rms(acc_ref[...], fn_ref[...], RMS_EPS)


def _combine(cnt, x1, slot_t, fn, y, tb):
    m = x1.shape[0]
    nb = m // tb
    cap = y.shape[1]
    rw = LANES // 2
    assert cap % BF16_SUBLANES == 0 and cap >= LANES
    return pl.pallas_call(
        functools.partial(_combine_kernel, nb=nb, cap=cap, rw=rw),
        out_shape=jax.ShapeDtypeStruct((m, D_MODEL), F32),
        grid_spec=pltpu.PrefetchScalarGridSpec(
            num_scalar_prefetch=1,
            grid=(nb,),
            in_specs=[pl.BlockSpec((tb, D_MODEL), lambda b, c: (b, 0)),
                      pl.BlockSpec((tb, N_EXPERTS), lambda b, c: (b, 0)),
                      pl.BlockSpec((1, D_MODEL), lambda b, c: (0, 0)),
                      pl.BlockSpec(memory_space=pl.ANY)],
            out_specs=pl.BlockSpec((tb, D_MODEL), lambda b, c: (b, 0)),
            scratch_shapes=[pltpu.VMEM((2, N_EXPERTS * rw, D_MODEL), BF16),
                            pltpu.SemaphoreType.DMA((2, N_EXPERTS)),
                            pltpu.VMEM((LANES, D_MODEL), BF16),
                            pltpu.SemaphoreType.DMA(()),
                            pltpu.VMEM((tb, D_MODEL), F32)]),
        compiler_params=_params(("arbitrary",)),
        name="moe_combine",
    )(cnt, x1, slot_t, fn, y)


def _pick(n, prefs):
    for p in prefs:
        if n % p == 0:
            return p
    raise ValueError(f"no tile in {prefs} divides {n}")


def _trunk(x, tbl, sink, lam, subln_g, norm1, w_in, w_o1, w_o2, norm2, wr, wg, wu, wd, fn):
    bsz, s, d = x.shape
    m = bsz * s
    cap = CAPACITY_FACTOR * m // N_EXPERTS
    x2 = x.reshape(m, d)
    proj = _inproj(x2, norm1, w_in, _pick(m, (1024, 512, 256)))
    proj3 = proj.reshape(bsz, s, IN_COLS)
    out_a = _diff_attention(proj3, tbl, *lam, subln_g, _pick(s, (512, 256)))
    out_b = _windowed_gqa(proj3, tbl, sink, _pick(s, (256,)))
    x1, h2, aff = _oproj(out_a.reshape(m, DA_V), out_b.reshape(m, WA_Q), x2, w_o1, w_o2,
                         norm2, wr, _pick(m, (1024, 512, 256)))
    slot, excl = _select(aff.T, cap)
    def block_counts(tb):
        return jnp.concatenate([excl[:, ::tb], jnp.full((N_EXPERTS, 1), cap, I32)], axis=1).reshape(-1)

    tb_gather = _pick(m, (512, 256))
    tb_combine = 256
    xg = _gather(block_counts(tb_gather), h2, slot, cap, tb_gather)
    y = _ffn(xg, wg, wu, wd, cap, _pick(cap, (1024, 512, 256, 128)))
    out = _combine(block_counts(tb_combine), x1, slot.T, fn, y, tb_combine)
    return out.reshape(bsz, s, d)


def kernel(x_prompt, x_sample, rel_bias, norm1, w_in, lam_q1, lam_k1, lam_q2, lam_k2, subln_g, sink,
           w_o, norm2, w_router, w_gate, w_up, w_down, final_norm):
    col = jnp.arange(IN_COLS)
    is_q = (col < DA_HEADS * 2 * HEAD_DIM) | ((col >= 3 * DA_V) & (col < 3 * DA_V + WA_Q))
    col_scale = jnp.where(is_q, HEAD_DIM ** -0.5 * LOG2E, 1.0).astype(F32)
    w_in_s = (w_in[0] * col_scale).astype(BF16)
    args = dict(
        tbl=rel_bias, sink=sink,
        lam=(lam_q1, lam_k1, lam_q2, lam_k2), subln_g=subln_g,
        norm1=norm1, w_in=w_in_s,
        w_o1=w_o[0, :DA_V].astype(BF16), w_o2=w_o[0, DA_V:].astype(BF16),
        norm2=norm2, wr=w_router[0],
        wg=w_gate[0], wu=w_up[0], wd=w_down[0],
        fn=final_norm.reshape(1, D_MODEL))
    return (_trunk(x_prompt, **args), _trunk(x_sample, **args))
```
